```python
import jax, jax.numpy as jnp
from jax import lax
import numpy as np

D_MODEL = 1024
BATCH = 8
SEQ = 2048
DEPTH = 4
DEC_BATCH = 32
DEC_SEQ = 8
PAST_LEN = 16384
PAGE_SIZE = 128

POOL_WINDOWS = (2, 4, 8, 16)
POOL_GROUPS = 4
POOL_GW = D_MODEL // 16
POOL_W = POOL_GROUPS * POOL_GW
POOL_KEEP = max(POOL_WINDOWS) - 1
MLA_HEADS = 8
QK_NOPE = D_MODEL // 16
QK_ROPE = D_MODEL // 32
V_HEAD = D_MODEL // 16
Q_LORA = 3 * D_MODEL // 8
KV_LORA = D_MODEL // 4
ROPE_THETA = 10000.0
MLA_SCALE = (QK_NOPE + QK_ROPE) ** -0.5
Q_BLOCK = 128
GM_CHUNK = 128
GM_GROUPS = 4
GM_GC = D_MODEL // 16
GM_W = GM_GROUPS * GM_GC
D_FF = ((8 * D_MODEL // 3 + 255) // 256) * 256
CONV_W = 3
N_GATES = 3
IN_SIZES = (POOL_W, Q_LORA, KV_LORA, QK_ROPE, GM_W, GM_W, N_GATES * D_MODEL)
IN_COLS = sum(IN_SIZES)
IN_SPLITS = tuple(int(v) for v in np.cumsum(IN_SIZES)[:-1])
ALPHA = (2.0 * DEPTH) ** 0.25
BETA = (8.0 * DEPTH) ** -0.25
RMS_EPS = 1e-6
LN_EPS = 1e-5

kernel_name = "hybrid_pool_mla_gmlp_deepnorm_step"


def rmsnorm(x, g):
    xf = x.astype(jnp.float32)
    y = xf * lax.rsqrt(jnp.mean(xf * xf, axis=-1, keepdims=True) + RMS_EPS)
    return (y * g).astype(x.dtype)


def layernorm(x, g, b):
    xf = x.astype(jnp.float32)
    mu = jnp.mean(xf, axis=-1, keepdims=True)
    xc = xf - mu
    var = jnp.mean(xc * xc, axis=-1, keepdims=True)
    return (xc * lax.rsqrt(var + LN_EPS) * g + b).astype(x.dtype)


def rope(x, pos):
    half = x.shape[-1] // 2
    inv = ROPE_THETA ** (-jnp.arange(half, dtype=jnp.float32) / half)
    ang = pos.astype(jnp.float32)[:, None] * inv[None, :]
    cos = jnp.cos(ang)[:, None, :]
    sin = jnp.sin(ang)[:, None, :]
    x1 = x[..., :half].astype(jnp.float32)
    x2 = x[..., half:].astype(jnp.float32)
    return jnp.concatenate([x1 * cos - x2 * sin, x1 * sin + x2 * cos], axis=-1).astype(x.dtype)


def pool_mix(z_pre, z, pos, pool_w, pool_scale):
    B, T, _ = z.shape
    P = z_pre.shape[1]
    ext = jnp.concatenate([z_pre.astype(z.dtype), z], axis=1)
    ef = ext.astype(jnp.float32)
    c = jnp.pad(jnp.cumsum(ef, axis=1), ((0, 0), (1, 0), (0, 0)))
    means = []
    for gi, w in enumerate(POOL_WINDOWS):
        sl = slice(gi * POOL_GW, (gi + 1) * POOL_GW)
        s = c[:, P + 1:P + 1 + T, sl] - c[:, P + 1 - w:P + 1 - w + T, sl]
        cnt = jnp.minimum(pos + 1, w).astype(jnp.float32)[None, :, None]
        means.append(s / cnt)
    d = (jnp.concatenate(means, axis=-1) - ef[:, P:]).astype(z.dtype)
    d = d.reshape(B, T, POOL_GROUPS, POOL_GW)
    y = jnp.einsum("btgc,gcd->btgd", d, pool_w).reshape(B, T, POOL_W) * pool_scale
    return y, ext[:, -POOL_KEEP:]


def mla_attend(q_lat, q_pe, q_pos, ckv, kpe, k_pos):
    s = (jnp.einsum("bqhc,bkc->bhqk", q_lat, ckv, preferred_element_type=jnp.float32)
         + jnp.einsum("bqhr,bkr->bhqk", q_pe, kpe, preferred_element_type=jnp.float32)) * MLA_SCALE
    s = jnp.where(k_pos[None, None, None, :] <= q_pos[None, None, :, None], s, -jnp.inf)
    p = jax.nn.softmax(s, axis=-1).astype(ckv.dtype)
    return jnp.einsum("bhqk,bkc->bqhc", p, ckv)


def prompt_attend(q_lat, q_pe, ckv, kpe, pos):
    B, S = q_lat.shape[0], q_lat.shape[1]
    nb = S // Q_BLOCK
    qlb = q_lat.reshape(B, nb, Q_BLOCK, MLA_HEADS, KV_LORA).swapaxes(0, 1)
    qpb = q_pe.reshape(B, nb, Q_BLOCK, MLA_HEADS, QK_ROPE).swapaxes(0, 1)
    posb = pos.reshape(nb, Q_BLOCK)

    def blk(args):
        ql, qp, qpos = args
        return mla_attend(ql, qp, qpos, ckv, kpe, pos)

    o = lax.map(blk, (qlb, qpb, posb))
    return o.swapaxes(0, 1).reshape(B, S, MLA_HEADS, KV_LORA)


def sample_attend(q_lat, q_pe, ckv, kpe, q_pos, ckv_past, kpe_past):
    keys_c = jnp.concatenate([ckv_past.astype(ckv.dtype), ckv], axis=1)
    keys_r = jnp.concatenate([kpe_past.astype(kpe.dtype), kpe], axis=1)
    k_pos = jnp.arange(keys_c.shape[1], dtype=jnp.int32)
    return mla_attend(q_lat, q_pe, q_pos, keys_c, keys_r, k_pos)


def gmlp_mix(u, v, v_g, v_b, w_s, b_s):
    B, T, _ = u.shape
    vn = layernorm(v, v_g, v_b)
    L = min(T, GM_CHUNK)
    nc = T // L
    ws = jnp.tril(w_s[:, :L, :L])
    vc = vn.reshape(B, nc, L, GM_GROUPS, GM_GC)
    s = jnp.einsum("gts,bnsgc->bntgc", ws, vc) + b_s[:, :L].T[None, None, :, :, None]
    return u * s.reshape(B, T, GM_W), vn


def token_mixer(h, pos, pool_pre, attend, lw):
    B, T, _ = h.shape
    z = h @ lw["w_in"]
    a_in, cq, ckv, kr, u, v, g = jnp.split(z, IN_SPLITS, axis=-1)
    ya, pool_tail = pool_mix(pool_pre, a_in, pos, lw["pool_w"], lw["pool_scale"])
    ya = ya @ lw["w_pool_out"]
    cq = rmsnorm(cq, lw["q_norm_g"])
    q = (cq @ lw["w_uq"]).reshape(B, T, MLA_HEADS, QK_NOPE + QK_ROPE)
    q_pe = rope(q[..., QK_NOPE:], pos)
    q_lat = jnp.einsum("bthn,chn->bthc", q[..., :QK_NOPE], lw["w_uk"])
    ckv = rmsnorm(ckv, lw["kv_norm_g"])
    kpe = rope(kr[:, :, None, :], pos)[:, :, 0, :]
    o_lat = attend(q_lat, q_pe, ckv, kpe)
    yb = jnp.einsum("bthc,chv->bthv", o_lat, lw["w_uv"]).reshape(B, T, MLA_HEADS * V_HEAD)
    yb = yb @ lw["w_mla_out"]
    yc, v_rows = gmlp_mix(u, v, lw["v_norm_g"], lw["v_norm_b"], lw["w_spatial"], lw["b_spatial"])
    yc = yc @ lw["w_gmlp_out"]
    gates = jax.nn.sigmoid(g.reshape(B, T, N_GATES, D_MODEL).astype(jnp.float32)).astype(h.dtype)
    m = gates[:, :, 0] * ya + gates[:, :, 1] * yb + gates[:, :, 2] * yc
    return m @ lw["w_out"], (ckv, kpe, pool_tail, v_rows)


def conv_ffn(h, prefix, w_up, conv_w, conv_b, w_down):
    a = h @ w_up
    T = a.shape[1]
    ext = jnp.concatenate([prefix.astype(a.dtype), a], axis=1)
    c = conv_b + conv_w[0] * ext[:, 0:T]
    for k in range(1, CONV_W):
        c = c + conv_w[k] * ext[:, k:k + T]
    gate, up = jnp.split(c, 2, axis=-1)
    y = (jax.nn.silu(gate) * up) @ w_down
    return y, ext[:, -(CONV_W - 1):]


def setup_inputs(seed: int = 0) -> dict:
    key = jax.random.key(seed)
    ks = iter(jax.random.split(key, 40))
    f32 = jnp.float32
    n_pages = PAST_LEN // PAGE_SIZE
    n_used = DEC_BATCH * n_pages
    n_pool = n_used + n_used // 4

    def nrm(shape, scale):
        return jax.random.normal(next(ks), shape, f32) * scale

    def gain(shape):
        return 1.0 + 0.1 * jax.random.normal(next(ks), shape, f32)

    x_prompt = nrm((BATCH, SEQ, D_MODEL), 1.0)
    x_sample = nrm((DEC_BATCH, DEC_SEQ, D_MODEL), 1.0)
    cache_ckv = nrm((DEPTH, n_pool, PAGE_SIZE, KV_LORA), 1.0)
    cache_kpe = nrm((DEPTH, n_pool, PAGE_SIZE, QK_ROPE), 1.0)
    state_pool = nrm((DEPTH, DEC_BATCH, POOL_KEEP, POOL_W), 1.0)
    state_ffn = nrm((DEPTH, DEC_BATCH, CONV_W - 1, 2 * D_FF), 1.0)
    perm = jax.random.permutation(next(ks), n_pool)
    page_table = perm[:n_used].reshape(DEC_BATCH, n_pages).astype(jnp.int32)
    return {
        "x_prompt": x_prompt,
        "x_sample": x_sample,
        "cache_ckv": cache_ckv,
        "cache_kpe": cache_kpe,
        "state_pool": state_pool,
        "state_ffn": state_ffn,
        "page_table": page_table,
        "w_in": nrm((DEPTH, D_MODEL, IN_COLS), D_MODEL ** -0.5),
        "pool_w": nrm((DEPTH, POOL_GROUPS, POOL_GW, POOL_GW), POOL_GW ** -0.5),
        "pool_scale": gain((DEPTH, POOL_W)),
        "w_pool_out": nrm((DEPTH, POOL_W, D_MODEL), POOL_W ** -0.5),
        "q_norm_g": gain((DEPTH, Q_LORA)),
        "w_uq": nrm((DEPTH, Q_LORA, MLA_HEADS * (QK_NOPE + QK_ROPE)), Q_LORA ** -0.5),
        "kv_norm_g": gain((DEPTH, KV_LORA)),
        "w_uk": nrm((DEPTH, KV_LORA, MLA_HEADS, QK_NOPE), KV_LORA ** -0.5),
        "w_uv": nrm((DEPTH, KV_LORA, MLA_HEADS, V_HEAD), KV_LORA ** -0.5),
        "w_mla_out": nrm((DEPTH, MLA_HEADS * V_HEAD, D_MODEL), (MLA_HEADS * V_HEAD) ** -0.5),
        "v_norm_g": gain((DEPTH, GM_W)),
        "v_norm_b": nrm((DEPTH, GM_W), 0.02),
        "w_spatial": nrm((DEPTH, GM_GROUPS, GM_CHUNK, GM_CHUNK), GM_CHUNK ** -0.5),
        "b_spatial": gain((DEPTH, GM_GROUPS, GM_CHUNK)),
        "w_gmlp_out": nrm((DEPTH, GM_W, D_MODEL), GM_W ** -0.5),
        "w_out": nrm((DEPTH, D_MODEL, D_MODEL), BETA * D_MODEL ** -0.5),
        "ln1_g": gain((DEPTH, D_MODEL)),
        "ln1_b": nrm((DEPTH, D_MODEL), 0.02),
        "w_up": nrm((DEPTH, D_MODEL, 2 * D_FF), D_MODEL ** -0.5),
        "conv_w": nrm((DEPTH, CONV_W, 2 * D_FF), CONV_W ** -0.5),
        "conv_b": nrm((DEPTH, 2 * D_FF), 0.02),
        "w_down": nrm((DEPTH, D_FF, D_MODEL), BETA * D_FF ** -0.5),
        "ln2_g": gain((DEPTH, D_MODEL)),
        "ln2_b": nrm((DEPTH, D_MODEL), 0.02),
    }


def reference(x_prompt, x_sample, cache_ckv, cache_kpe, state_pool, state_ffn, page_table,
              w_in, pool_w, pool_scale, w_pool_out, q_norm_g, w_uq, kv_norm_g, w_uk, w_uv,
              w_mla_out, v_norm_g, v_norm_b, w_spatial, b_spatial, w_gmlp_out, w_out,
              ln1_g, ln1_b, w_up, conv_w, conv_b, w_down, ln2_g, ln2_b):
    B, S, _ = x_prompt.shape
    DB, T, _ = x_sample.shape
    ps = cache_ckv.shape[2]
    n_pages = page_table.shape[1]
    past = n_pages * ps
    pos_p = jnp.arange(S, dtype=jnp.int32)
    pos_s = past + jnp.arange(T, dtype=jnp.int32)
    xp, xs = x_prompt, x_sample
    ckv_p_l, kpe_p_l, ckv_s_l, kpe_s_l = [], [], [], []
    pool_p_l, pool_s_l, ffn_p_l, ffn_s_l, gv_s_l = [], [], [], [], []
    for l in range(DEPTH):
        lw = {"w_in": w_in[l], "pool_w": pool_w[l], "pool_scale": pool_scale[l],
              "w_pool_out": w_pool_out[l], "q_norm_g": q_norm_g[l], "w_uq": w_uq[l],
              "kv_norm_g": kv_norm_g[l], "w_uk": w_uk[l], "w_uv": w_uv[l],
              "w_mla_out": w_mla_out[l], "v_norm_g": v_norm_g[l], "v_norm_b": v_norm_b[l],
              "w_spatial": w_spatial[l], "b_spatial": b_spatial[l],
              "w_gmlp_out": w_gmlp_out[l], "w_out": w_out[l]}
        mix_p, (ckv_p, kpe_p, tail_p, _) = token_mixer(
            xp, pos_p, jnp.zeros((B, POOL_KEEP, POOL_W), xp.dtype),
            lambda ql, qp, c, k: prompt_attend(ql, qp, c, k, pos_p), lw)
        xp = layernorm(ALPHA * xp + mix_p, ln1_g[l], ln1_b[l])
        ffn_p, conv_tail_p = conv_ffn(xp, jnp.zeros((B, CONV_W - 1, 2 * D_FF), xp.dtype),
                                      w_up[l], conv_w[l], conv_b[l], w_down[l])
        xp = layernorm(ALPHA * xp + ffn_p, ln2_g[l], ln2_b[l])
        ckv_past = cache_ckv[l, page_table].reshape(DB, past, KV_LORA)
        kpe_past = cache_kpe[l, page_table].reshape(DB, past, QK_ROPE)
        mix_s, (ckv_s, kpe_s, tail_s, v_s) = token_mixer(
            xs, pos_s, state_pool[l],
            lambda ql, qp, c, k: sample_attend(ql, qp, c, k, pos_s, ckv_past, kpe_past), lw)
        xs = layernorm(ALPHA * xs + mix_s, ln1_g[l], ln1_b[l])
        ffn_s, conv_tail_s = conv_ffn(xs, state_ffn[l], w_up[l], conv_w[l], conv_b[l], w_down[l])
        xs = layernorm(ALPHA * xs + ffn_s, ln2_g[l], ln2_b[l])
        ckv_p_l.append(ckv_p)
        kpe_p_l.append(kpe_p)
        ckv_s_l.append(ckv_s)
        kpe_s_l.append(kpe_s)
        pool_p_l.append(tail_p)
        pool_s_l.append(tail_s)
        ffn_p_l.append(conv_tail_p)
        ffn_s_l.append(conv_tail_s)
        gv_s_l.append(v_s)
    n_new_pages = (B * S) // ps
    new_ckv_prompt = jnp.stack(ckv_p_l).reshape(DEPTH, n_new_pages, ps, KV_LORA)
    new_kpe_prompt = jnp.stack(kpe_p_l).reshape(DEPTH, n_new_pages, ps, QK_ROPE)
    new_ckv_sample = jnp.stack(ckv_s_l)
    new_kpe_sample = jnp.stack(kpe_s_l)
    new_pool_prompt = jnp.stack(pool_p_l)
    new_pool_sample = jnp.stack(pool_s_l)
    new_ffn_prompt = jnp.stack(ffn_p_l)
    new_ffn_sample = jnp.stack(ffn_s_l)
    new_gmlp_v_sample = jnp.stack(gv_s_l)
    return (xp, xs, new_ckv_prompt, new_kpe_prompt, new_ckv_sample, new_kpe_sample,
            new_pool_prompt, new_pool_sample, new_ffn_prompt, new_ffn_sample, new_gmlp_v_sample)
```

```python
import functools

import jax
import jax.numpy as jnp
import numpy as np
from jax import lax
from jax.experimental import pallas as pl
from jax.experimental.pallas import tpu as pltpu

BF = jnp.bfloat16
F32 = jnp.float32

POOL_WINDOWS = (2, 4, 8, 16)
POOL_KEEP = max(POOL_WINDOWS) - 1
N_HEADS = 8
ROPE_THETA = 10000.0
RMS_EPS = 1e-6
LN_EPS = 1e-5
CONV_W = 3
SUBLANES = 8
LANES = 128
FF_CHUNK = 256
NEG_BIG = -1e30
VMEM_LIMIT = 56 * 1024 * 1024


def _params(*sem):
    return pltpu.CompilerParams(dimension_semantics=sem, vmem_limit_bytes=VMEM_LIMIT)


def _const_spec(shape):
    nd = len(shape)
    return pl.BlockSpec(shape, lambda *_: (0,) * nd)


def _rmsnorm(x, g):
    return x * lax.rsqrt(jnp.mean(x * x, axis=-1, keepdims=True) + RMS_EPS) * g


def _layernorm(x, g, b):
    mu = jnp.mean(x, axis=-1, keepdims=True)
    xc = x - mu
    var = jnp.mean(xc * xc, axis=-1, keepdims=True)
    return xc * lax.rsqrt(var + LN_EPS) * g + b


def _dot(a, b):
    return jnp.dot(a, b, preferred_element_type=F32)


def _dot_nt(a, b):
    return lax.dot_general(a, b, (((1,), (1,)), ((), ())), preferred_element_type=F32)


def _shift_rows(y3, prev3, k):
    t = lax.broadcasted_iota(jnp.int32, y3.shape, 1)
    return jnp.where(t >= k, pltpu.roll(y3, k, 1), pltpu.roll(prev3, k, 1))


def _prev_groups(y3, first):
    if y3.shape[0] == 1:
        return first
    return jnp.concatenate([first, y3[:-1]], axis=0)


def _in_proj_body(x_ref, w_ref, qg_ref, kvg_ref, vg_ref, vb_ref, cos_ref, sin_ref,
                  a_ref, cq_ref, ckv_ref, kpe_ref, kv_ref, u_ref, vn_ref, *, dims):
    pw, ql, kl, rp, gw = dims
    z = _dot(x_ref[...].astype(BF), w_ref[...])
    o = 0
    a_ref[...] = z[:, o:o + pw]
    o += pw
    cq_ref[...] = _rmsnorm(z[:, o:o + ql], qg_ref[...]).astype(BF)
    o += ql
    ckvn = _rmsnorm(z[:, o:o + kl], kvg_ref[...])
    ckv_ref[...] = ckvn
    o += kl
    u_ref[...] = z[:, o:o + gw].astype(BF)
    o += gw
    vn_ref[...] = _layernorm(z[:, o:o + gw], vg_ref[...], vb_ref[...]).astype(vn_ref.dtype)
    o += gw
    kr = z[:, o:o + LANES]
    kpe = kr * cos_ref[...] + pltpu.roll(kr, LANES - rp, 1) * sin_ref[...]
    kpe_ref[...] = kpe[:, :rp]
    tiled = kpe
    for j in range(1, LANES // rp):
        tiled = tiled + pltpu.roll(kpe, j * rp, 1)
    tb = tiled.astype(BF)
    kv_ref[:, 0:kl] = ckvn.astype(BF)
    for j in range(N_HEADS * rp // LANES):
        kv_ref[:, kl + j * LANES:kl + (j + 1) * LANES] = tb


def _in_proj(x2d, lw, cos_k, sin_k, *, tm, vn_dtype):
    rows, dm = x2d.shape
    dims = lw["dims"]
    pw, ql, kl, rp, gw = dims
    ntab = cos_k.shape[0] // tm
    ncol = lw["w_in_main"].shape[1]
    row_spec = lambda w: pl.BlockSpec((tm, w), lambda i: (i, 0))
    tab_spec = pl.BlockSpec((tm, LANES), lambda i: (i % ntab, 0))
    return pl.pallas_call(
        functools.partial(_in_proj_body, dims=dims),
        grid=(rows // tm,),
        in_specs=[row_spec(dm), _const_spec((dm, ncol)), _const_spec((1, ql)), _const_spec((1, kl)),
                  _const_spec((1, gw)), _const_spec((1, gw)), tab_spec, tab_spec],
        out_specs=[row_spec(pw), row_spec(ql), row_spec(kl), row_spec(rp), row_spec(kl + N_HEADS * rp),
                   row_spec(gw), row_spec(gw)],
        out_shape=[jax.ShapeDtypeStruct((rows, pw), F32), jax.ShapeDtypeStruct((rows, ql), BF),
                   jax.ShapeDtypeStruct((rows, kl), F32), jax.ShapeDtypeStruct((rows, rp), F32),
                   jax.ShapeDtypeStruct((rows, kl + N_HEADS * rp), BF), jax.ShapeDtypeStruct((rows, gw), BF),
                   jax.ShapeDtypeStruct((rows, gw), vn_dtype)],
        compiler_params=_params("arbitrary"),
        name="in_proj",
    )(x2d, lw["w_in_main"], lw["q_norm_g"], lw["kv_norm_g"], lw["v_norm_g"], lw["v_norm_b"], cos_k, sin_k)


def _q_body(cq_ref, w_ref, wuk_ref, cos_ref, sin_ref, *out_refs, nope, rp, kl, scale, combined):
    hn = N_HEADS * nope
    hr = N_HEADS * rp
    q = _dot(cq_ref[...], w_ref[...])
    qpe = (q[:, hn:hn + hr] * cos_ref[...] + q[:, hn + hr:hn + 2 * hr] * sin_ref[...]) * scale
    if not combined:
        out_refs[1][...] = qpe
    else:
        head_of_lane = lax.broadcasted_iota(jnp.int32, qpe.shape, 1) // rp
    for h in range(N_HEADS):
        pair = q[:, (h // 2) * LANES:(h // 2 + 1) * LANES].astype(BF)
        qlat = _dot(pair, wuk_ref[h]) * scale
        if combined:
            out_refs[0][h, :, 0:kl] = qlat.astype(BF)
            out_refs[0][h, :, kl:kl + hr] = jnp.where(head_of_lane == h, qpe, 0.0).astype(BF)
        else:
            out_refs[0][h] = qlat


def _q_proj(cq, lw, cos_q, sin_q, *, tm, combined):
    rows, ql = cq.shape
    pw, _, kl, rp, gw = lw["dims"]
    nope = lw["nope"]
    hr = N_HEADS * rp
    ntab = cos_q.shape[0] // tm
    tab_spec = pl.BlockSpec((tm, hr), lambda i: (i % ntab, 0))
    if combined:
        out_specs = [pl.BlockSpec((N_HEADS, tm, kl + hr), lambda i: (0, i, 0))]
        out_shape = [jax.ShapeDtypeStruct((N_HEADS, rows, kl + hr), BF)]
    else:
        out_specs = [pl.BlockSpec((N_HEADS, tm, kl), lambda i: (0, i, 0)), pl.BlockSpec((tm, hr), lambda i: (i, 0))]
        out_shape = [jax.ShapeDtypeStruct((N_HEADS, rows, kl), F32), jax.ShapeDtypeStruct((rows, hr), F32)]
    return pl.pallas_call(
        functools.partial(_q_body, nope=nope, rp=rp, kl=kl, scale=lw["scale"], combined=combined),
        grid=(rows // tm,),
        in_specs=[pl.BlockSpec((tm, ql), lambda i: (i, 0)), _const_spec(lw["w_q"].shape),
                  _const_spec(lw["w_uk_pad"].shape), tab_spec, tab_spec],
        out_specs=out_specs, out_shape=out_shape,
        compiler_params=_params("arbitrary"),
        name="q_proj",
    )(cq, lw["w_q"], lw["w_uk_pad"], cos_q, sin_q)


def _heads_out(o, wuv_ref, rows_per_head, full_m):
    outs = []
    ob = o.astype(BF)
    for j in range(N_HEADS // 2):
        acc = None
        for h in (2 * j, 2 * j + 1):
            sl = slice(h * rows_per_head, (h + 1) * rows_per_head)
            if full_m:
                y = _dot(ob, wuv_ref[h])[sl]
            else:
                y = _dot(ob[sl], wuv_ref[h])
            acc = y if acc is None else acc + y
        outs.append(acc)
    return outs


def _attn_body(q_ref, kv_ref, wuv_ref, o_ref, m_ref, l_ref, acc_ref, *, tq, kl):
    qi = pl.program_id(1)
    ki = pl.program_id(2)
    rows = N_HEADS * tq

    @pl.when(ki == 0)
    def _():
        m_ref[...] = jnp.full(m_ref.shape, NEG_BIG, F32)
        l_ref[...] = jnp.zeros(l_ref.shape, F32)
        acc_ref[...] = jnp.zeros(acc_ref.shape, F32)

    def step(masked):
        q = q_ref[...].reshape(rows, q_ref.shape[-1])
        kv = kv_ref[...]
        s = _dot_nt(q, kv)
        if masked:
            qpos = lax.broadcasted_iota(jnp.int32, (N_HEADS, tq, tq), 1).reshape(rows, tq)
            kpos = lax.broadcasted_iota(jnp.int32, (rows, tq), 1)
            s = jnp.where(kpos <= qpos, s, NEG_BIG)
        m_prev = m_ref[...]
        m_new = jnp.maximum(m_prev, jnp.max(s, axis=-1, keepdims=True))
        alpha = jnp.exp(m_prev - m_new)
        p = jnp.exp(s - m_new)
        l_ref[...] = alpha * l_ref[...] + jnp.sum(p, axis=-1, keepdims=True)
        acc_ref[...] = alpha * acc_ref[...] + _dot(p.astype(BF), kv[:, :kl])
        m_ref[...] = m_new

    @pl.when(ki < qi)
    def _():
        step(False)

    @pl.when(ki == qi)
    def _():
        step(True)
        o = acc_ref[...] * (1.0 / l_ref[...])
        for j, y in enumerate(_heads_out(o, wuv_ref, tq, full_m=False)):
            o_ref[:, j * LANES:(j + 1) * LANES] = y.astype(o_ref.dtype)


def _prompt_attention(q, kv, lw, *, batch, seq, tq):
    kl = lw["dims"][2]
    width = q.shape[-1]
    nq = seq // tq
    hv = lw["w_uv_pad"].shape[2] * N_HEADS // 2
    return pl.pallas_call(
        functools.partial(_attn_body, tq=tq, kl=kl),
        grid=(batch, nq, nq),
        in_specs=[pl.BlockSpec((N_HEADS, tq, width), lambda b, i, k: (0, b * nq + i, 0)),
                  pl.BlockSpec((tq, width), lambda b, i, k: (b * nq + jnp.minimum(k, i), 0)),
                  _const_spec(lw["w_uv_pad"].shape)],
        out_specs=pl.BlockSpec((tq, hv), lambda b, i, k: (b * nq + i, 0)),
        out_shape=jax.ShapeDtypeStruct((batch * seq, hv), BF),
        scratch_shapes=[pltpu.VMEM((N_HEADS * tq, 1), F32), pltpu.VMEM((N_HEADS * tq, 1), F32),
                        pltpu.VMEM((N_HEADS * tq, kl), F32)],
        compiler_params=_params("arbitrary", "arbitrary", "arbitrary"),
        name="prompt_attention",
    )(q, kv, lw["w_uv_pad"])


def _decode_body(pt_ref, qlat_ref, qpe_ref, ckvn_ref, kpen_ref, wuv_ref, *rest, n_grp, tdec, rp, kl):
    ckv_refs = rest[:n_grp]
    kpe_refs = rest[n_grp:2 * n_grp]
    o_ref, q_s, qp_s, m_ref, l_ref, acc_ref = rest[2 * n_grp:]
    p_id = pl.program_id(1)
    rows = N_HEADS * tdec

    @pl.when(p_id == 0)
    def _():
        m_ref[...] = jnp.full(m_ref.shape, NEG_BIG, F32)
        l_ref[...] = jnp.zeros(l_ref.shape, F32)
        acc_ref[...] = jnp.zeros(acc_ref.shape, F32)
        q_s[...] = qlat_ref[...].reshape(rows, kl).astype(BF)
        qpe = qpe_ref[...]
        qp_s[...] = jnp.concatenate([qpe[:, h * rp:(h + 1) * rp] for h in range(N_HEADS)], axis=0).astype(BF)

    def update(s, keys):
        m_prev = m_ref[...]
        m_new = jnp.maximum(m_prev, jnp.max(s, axis=-1, keepdims=True))
        alpha = jnp.exp(m_prev - m_new)
        p = jnp.exp(s - m_new)
        l_ref[...] = alpha * l_ref[...] + jnp.sum(p, axis=-1, keepdims=True)
        acc = alpha * acc_ref[...]
        off = 0
        pb = p.astype(BF)
        for k in keys:
            acc = acc + _dot(pb[:, off:off + k.shape[0]], k)
            off += k.shape[0]
        acc_ref[...] = acc
        m_ref[...] = m_new

    q = q_s[...]
    qp = qp_s[...]
    keys = [r[...].astype(BF) for r in ckv_refs]
    s = jnp.concatenate([_dot_nt(q, k) + _dot_nt(qp, r[...].astype(BF)) for k, r in zip(keys, kpe_refs)], axis=1)
    update(s, keys)

    @pl.when(p_id == pl.num_programs(1) - 1)
    def _():
        pad = 2 * tdec
        kn = jnp.concatenate([ckvn_ref[...], jnp.zeros((pad - tdec, kl), F32)], axis=0).astype(BF)
        kpn = jnp.concatenate([kpen_ref[...], jnp.zeros((pad - tdec, rp), F32)], axis=0).astype(BF)
        sn = _dot_nt(q, kn) + _dot_nt(qp, kpn)
        qpos = lax.broadcasted_iota(jnp.int32, (N_HEADS, tdec, pad), 1).reshape(rows, pad)
        kpos = lax.broadcasted_iota(jnp.int32, (rows, pad), 1)
        sn = jnp.where(kpos <= qpos, sn, NEG_BIG)
        update(sn, [kn])
        o = acc_ref[...] * (1.0 / l_ref[...])
        for j, y in enumerate(_heads_out(o, wuv_ref, tdec, full_m=True)):
            o_ref[:, j * LANES:(j + 1) * LANES] = y.astype(o_ref.dtype)


def _sample_attention(qlat, qpe, ckvn, kpen, cache_ckv, cache_kpe, page_table, layer, lw, *, dbatch, tdec, n_grp):
    _, _, kl, rp, _ = lw["dims"]
    n_pages = page_table.shape[1]
    ps = cache_ckv.shape[2]
    hv = lw["w_uv_pad"].shape[2] * N_HEADS // 2
    hr = N_HEADS * rp
    rows = N_HEADS * tdec

    def page_spec(width, j):
        return pl.BlockSpec((None, None, ps, width), lambda b, p, pt: (layer, pt[b, p * n_grp + j], 0, 0))

    in_specs = [pl.BlockSpec((N_HEADS, tdec, kl), lambda b, p, pt: (0, b, 0)),
                pl.BlockSpec((tdec, hr), lambda b, p, pt: (b, 0)),
                pl.BlockSpec((tdec, kl), lambda b, p, pt: (b, 0)),
                pl.BlockSpec((tdec, rp), lambda b, p, pt: (b, 0)),
                _const_spec(lw["w_uv_pad"].shape)]
    in_specs += [page_spec(kl, j) for j in range(n_grp)]
    in_specs += [page_spec(rp, j) for j in range(n_grp)]
    grid_spec = pltpu.PrefetchScalarGridSpec(
        num_scalar_prefetch=1, grid=(dbatch, n_pages // n_grp), in_specs=in_specs,
        out_specs=pl.BlockSpec((tdec, hv), lambda b, p, pt: (b, 0)),
        scratch_shapes=[pltpu.VMEM((rows, kl), BF), pltpu.VMEM((rows, rp), BF), pltpu.VMEM((rows, 1), F32),
                        pltpu.VMEM((rows, 1), F32), pltpu.VMEM((rows, kl), F32)])
    return pl.pallas_call(
        functools.partial(_decode_body, n_grp=n_grp, tdec=tdec, rp=rp, kl=kl),
        grid_spec=grid_spec,
        out_shape=jax.ShapeDtypeStruct((dbatch * tdec, hv), BF),
        compiler_params=_params("arbitrary", "arbitrary"),
        name="sample_attention",
    )(page_table, qlat, qpe, ckvn, kpen, lw["w_uv_pad"], *([cache_ckv] * n_grp), *([cache_kpe] * n_grp))


def _window_select(sums, shape):
    gw = shape[-1] // len(POOL_WINDOWS)
    grp = lax.broadcasted_iota(jnp.int32, shape, len(shape) - 1) // gw
    out = sums[-1]
    for gi in range(len(POOL_WINDOWS) - 2, -1, -1):
        out = jnp.where(grp == gi, sums[gi], out)
    return out


def _merge_tail(x, d, u, vn, yb_pre, wg_ref, pbd_ref, psc_ref, wpo_ref, ws_ref, bs_ref, wgo_ref, wmo_ref,
                wo_ref, lng_ref, lnb_ref, alpha):
    dm = x.shape[1]
    xb = x.astype(BF)
    ya = _dot(d.astype(BF), pbd_ref[...]) * psc_ref[...]
    ya = _dot(ya.astype(BF), wpo_ref[...])
    m = jax.nn.sigmoid(_dot(xb, wg_ref[:, 0:dm])) * ya
    yb = _dot(yb_pre, wmo_ref[...])
    m = m + jax.nn.sigmoid(_dot(xb, wg_ref[:, dm:2 * dm])) * yb
    n_g, clen, _ = ws_ref.shape
    gc = vn.shape[1] // n_g
    grp = lax.broadcasted_iota(jnp.int32, (clen, vn.shape[1]), 1) // gc
    parts = []
    for c in range(vn.shape[0] // clen):
        vc = vn[c * clen:(c + 1) * clen]
        s = _dot(ws_ref[n_g - 1], vc)
        for g in range(n_g - 2, -1, -1):
            s = jnp.where(grp == g, _dot(ws_ref[g], vc), s)
        parts.append(s + bs_ref[...])
    s = parts[0] if len(parts) == 1 else jnp.concatenate(parts, axis=0)
    yc = _dot((u.astype(F32) * s).astype(BF), wgo_ref[...])
    m = m + jax.nn.sigmoid(_dot(xb, wg_ref[:, 2 * dm:3 * dm])) * yc
    y = alpha * x + _dot(m.astype(BF), wo_ref[...])
    return _layernorm(y, lng_ref[...], lnb_ref[...])


def _merge_prompt_body(x_ref, a_ref, aprev_ref, icnt_ref, u_ref, vn_ref, yb_ref, *rest, tiles_per_seq, alpha):
    w_refs, o_ref = rest[:-1], rest[-1]
    i = pl.program_id(0)
    a = a_ref[...]
    tm, pw = a.shape
    hist = jnp.where(i % tiles_per_seq == 0, 0.0, aprev_ref[...])
    n_hist = hist.shape[0] // SUBLANES
    ext = jnp.concatenate([hist, a], axis=0).reshape(tm // SUBLANES + n_hist, SUBLANES, pw)
    zero = jnp.zeros((1, SUBLANES, pw), F32)
    s2 = ext + _shift_rows(ext, _prev_groups(ext, zero), 1)
    s4 = s2 + _shift_rows(s2, _prev_groups(s2, zero), 2)
    s8 = s4 + _shift_rows(s4, _prev_groups(s4, zero), 4)
    s16 = s8 + _prev_groups(s8, zero)
    sel = _window_select([s[n_hist:] for s in (s2, s4, s8, s16)], (tm // SUBLANES, SUBLANES, pw))
    d = sel.reshape(tm, pw) * icnt_ref[...] - a
    o_ref[...] = _merge_tail(x_ref[...], d, u_ref[...], vn_ref[...], yb_ref[...], *w_refs, alpha)


def _merge_sample_body(x_ref, ext_ref, icnt_ref, u_ref, vn_ref, yb_ref, *rest, tdec, alpha):
    w_refs, o_ref = rest[:-1], rest[-1]
    nb, ext_len, pw = ext_ref.shape
    acc = None
    sums = []
    for j in range(max(POOL_WINDOWS)):
        cur = ext_ref[:, ext_len - tdec - j:ext_len - j, :]
        acc = cur if acc is None else acc + cur
        if j + 1 in POOL_WINDOWS:
            sums.append(acc)
    tok = ext_ref[:, ext_len - tdec:ext_len, :]
    sel = _window_select(sums, (nb, tdec, pw))
    d = sel.reshape(nb * tdec, pw) * icnt_ref[...] - tok.reshape(nb * tdec, pw)
    o_ref[...] = _merge_tail(x_ref[...], d, u_ref[...], vn_ref[...], yb_ref[...], *w_refs, alpha)


def _merge_weights(lw):
    return [lw["w_gates"], lw["pool_bd"], lw["pool_scale"], lw["w_pool_out"], lw["ws"], lw["bs"],
            lw["w_gmlp_out"], lw["w_mla_out"], lw["w_out"], lw["ln1_g"], lw["ln1_b"]]


def _merge_prompt(x2d, a_in, icnt, u, vn, yb_pre, lw, *, tm, seq):
    rows, dm = x2d.shape
    pw, _, _, _, gw = lw["dims"]
    tiles_per_seq = seq // tm
    hist_rows = 2 * SUBLANES
    hist_per_tile = tm // hist_rows
    weights = _merge_weights(lw)
    row_spec = lambda w: pl.BlockSpec((tm, w), lambda i: (i, 0))
    in_specs = [row_spec(dm), row_spec(pw),
                pl.BlockSpec((hist_rows, pw), lambda i: (jnp.maximum(i * hist_per_tile - 1, 0), 0)),
                pl.BlockSpec((tm, pw), lambda i: (i % tiles_per_seq, 0)),
                row_spec(gw), row_spec(gw), row_spec(yb_pre.shape[1])]
    in_specs += [_const_spec(w.shape) for w in weights]
    return pl.pallas_call(
        functools.partial(_merge_prompt_body, tiles_per_seq=tiles_per_seq, alpha=lw["alpha"]),
        grid=(rows // tm,), in_specs=in_specs, out_specs=row_spec(dm),
        out_shape=jax.ShapeDtypeStruct((rows, dm), F32),
        compiler_params=_params("arbitrary"),
        name="merge_prompt",
    )(x2d, a_in, a_in, icnt, u, vn, yb_pre, *weights)


def _merge_sample(x2d, ext, icnt, u, vn, yb_pre, lw, *, tdec):
    rows, dm = x2d.shape
    weights = [lw["w_gates"], lw["pool_bd"], lw["pool_scale"], lw["w_pool_out"], lw["ws_dec"], lw["bs_dec"],
               lw["w_gmlp_out"], lw["w_mla_out"], lw["w_out"], lw["ln1_g"], lw["ln1_b"]]
    args = [x2d, ext, icnt, u, vn, yb_pre] + weights
    return pl.pallas_call(
        functools.partial(_merge_sample_body, tdec=tdec, alpha=lw["alpha"]),
        grid=(1,), in_specs=[_const_spec(a.shape) for a in args], out_specs=_const_spec((rows, dm)),
        out_shape=jax.ShapeDtypeStruct((rows, dm), F32),
        compiler_params=_params("arbitrary"),
        name="merge_sample",
    )(*args)


def _ffn_body(x_ref, *rest, tiles_per_seq, alpha, has_prefix):
    if has_prefix:
        prefix_ref, rest = rest[0], rest[1:]
    wup_ref, cw_ref, cb_ref, wdn_ref, lng_ref, lnb_ref, o_ref, tail_ref, acc_ref, carry_ref = rest
    n_chunks, _, cw2 = wup_ref.shape
    half = cw2 // 2
    tm = x_ref.shape[0]
    grp = tm // SUBLANES
    x = x_ref[...]
    xb = x.astype(BF)
    acc_ref[...] = jnp.zeros(acc_ref.shape, F32)

    if not has_prefix:
        @pl.when(pl.program_id(0) % tiles_per_seq == 0)
        def _():
            carry_ref[...] = jnp.zeros(carry_ref.shape, F32)

    def chunk(c, carry):
        a3 = _dot(xb, wup_ref[c]).reshape(grp, SUBLANES, cw2)
        if has_prefix:
            prev = prefix_ref[c].reshape(grp, SUBLANES, cw2)
        else:
            prev = _prev_groups(a3, carry_ref[c][None])
            carry_ref[c] = a3[grp - 1]
        w = cw_ref[c]
        conv = (cb_ref[c] + w[0:1] * _shift_rows(a3, prev, 2) + w[1:2] * _shift_rows(a3, prev, 1)
                + w[2:3] * a3).reshape(tm, cw2)
        h = jax.nn.silu(conv[:, :half]) * conv[:, half:]
        acc_ref[...] += _dot(h.astype(BF), wdn_ref[c])
        if has_prefix:
            tail_ref[c] = a3[:, SUBLANES - (CONV_W - 1):, :]
        else:
            tail_ref[0, c] = a3[grp - 1, SUBLANES - (CONV_W - 1):, :]
        return carry

    lax.fori_loop(0, n_chunks, chunk, 0)
    o_ref[...] = _layernorm(alpha * x + acc_ref[...], lng_ref[...], lnb_ref[...])


def _ffn(x2d, prefix, lw, *, tm, seq, n_seq):
    rows, dm = x2d.shape
    wup, cw, cb, wdn = lw["w_up_c"], lw["conv_w_c"], lw["conv_b_c"], lw["w_down_c"]
    n_chunks, _, cw2 = wup.shape
    has_prefix = prefix is not None
    tiles_per_seq = max(seq // tm, 1)
    keep = CONV_W - 1
    in_specs = [pl.BlockSpec((tm, dm), lambda i: (i, 0))]
    args = [x2d]
    if has_prefix:
        in_specs.append(_const_spec(prefix.shape))
        args.append(prefix)
        tail_shape = (n_chunks, rows // SUBLANES, keep, cw2)
        tail_spec = _const_spec(tail_shape)
    else:
        tail_shape = (n_seq, n_chunks, keep, cw2)
        tail_spec = pl.BlockSpec((1, n_chunks, keep, cw2), lambda i: (i // tiles_per_seq, 0, 0, 0))
    weights = [wup, cw, cb, wdn, lw["ln2_g"], lw["ln2_b"]]
    in_specs += [_const_spec(w.shape) for w in weights]
    return pl.pallas_call(
        functools.partial(_ffn_body, tiles_per_seq=tiles_per_seq, alpha=lw["alpha"], has_prefix=has_prefix),
        grid=(rows // tm,), in_specs=in_specs,
        out_specs=[pl.BlockSpec((tm, dm), lambda i: (i, 0)), tail_spec],
        out_shape=[jax.ShapeDtypeStruct((rows, dm), F32), jax.ShapeDtypeStruct(tail_shape, F32)],
        scratch_shapes=[pltpu.VMEM((tm, dm), F32), pltpu.VMEM((n_chunks, SUBLANES, cw2), F32)],
        compiler_params=_params("arbitrary"),
        name="ffn_sample" if has_prefix else "ffn_prompt",
    )(*args, *weights)


def _ff_chunked(v, d_ff):
    n = d_ff // FF_CHUNK
    v = v.reshape(v.shape[:-1] + (2, n, FF_CHUNK))
    v = jnp.moveaxis(v, -2, 0)
    return v.reshape(v.shape[:-2] + (2 * FF_CHUNK,))


def _ff_unchunked(v):
    n = v.shape[-3]
    v = v.reshape(v.shape[:-1] + (2, FF_CHUNK))
    v = jnp.moveaxis(v, -4, -2)
    return v.reshape(v.shape[:-3] + (2 * n * FF_CHUNK,))


def _rope_tables(pos, half):
    inv = ROPE_THETA ** (-jnp.arange(half, dtype=F32) / half)
    ang = pos.astype(F32)[:, None] * inv[None, :]
    cos, sin = jnp.cos(ang), jnp.sin(ang)
    return jnp.concatenate([cos, cos], axis=-1), jnp.concatenate([sin, sin], axis=-1)


def _rotate_half_cols(w):
    half = w.shape[-1] // 2
    return jnp.concatenate([-w[..., half:], w[..., :half]], axis=-1)


def _layer_weights(l, p, sizes, tdec, dbatch):
    pw, ql, kl, rp, gw, dm, nope, vh, d_ff = sizes
    w_in = p["w_in"][l]
    o = 0
    cols = {}
    for name, width in (("a", pw), ("cq", ql), ("ckv", kl), ("kr", rp), ("u", gw), ("v", gw), ("g", 3 * dm)):
        cols[name] = w_in[:, o:o + width]
        o += width
    kr_block = jnp.concatenate([cols["kr"], _rotate_half_cols(cols["kr"]),
                                jnp.zeros((dm, LANES - 2 * rp), F32)], axis=1)
    lw = {"dims": (pw, ql, kl, rp, gw), "nope": nope,
          "scale": float((nope + rp) ** -0.5), "alpha": float((2.0 * p["w_in"].shape[0]) ** 0.25)}
    lw["w_in_main"] = jnp.concatenate([cols["a"], cols["cq"], cols["ckv"], cols["u"], cols["v"], kr_block],
                                      axis=1).astype(BF)
    lw["w_gates"] = cols["g"].astype(BF)
    row = lambda v: v.reshape(1, -1)
    for k in ("q_norm_g", "kv_norm_g", "v_norm_g", "v_norm_b", "pool_scale", "ln1_g", "ln1_b", "ln2_g", "ln2_b"):
        lw[k] = row(p[k][l])
    w_uq = p["w_uq"][l].reshape(ql, N_HEADS, nope + rp)
    q_nope = w_uq[:, :, :nope].reshape(ql, N_HEADS * nope)
    q_pe = w_uq[:, :, nope:]
    lw["w_q"] = jnp.concatenate([q_nope, q_pe.reshape(ql, N_HEADS * rp),
                                 _rotate_half_cols(q_pe).reshape(ql, N_HEADS * rp)], axis=1).astype(BF)
    wuk = jnp.transpose(p["w_uk"][l], (1, 2, 0))
    lw["w_uk_pad"] = jnp.stack([
        jnp.pad(wuk[h], (((h % 2) * nope, LANES - nope - (h % 2) * nope), (0, 0))) for h in range(N_HEADS)
    ]).astype(BF)
    wuv = jnp.transpose(p["w_uv"][l], (1, 0, 2))
    lw["w_uv_pad"] = jnp.stack([
        jnp.pad(wuv[h], ((0, 0), ((h % 2) * vh, LANES - vh - (h % 2) * vh))) for h in range(N_HEADS)
    ]).astype(BF)
    pool_w = p["pool_w"][l]
    n_pg = pool_w.shape[0]
    lw["pool_bd"] = jax.scipy.linalg.block_diag(*[pool_w[g] for g in range(n_pg)]).astype(BF)
    lw["w_pool_out"] = p["w_pool_out"][l].astype(BF)
    lw["w_mla_out"] = p["w_mla_out"][l].astype(BF)
    lw["w_gmlp_out"] = p["w_gmlp_out"][l].astype(BF)
    lw["w_out"] = p["w_out"][l].astype(BF)
    w_s = p["w_spatial"][l]
    b_s = p["b_spatial"][l]
    n_gg, clen, _ = w_s.shape
    gc = gw // n_gg
    lw["ws"] = jnp.tril(w_s).astype(BF)
    lw["bs"] = jnp.repeat(b_s.T, gc, axis=1)
    ws_dec = jnp.tril(w_s[:, :tdec, :tdec])
    eye = jnp.eye(dbatch, dtype=F32)
    lw["ws_dec"] = jnp.stack([jnp.kron(eye, ws_dec[g]) for g in range(n_gg)]).astype(BF)
    lw["bs_dec"] = jnp.tile(jnp.repeat(b_s[:, :tdec].T, gc, axis=1), (dbatch, 1))
    lw["w_up_c"] = _ff_chunked(p["w_up"][l], d_ff).astype(BF)
    lw["conv_w_c"] = _ff_chunked(p["conv_w"][l], d_ff)
    lw["conv_b_c"] = _ff_chunked(p["conv_b"][l].reshape(1, -1), d_ff)
    lw["w_down_c"] = p["w_down"][l].reshape(d_ff // FF_CHUNK, FF_CHUNK, dm).astype(BF)
    return lw


def _inv_count(pos, pw):
    gwidth = pw // len(POOL_WINDOWS)
    win = jnp.repeat(jnp.asarray(POOL_WINDOWS, jnp.int32), gwidth)
    cnt = jnp.minimum(pos[:, None] + 1, win[None, :]).astype(F32)
    return 1.0 / cnt


def kernel(x_prompt, x_sample, cache_ckv, cache_kpe, state_pool, state_ffn, page_table, w_in, pool_w, pool_scale,
           w_pool_out, q_norm_g, w_uq, kv_norm_g, w_uk, w_uv, w_mla_out, v_norm_g, v_norm_b, w_spatial, b_spatial,
           w_gmlp_out, w_out, ln1_g, ln1_b, w_up, conv_w, conv_b, w_down, ln2_g, ln2_b):
    p = dict(w_in=w_in, pool_w=pool_w, pool_scale=pool_scale, w_pool_out=w_pool_out, q_norm_g=q_norm_g, w_uq=w_uq,
             kv_norm_g=kv_norm_g, w_uk=w_uk, w_uv=w_uv, w_mla_out=w_mla_out, v_norm_g=v_norm_g, v_norm_b=v_norm_b,
             w_spatial=w_spatial, b_spatial=b_spatial, w_gmlp_out=w_gmlp_out, w_out=w_out, ln1_g=ln1_g,
             ln1_b=ln1_b, w_up=w_up, conv_w=conv_w, conv_b=conv_b, w_down=w_down, ln2_g=ln2_g, ln2_b=ln2_b)
    batch, seq, dm = x_prompt.shape
    dbatch, tdec, _ = x_sample.shape
    depth = w_in.shape[0]
    ps = cache_ckv.shape[2]
    n_pages = page_table.shape[1]
    past = n_pages * ps
    kl, rp = cache_ckv.shape[3], cache_kpe.shape[3]
    pw = state_pool.shape[3]
    ql = q_norm_g.shape[1]
    gw = v_norm_g.shape[1]
    nope, vh = w_uk.shape[3], w_uv.shape[3]
    d_ff = w_down.shape[1]
    sizes = (pw, ql, kl, rp, gw, dm, nope, vh, d_ff)
    assert tdec == SUBLANES and pw == 2 * LANES and kl % LANES == 0 and N_HEADS * rp == 2 * LANES
    assert nope * 2 == LANES and vh * 2 == LANES and d_ff % FF_CHUNK == 0

    tm = min(512, seq)
    tq = min(256, seq)
    rows_s = dbatch * tdec
    n_grp = 8 if n_pages % 8 == 0 else 1
    assert seq % tm == 0 and seq % tq == 0

    pos_p = jnp.arange(seq, dtype=jnp.int32)
    pos_s = past + jnp.arange(tdec, dtype=jnp.int32)

    def tables(pos, reps):
        c, s = _rope_tables(pos, rp // 2)
        zeros = jnp.zeros((pos.shape[0], LANES - rp), F32)
        ck, sk = jnp.concatenate([c, zeros], 1), jnp.concatenate([s, zeros], 1)
        cq, sq = jnp.tile(c, (1, N_HEADS)), jnp.tile(s, (1, N_HEADS))
        ic = _inv_count(pos, pw)
        return [jnp.tile(t, (reps, 1)) for t in (ck, sk, cq, sq, ic)]

    cos_kp, sin_kp, cos_qp, sin_qp, icnt_p = tables(pos_p, 1)
    cos_ks, sin_ks, cos_qs, sin_qs, icnt_s = tables(pos_s, dbatch)

    xp = x_prompt.reshape(batch * seq, dm)
    xs = x_sample.reshape(rows_s, dm)
    outs = [[] for _ in range(9)]
    for l in range(depth):
        lw = _layer_weights(l, p, sizes, tdec, dbatch)
        a_in, cqn, ckvn, kpe, kv, u, vn = _in_proj(xp, lw, cos_kp, sin_kp, tm=tm, vn_dtype=BF)
        (q,) = _q_proj(cqn, lw, cos_qp, sin_qp, tm=tm, combined=True)
        yb_pre = _prompt_attention(q, kv, lw, batch=batch, seq=seq, tq=tq)
        xp = _merge_prompt(xp, a_in, icnt_p, u, vn, yb_pre, lw, tm=tm, seq=seq)
        xp, tail_p = _ffn(xp, None, lw, tm=tm, seq=seq, n_seq=batch)
        outs[0].append(ckvn.reshape(batch * seq // ps, ps, kl))
        outs[1].append(kpe.reshape(batch * seq // ps, ps, rp))
        outs[4].append(a_in.reshape(batch, seq, pw)[:, seq - POOL_KEEP:])
        outs[6].append(_ff_unchunked(tail_p))
        a_s, cqn_s, ckvn_s, kpe_s, _, u_s, vn_s = _in_proj(xs, lw, cos_ks, sin_ks, tm=rows_s, vn_dtype=F32)
        qlat_s, qpe_s = _q_proj(cqn_s, lw, cos_qs, sin_qs, tm=rows_s, combined=False)
        yb_s = _sample_attention(qlat_s, qpe_s, ckvn_s, kpe_s, cache_ckv, cache_kpe, page_table, l, lw,
                                 dbatch=dbatch, tdec=tdec, n_grp=n_grp)
        pool_ext = jnp.concatenate([jnp.zeros((dbatch, 1, pw), F32), state_pool[l],
                                    a_s.reshape(dbatch, tdec, pw)], axis=1)
        xs = _merge_sample(xs, pool_ext, icnt_s, u_s, vn_s.astype(BF), yb_s, lw, tdec=tdec)
        prefix = jnp.pad(state_ffn[l], ((0, 0), (SUBLANES - (CONV_W - 1), 0), (0, 0))).reshape(rows_s, 2 * d_ff)
        xs, tail_s = _ffn(xs, _ff_chunked(prefix, d_ff), lw, tm=rows_s, seq=tdec, n_seq=dbatch)
        outs[2].append(ckvn_s.reshape(dbatch, tdec, kl))
        outs[3].append(kpe_s.reshape(dbatch, tdec, rp))
        outs[5].append(pool_ext[:, -POOL_KEEP:])
        outs[7].append(_ff_unchunked(jnp.moveaxis(tail_s, 0, 1)))
        outs[8].append(vn_s.reshape(dbatch, tdec, gw))
    stacked = [jnp.stack(o) for o in outs]
    new_ckv_p, new_kpe_p, new_ckv_s, new_kpe_s, pool_p, pool_s, ffn_p, ffn_s, gv_s = stacked
    return (xp.reshape(batch, seq, dm), xs.reshape(dbatch, tdec, dm), new_ckv_p, new_kpe_p, new_ckv_s, new_kpe_s,
            pool_p, pool_s, ffn_p, ffn_s, gv_s)
```

```python
import functools

import jax
import jax.numpy as jnp
import numpy as np
from jax import lax
from jax.experimental import pallas as pl
from jax.experimental.pallas import tpu as pltpu

BF = jnp.bfloat16
F32 = jnp.float32

POOL_WINDOWS = (2, 4, 8, 16)
POOL_KEEP = max(POOL_WINDOWS) - 1
N_HEADS = 8
ROPE_THETA = 10000.0
RMS_EPS = 1e-6
LN_EPS = 1e-5
CONV_W = 3
SUBLANES = 8
LANES = 128
FF_CHUNK = 256
NEG_BIG = -1e30
SCORE_LOOKAHEAD = 4
VMEM_LIMIT = 56 * 1024 * 1024


def _params(*sem):
    return pltpu.CompilerParams(dimension_semantics=sem, vmem_limit_bytes=VMEM_LIMIT)


def _const_spec(shape):
    nd = len(shape)
    return pl.BlockSpec(shape, lambda *_: (0,) * nd)


def _rmsnorm(x, g):
    return x * lax.rsqrt(jnp.mean(x * x, axis=-1, keepdims=True) + RMS_EPS) * g


def _layernorm(x, g, b):
    mu = jnp.mean(x, axis=-1, keepdims=True)
    xc = x - mu
    var = jnp.mean(xc * xc, axis=-1, keepdims=True)
    return xc * lax.rsqrt(var + LN_EPS) * g + b


def _dot(a, b):
    return jnp.dot(a, b, preferred_element_type=F32)


def _dot_nt(a, b):
    return lax.dot_general(a, b, (((1,), (1,)), ((), ())), preferred_element_type=F32)


def _shift_rows(y3, prev3, k):
    t = lax.broadcasted_iota(jnp.int32, y3.shape, 1)
    return jnp.where(t >= k, pltpu.roll(y3, k, 1), pltpu.roll(prev3, k, 1))


def _prev_groups(y3, first):
    if y3.shape[0] == 1:
        return first
    return jnp.concatenate([first, y3[:-1]], axis=0)


def _in_proj_body(x_ref, w_ref, qg_ref, kvg_ref, vg_ref, vb_ref, cos_ref, sin_ref,
                  a_ref, cq_ref, ckv_ref, kpe_ref, kv_ref, vt_ref, u_ref, vn_ref, *, dims):
    pw, ql, kl, rp, gw = dims
    z = _dot(x_ref[...].astype(BF), w_ref[...])
    o = 0
    a_ref[...] = z[:, o:o + pw]
    o += pw
    cq_ref[...] = _rmsnorm(z[:, o:o + ql], qg_ref[...]).astype(BF)
    o += ql
    ckvn = _rmsnorm(z[:, o:o + kl], kvg_ref[...])
    ckv_ref[...] = ckvn
    o += kl
    u_ref[...] = z[:, o:o + gw].astype(BF)
    o += gw
    vn_ref[...] = _layernorm(z[:, o:o + gw], vg_ref[...], vb_ref[...]).astype(vn_ref.dtype)
    o += gw
    kr = z[:, o:o + LANES]
    kpe = kr * cos_ref[...] + pltpu.roll(kr, LANES - rp, 1) * sin_ref[...]
    kpe_ref[...] = kpe[:, :rp]
    tiled = kpe
    for j in range(1, LANES // rp):
        tiled = tiled + pltpu.roll(kpe, j * rp, 1)
    tb = tiled.astype(BF)
    kv_ref[:, 0:kl] = ckvn.astype(BF)
    vt_ref[...] = ckvn.T.astype(BF)
    for j in range(N_HEADS * rp // LANES):
        kv_ref[:, kl + j * LANES:kl + (j + 1) * LANES] = tb


def _in_proj(x2d, lw, cos_k, sin_k, *, tm, vn_dtype):
    rows, dm = x2d.shape
    dims = lw["dims"]
    pw, ql, kl, rp, gw = dims
    ntab = cos_k.shape[0] // tm
    ncol = lw["w_in_main"].shape[1]
    row_spec = lambda w: pl.BlockSpec((tm, w), lambda i: (i, 0))
    tab_spec = pl.BlockSpec((tm, LANES), lambda i: (i % ntab, 0))
    return pl.pallas_call(
        functools.partial(_in_proj_body, dims=dims),
        grid=(rows // tm,),
        in_specs=[row_spec(dm), _const_spec((dm, ncol)), _const_spec((1, ql)), _const_spec((1, kl)),
                  _const_spec((1, gw)), _const_spec((1, gw)), tab_spec, tab_spec],
        out_specs=[row_spec(pw), row_spec(ql), row_spec(kl), row_spec(rp), row_spec(kl + N_HEADS * rp),
                   pl.BlockSpec((kl, tm), lambda i: (0, i)), row_spec(gw), row_spec(gw)],
        out_shape=[jax.ShapeDtypeStruct((rows, pw), F32), jax.ShapeDtypeStruct((rows, ql), BF),
                   jax.ShapeDtypeStruct((rows, kl), F32), jax.ShapeDtypeStruct((rows, rp), F32),
                   jax.ShapeDtypeStruct((rows, kl + N_HEADS * rp), BF), jax.ShapeDtypeStruct((kl, rows), BF),
                   jax.ShapeDtypeStruct((rows, gw), BF), jax.ShapeDtypeStruct((rows, gw), vn_dtype)],
        compiler_params=_params("arbitrary"),
        name="in_proj",
    )(x2d, lw["w_in_main"], lw["q_norm_g"], lw["kv_norm_g"], lw["v_norm_g"], lw["v_norm_b"], cos_k, sin_k)


def _q_body(cq_ref, w_ref, wuk_ref, cos_ref, sin_ref, *out_refs, nope, rp, kl, scale, combined):
    hn = N_HEADS * nope
    hr = N_HEADS * rp
    q = _dot(cq_ref[...], w_ref[...])
    qpe = (q[:, hn:hn + hr] * cos_ref[...] + q[:, hn + hr:hn + 2 * hr] * sin_ref[...]) * scale
    if not combined:
        out_refs[1][...] = qpe
    else:
        head_of_lane = lax.broadcasted_iota(jnp.int32, qpe.shape, 1) // rp
    for h in range(N_HEADS):
        pair = q[:, (h // 2) * LANES:(h // 2 + 1) * LANES].astype(BF)
        qlat = _dot(pair, wuk_ref[h]) * scale
        if combined:
            out_refs[0][h, :, 0:kl] = qlat.astype(BF)
            out_refs[0][h, :, kl:kl + hr] = jnp.where(head_of_lane == h, qpe, 0.0).astype(BF)
        else:
            out_refs[0][h] = qlat


def _q_proj(cq, lw, cos_q, sin_q, *, tm, combined):
    rows, ql = cq.shape
    pw, _, kl, rp, gw = lw["dims"]
    nope = lw["nope"]
    hr = N_HEADS * rp
    ntab = cos_q.shape[0] // tm
    tab_spec = pl.BlockSpec((tm, hr), lambda i: (i % ntab, 0))
    if combined:
        out_specs = [pl.BlockSpec((N_HEADS, tm, kl + hr), lambda i: (0, i, 0))]
        out_shape = [jax.ShapeDtypeStruct((N_HEADS, rows, kl + hr), BF)]
    else:
        out_specs = [pl.BlockSpec((N_HEADS, tm, kl), lambda i: (0, i, 0)), pl.BlockSpec((tm, hr), lambda i: (i, 0))]
        out_shape = [jax.ShapeDtypeStruct((N_HEADS, rows, kl), F32), jax.ShapeDtypeStruct((rows, hr), F32)]
    return pl.pallas_call(
        functools.partial(_q_body, nope=nope, rp=rp, kl=kl, scale=lw["scale"], combined=combined),
        grid=(rows // tm,),
        in_specs=[pl.BlockSpec((tm, ql), lambda i: (i, 0)), _const_spec(lw["w_q"].shape),
                  _const_spec(lw["w_uk_pad"].shape), tab_spec, tab_spec],
        out_specs=out_specs, out_shape=out_shape,
        compiler_params=_params("arbitrary"),
        name="q_proj",
    )(cq, lw["w_q"], lw["w_uk_pad"], cos_q, sin_q)


def _heads_out(o, wuv_ref, rows_per_head, full_m):
    outs = []
    ob = o.astype(BF)
    for j in range(N_HEADS // 2):
        acc = None
        for h in (2 * j, 2 * j + 1):
            sl = slice(h * rows_per_head, (h + 1) * rows_per_head)
            if full_m:
                y = _dot(ob, wuv_ref[h])[sl]
            else:
                y = _dot(ob[sl], wuv_ref[h])
            acc = y if acc is None else acc + y
        outs.append(acc)
    return outs


def _attn_body(qi_ref, ki_ref, q_ref, kv_ref, vt_ref, wuvt_ref, o_ref, m_ref, l_ref, acc_ref, *, tq, kl):
    t = pl.program_id(1)
    qi = qi_ref[t]
    ki = ki_ref[t]

    @pl.when(ki == 0)
    def _():
        m_ref[...] = jnp.full(m_ref.shape, NEG_BIG, F32)
        l_ref[...] = jnp.zeros(l_ref.shape, F32)
        acc_ref[...] = jnp.zeros(acc_ref.shape, F32)

    def step(masked):
        kv = kv_ref[...]
        vt = vt_ref[...]
        if masked:
            causal = (lax.broadcasted_iota(jnp.int32, (tq, tq), 0) <= lax.broadcasted_iota(jnp.int32, (tq, tq), 1))
        def scores(h):
            s = _dot_nt(kv, q_ref[h])
            return jnp.where(causal, s, NEG_BIG) if masked else s

        pending = [scores(h) for h in range(SCORE_LOOKAHEAD)]
        for h in range(N_HEADS):
            s = pending.pop(0)
            if h + SCORE_LOOKAHEAD < N_HEADS:
                pending.append(scores(h + SCORE_LOOKAHEAD))
            m_prev = m_ref[h]
            m_new = jnp.maximum(m_prev, jnp.max(s, axis=0, keepdims=True))
            alpha = jnp.exp(m_prev - m_new)
            p = jnp.exp(s - m_new)
            l_ref[h] = alpha * l_ref[h] + jnp.sum(p, axis=0, keepdims=True)
            acc_ref[h] = alpha * acc_ref[h] + _dot(vt, p.astype(BF))
            m_ref[h] = m_new

    @pl.when(ki < qi)
    def _():
        step(False)

    @pl.when(ki == qi)
    def _():
        step(True)
        outs = []
        for h in range(N_HEADS):
            o = (acc_ref[h] * (1.0 / l_ref[h])).astype(BF)
            outs.append(_dot(wuvt_ref[h], o))
        o_ref[...] = jnp.concatenate(outs, axis=0).T.astype(o_ref.dtype)


def _prompt_attention(q, kv, vt, lw, *, batch, seq, tq):
    kl = lw["dims"][2]
    width = q.shape[-1]
    nq = seq // tq
    n_h, vh, _ = lw["w_uv_t"].shape
    pairs = [(i, k) for i in range(nq) for k in range(i + 1)]
    qi_tab = jnp.asarray([i for i, _ in pairs], jnp.int32)
    ki_tab = jnp.asarray([k for _, k in pairs], jnp.int32)
    grid_spec = pltpu.PrefetchScalarGridSpec(
        num_scalar_prefetch=2, grid=(batch, len(pairs)),
        in_specs=[pl.BlockSpec((N_HEADS, tq, width), lambda b, t, qt, kt: (0, b * nq + qt[t], 0)),
                  pl.BlockSpec((tq, width), lambda b, t, qt, kt: (b * nq + kt[t], 0)),
                  pl.BlockSpec((kl, tq), lambda b, t, qt, kt: (0, b * nq + kt[t])),
                  _const_spec(lw["w_uv_t"].shape)],
        out_specs=pl.BlockSpec((tq, n_h * vh), lambda b, t, qt, kt: (b * nq + qt[t], 0)),
        scratch_shapes=[pltpu.VMEM((N_HEADS, 1, tq), F32), pltpu.VMEM((N_HEADS, 1, tq), F32),
                        pltpu.VMEM((N_HEADS, kl, tq), F32)])
    return pl.pallas_call(
        functools.partial(_attn_body, tq=tq, kl=kl),
        grid_spec=grid_spec,
        out_shape=jax.ShapeDtypeStruct((batch * seq, n_h * vh), BF),
        compiler_params=_params("arbitrary", "arbitrary"),
        name="prompt_attention",
    )(qi_tab, ki_tab, q, kv, vt, lw["w_uv_t"])


def _decode_body(pt_ref, qlat_ref, qpe_ref, ckvn_ref, kpen_ref, wuv_ref, *rest, n_grp, n_split, tdec, rp, kl):
    ckv_refs = rest[:n_grp]
    kpe_refs = rest[n_grp:2 * n_grp]
    o_ref, q_s, qp_s, m_ref, l_ref, acc_ref = rest[2 * n_grp:]
    p_id = pl.program_id(1)
    rows = N_HEADS * tdec

    @pl.when(p_id == 0)
    def _():
        m_ref[...] = jnp.full(m_ref.shape, NEG_BIG, F32)
        l_ref[...] = jnp.zeros(l_ref.shape, F32)
        acc_ref[...] = jnp.zeros(acc_ref.shape, F32)
        q_s[...] = qlat_ref[...].reshape(rows, kl).astype(BF)
        qpe = qpe_ref[...]
        qp_s[...] = jnp.concatenate([qpe[:, h * rp:(h + 1) * rp] for h in range(N_HEADS)], axis=0).astype(BF)

    def update(g, s, keys):
        m_prev = m_ref[g]
        m_new = jnp.maximum(m_prev, jnp.max(s, axis=-1, keepdims=True))
        alpha = jnp.exp(m_prev - m_new)
        p = jnp.exp(s - m_new)
        l_ref[g] = alpha * l_ref[g] + jnp.sum(p, axis=-1, keepdims=True)
        acc = alpha * acc_ref[g]
        off = 0
        pb = p.astype(BF)
        for k in keys:
            acc = acc + _dot(pb[:, off:off + k.shape[0]], k)
            off += k.shape[0]
        acc_ref[g] = acc
        m_ref[g] = m_new

    q = q_s[...]
    qp = qp_s[...]
    per = n_grp // n_split
    keys = [r[...].astype(BF) for r in ckv_refs]
    scores = [_dot_nt(q, k) + _dot(qp, r[...].astype(BF)) for k, r in zip(keys, kpe_refs)]
    for g in range(n_split):
        update(g, jnp.concatenate(scores[g * per:(g + 1) * per], axis=1), keys[g * per:(g + 1) * per])

    @pl.when(p_id == pl.num_programs(1) - 1)
    def _():
        pad = 2 * tdec
        kn = jnp.concatenate([ckvn_ref[...], jnp.zeros((pad - tdec, kl), F32)], axis=0).astype(BF)
        kpn = jnp.concatenate([kpen_ref[...], jnp.zeros((pad - tdec, rp), F32)], axis=0).astype(BF)
        sn = _dot_nt(q, kn) + _dot_nt(qp, kpn)
        qpos = lax.broadcasted_iota(jnp.int32, (N_HEADS, tdec, pad), 1).reshape(rows, pad)
        kpos = lax.broadcasted_iota(jnp.int32, (rows, pad), 1)
        sn = jnp.where(kpos <= qpos, sn, NEG_BIG)
        update(0, sn, [kn])
        m_all = m_ref[0]
        for g in range(1, n_split):
            m_all = jnp.maximum(m_all, m_ref[g])
        l_all = jnp.zeros_like(m_all)
        o = jnp.zeros((rows, kl), F32)
        for g in range(n_split):
            w = jnp.exp(m_ref[g] - m_all)
            l_all = l_all + w * l_ref[g]
            o = o + w * acc_ref[g]
        o = o * (1.0 / l_all)
        for j, y in enumerate(_heads_out(o, wuv_ref, tdec, full_m=True)):
            o_ref[:, j * LANES:(j + 1) * LANES] = y.astype(o_ref.dtype)


def _sample_attention(qlat, qpe, ckvn, kpen, cache_ckv, cache_kpe_t, page_table, layer, lw, *, dbatch, tdec, n_grp,
                      n_split):
    _, _, kl, rp, _ = lw["dims"]
    n_pages = page_table.shape[1]
    ps = cache_ckv.shape[2]
    hv = lw["w_uv_pad"].shape[2] * N_HEADS // 2
    hr = N_HEADS * rp
    rows = N_HEADS * tdec

    def page_spec(shape, j):
        return pl.BlockSpec((None, None) + shape, lambda b, p, pt: (layer, pt[b, p * n_grp + j], 0, 0))

    in_specs = [pl.BlockSpec((N_HEADS, tdec, kl), lambda b, p, pt: (0, b, 0)),
                pl.BlockSpec((tdec, hr), lambda b, p, pt: (b, 0)),
                pl.BlockSpec((tdec, kl), lambda b, p, pt: (b, 0)),
                pl.BlockSpec((tdec, rp), lambda b, p, pt: (b, 0)),
                _const_spec(lw["w_uv_pad"].shape)]
    in_specs += [page_spec((ps, kl), j) for j in range(n_grp)]
    in_specs += [page_spec((rp, ps), j) for j in range(n_grp)]
    grid_spec = pltpu.PrefetchScalarGridSpec(
        num_scalar_prefetch=1, grid=(dbatch, n_pages // n_grp), in_specs=in_specs,
        out_specs=pl.BlockSpec((tdec, hv), lambda b, p, pt: (b, 0)),
        scratch_shapes=[pltpu.VMEM((rows, kl), BF), pltpu.VMEM((rows, rp), BF),
                        pltpu.VMEM((n_split, rows, 1), F32), pltpu.VMEM((n_split, rows, 1), F32),
                        pltpu.VMEM((n_split, rows, kl), F32)])
    return pl.pallas_call(
        functools.partial(_decode_body, n_grp=n_grp, n_split=n_split, tdec=tdec, rp=rp, kl=kl),
        grid_spec=grid_spec,
        out_shape=jax.ShapeDtypeStruct((dbatch * tdec, hv), F32),
        compiler_params=_params("arbitrary", "arbitrary"),
        name="sample_attention",
    )(page_table, qlat, qpe, ckvn, kpen, lw["w_uv_pad"], *([cache_ckv] * n_grp), *([cache_kpe_t] * n_grp))


def _window_select(sums, shape):
    gw = shape[-1] // len(POOL_WINDOWS)
    grp = lax.broadcasted_iota(jnp.int32, shape, len(shape) - 1) // gw
    out = sums[-1]
    for gi in range(len(POOL_WINDOWS) - 2, -1, -1):
        out = jnp.where(grp == gi, sums[gi], out)
    return out


def _merge_tail(x, d, u, vn, yb_pre, wg_ref, pbd_ref, psc_ref, wpo_ref, ws_ref, bs_ref, wgo_ref, wmo_ref,
                wo_ref, lng_ref, lnb_ref, alpha):
    dm = x.shape[1]
    xb = x.astype(BF)
    ya = _dot(d.astype(BF), pbd_ref[...]) * psc_ref[...]
    ya = _dot(ya.astype(BF), wpo_ref[...])
    m = jax.nn.sigmoid(_dot(xb, wg_ref[:, 0:dm])) * ya
    yb = _dot(yb_pre.astype(BF), wmo_ref[...])
    m = m + jax.nn.sigmoid(_dot(xb, wg_ref[:, dm:2 * dm])) * yb
    n_g, clen, _ = ws_ref.shape
    gc = vn.shape[1] // n_g
    grp = lax.broadcasted_iota(jnp.int32, (clen, vn.shape[1]), 1) // gc
    parts = []
    for c in range(vn.shape[0] // clen):
        vc = vn[c * clen:(c + 1) * clen]
        s = _dot(ws_ref[n_g - 1], vc)
        for g in range(n_g - 2, -1, -1):
            s = jnp.where(grp == g, _dot(ws_ref[g], vc), s)
        parts.append(s + bs_ref[...])
    s = parts[0] if len(parts) == 1 else jnp.concatenate(parts, axis=0)
    yc = _dot((u.astype(F32) * s).astype(BF), wgo_ref[...])
    m = m + jax.nn.sigmoid(_dot(xb, wg_ref[:, 2 * dm:3 * dm])) * yc
    y = alpha * x + _dot(m.astype(BF), wo_ref[...])
    return _layernorm(y, lng_ref[...], lnb_ref[...])


def _merge_prompt_body(x_ref, a_ref, aprev_ref, icnt_ref, u_ref, vn_ref, yb_ref, *rest, tiles_per_seq, alpha):
    w_refs, o_ref = rest[:-1], rest[-1]
    i = pl.program_id(0)
    a = a_ref[...]
    tm, pw = a.shape
    hist = jnp.where(i % tiles_per_seq == 0, 0.0, aprev_ref[...])
    n_hist = hist.shape[0] // SUBLANES
    ext = jnp.concatenate([hist, a], axis=0).reshape(tm // SUBLANES + n_hist, SUBLANES, pw)
    zero = jnp.zeros((1, SUBLANES, pw), F32)
    s2 = ext + _shift_rows(ext, _prev_groups(ext, zero), 1)
    s4 = s2 + _shift_rows(s2, _prev_groups(s2, zero), 2)
    s8 = s4 + _shift_rows(s4, _prev_groups(s4, zero), 4)
    s16 = s8 + _prev_groups(s8, zero)
    sel = _window_select([s[n_hist:] for s in (s2, s4, s8, s16)], (tm // SUBLANES, SUBLANES, pw))
    d = sel.reshape(tm, pw) * icnt_ref[...] - a
    o_ref[...] = _merge_tail(x_ref[...], d, u_ref[...], vn_ref[...], yb_ref[...], *w_refs, alpha)


def _merge_sample_body(x_ref, ext_ref, icnt_ref, u_ref, vn_ref, yb_ref, *rest, tdec, alpha):
    w_refs, o_ref = rest[:-1], rest[-1]
    nb, ext_len, pw = ext_ref.shape
    acc = None
    sums = []
    for j in range(max(POOL_WINDOWS)):
        cur = ext_ref[:, ext_len - tdec - j:ext_len - j, :]
        acc = cur if acc is None else acc + cur
        if j + 1 in POOL_WINDOWS:
            sums.append(acc)
    tok = ext_ref[:, ext_len - tdec:ext_len, :]
    sel = _window_select(sums, (nb, tdec, pw))
    d = sel.reshape(nb * tdec, pw) * icnt_ref[...] - tok.reshape(nb * tdec, pw)
    o_ref[...] = _merge_tail(x_ref[...], d, u_ref[...], vn_ref[...], yb_ref[...], *w_refs, alpha)


def _merge_weights(lw):
    return [lw["w_gates"], lw["pool_bd"], lw["pool_scale"], lw["w_pool_out"], lw["ws"], lw["bs"],
            lw["w_gmlp_out"], lw["w_mla_out"], lw["w_out"], lw["ln1_g"], lw["ln1_b"]]


def _merge_prompt(x2d, a_in, icnt, u, vn, yb_pre, lw, *, tm, seq):
    rows, dm = x2d.shape
    pw, _, _, _, gw = lw["dims"]
    tiles_per_seq = seq // tm
    hist_rows = 2 * SUBLANES
    hist_per_tile = tm // hist_rows
    weights = _merge_weights(lw)
    row_spec = lambda w: pl.BlockSpec((tm, w), lambda i: (i, 0))
    in_specs = [row_spec(dm), row_spec(pw),
                pl.BlockSpec((hist_rows, pw), lambda i: (jnp.maximum(i * hist_per_tile - 1, 0), 0)),
                pl.BlockSpec((tm, pw), lambda i: (i % tiles_per_seq, 0)),
                row_spec(gw), row_spec(gw), row_spec(yb_pre.shape[1])]
    in_specs += [_const_spec(w.shape) for w in weights]
    return pl.pallas_call(
        functools.partial(_merge_prompt_body, tiles_per_seq=tiles_per_seq, alpha=lw["alpha"]),
        grid=(rows // tm,), in_specs=in_specs, out_specs=row_spec(dm),
        out_shape=jax.ShapeDtypeStruct((rows, dm), F32),
        compiler_params=_params("arbitrary"),
        name="merge_prompt",
    )(x2d, a_in, a_in, icnt, u, vn, yb_pre, *weights)


def _merge_sample(x2d, ext, icnt, u, vn, yb_pre, lw, *, tdec):
    rows, dm = x2d.shape
    weights = [lw["w_gates"], lw["pool_bd"], lw["pool_scale"], lw["w_pool_out"], lw["ws_dec"], lw["bs_dec"],
               lw["w_gmlp_out"], lw["w_mla_out"], lw["w_out"], lw["ln1_g"], lw["ln1_b"]]
    args = [x2d, ext, icnt, u, vn, yb_pre] + weights
    return pl.pallas_call(
        functools.partial(_merge_sample_body, tdec=tdec, alpha=lw["alpha"]),
        grid=(1,), in_specs=[_const_spec(a.shape) for a in args], out_specs=_const_spec((rows, dm)),
        out_shape=jax.ShapeDtypeStruct((rows, dm), F32),
        compiler_params=_params("arbitrary"),
        name="merge_sample",
    )(*args)


def _ffn_body(x_ref, *rest, tiles_per_seq, alpha, has_prefix):
    if has_prefix:
        prefix_ref, rest = rest[0], rest[1:]
    wup_ref, cw_ref, cb_ref, wdn_ref, lng_ref, lnb_ref, o_ref, tail_ref, acc_ref, carry_ref = rest
    n_chunks, _, cw2 = wup_ref.shape
    half = cw2 // 2
    tm = x_ref.shape[0]
    grp = tm // SUBLANES
    x = x_ref[...]
    xb = x.astype(BF)

    if not has_prefix:
        @pl.when(pl.program_id(0) % tiles_per_seq == 0)
        def _():
            carry_ref[...] = jnp.zeros(carry_ref.shape, F32)

    def up(c):
        return _dot(xb, wup_ref[c]).reshape(grp, SUBLANES, cw2)

    a_next = up(0)
    for c in range(n_chunks):
        a3 = a_next
        if c + 1 < n_chunks:
            a_next = up(c + 1)
        if has_prefix:
            prev = prefix_ref[c].reshape(grp, SUBLANES, cw2)
        else:
            prev = _prev_groups(a3, carry_ref[c][None])
            carry_ref[c] = a3[grp - 1]
        w = cw_ref[c]
        conv = (cb_ref[c] + w[0:1] * _shift_rows(a3, prev, 2) + w[1:2] * _shift_rows(a3, prev, 1)
                + w[2:3] * a3).reshape(tm, cw2)
        h = jax.nn.silu(conv[:, :half]) * conv[:, half:]
        down = _dot(h.astype(BF), wdn_ref[c])
        if c == 0:
            acc_ref[...] = down
        else:
            acc_ref[...] += down
        if has_prefix:
            tail_ref[c] = a3[:, SUBLANES - (CONV_W - 1):, :]
        else:
            tail_ref[0, c] = a3[grp - 1, SUBLANES - (CONV_W - 1):, :]
    o_ref[...] = _layernorm(alpha * x + acc_ref[...], lng_ref[...], lnb_ref[...])


def _ffn(x2d, prefix, lw, *, tm, seq, n_seq):
    rows, dm = x2d.shape
    wup, cw, cb, wdn = lw["w_up_c"], lw["conv_w_c"], lw["conv_b_c"], lw["w_down_c"]
    n_chunks, _, cw2 = wup.shape
    has_prefix = prefix is not None
    tiles_per_seq = max(seq // tm, 1)
    keep = CONV_W - 1
    in_specs = [pl.BlockSpec((tm, dm), lambda i: (i, 0))]
    args = [x2d]
    if has_prefix:
        in_specs.append(_const_spec(prefix.shape))
        args.append(prefix)
        tail_shape = (n_chunks, rows // SUBLANES, keep, cw2)
        tail_spec = _const_spec(tail_shape)
    else:
        tail_shape = (n_seq, n_chunks, keep, cw2)
        tail_spec = pl.BlockSpec((1, n_chunks, keep, cw2), lambda i: (i // tiles_per_seq, 0, 0, 0))
    weights = [wup, cw, cb, wdn, lw["ln2_g"], lw["ln2_b"]]
    in_specs += [_const_spec(w.shape) for w in weights]
    return pl.pallas_call(
        functools.partial(_ffn_body, tiles_per_seq=tiles_per_seq, alpha=lw["alpha"], has_prefix=has_prefix),
        grid=(rows // tm,), in_specs=in_specs,
        out_specs=[pl.BlockSpec((tm, dm), lambda i: (i, 0)), tail_spec],
        out_shape=[jax.ShapeDtypeStruct((rows, dm), F32), jax.ShapeDtypeStruct(tail_shape, F32)],
        scratch_shapes=[pltpu.VMEM((tm, dm), F32), pltpu.VMEM((n_chunks, SUBLANES, cw2), F32)],
        compiler_params=_params("arbitrary"),
        name="ffn_sample" if has_prefix else "ffn_prompt",
    )(*args, *weights)


def _ff_chunked(v, d_ff):
    n = d_ff // FF_CHUNK
    v = v.reshape(v.shape[:-1] + (2, n, FF_CHUNK))
    v = jnp.moveaxis(v, -2, 0)
    return v.reshape(v.shape[:-2] + (2 * FF_CHUNK,))


def _ff_unchunked(v):
    n = v.shape[-3]
    v = v.reshape(v.shape[:-1] + (2, FF_CHUNK))
    v = jnp.moveaxis(v, -4, -2)
    return v.reshape(v.shape[:-3] + (2 * n * FF_CHUNK,))


def _rope_tables(pos, half):
    inv = ROPE_THETA ** (-jnp.arange(half, dtype=F32) / half)
    ang = pos.astype(F32)[:, None] * inv[None, :]
    cos, sin = jnp.cos(ang), jnp.sin(ang)
    return jnp.concatenate([cos, cos], axis=-1), jnp.concatenate([sin, sin], axis=-1)


def _rotate_half_cols(w):
    half = w.shape[-1] // 2
    return jnp.concatenate([-w[..., half:], w[..., :half]], axis=-1)


def _layer_weights(l, p, sizes, tdec, dbatch):
    pw, ql, kl, rp, gw, dm, nope, vh, d_ff = sizes
    w_in = p["w_in"][l]
    o = 0
    cols = {}
    for name, width in (("a", pw), ("cq", ql), ("ckv", kl), ("kr", rp), ("u", gw), ("v", gw), ("g", 3 * dm)):
        cols[name] = w_in[:, o:o + width]
        o += width
    kr_block = jnp.concatenate([cols["kr"], _rotate_half_cols(cols["kr"]),
                                jnp.zeros((dm, LANES - 2 * rp), F32)], axis=1)
    lw = {"dims": (pw, ql, kl, rp, gw), "nope": nope,
          "scale": float((nope + rp) ** -0.5), "alpha": float((2.0 * p["w_in"].shape[0]) ** 0.25)}
    lw["w_in_main"] = jnp.concatenate([cols["a"], cols["cq"], cols["ckv"], cols["u"], cols["v"], kr_block],
                                      axis=1).astype(BF)
    lw["w_gates"] = cols["g"].astype(BF)
    row = lambda v: v.reshape(1, -1)
    for k in ("q_norm_g", "kv_norm_g", "v_norm_g", "v_norm_b", "pool_scale", "ln1_g", "ln1_b", "ln2_g", "ln2_b"):
        lw[k] = row(p[k][l])
    w_uq = p["w_uq"][l].reshape(ql, N_HEADS, nope + rp)
    q_nope = w_uq[:, :, :nope].reshape(ql, N_HEADS * nope)
    q_pe = w_uq[:, :, nope:]
    lw["w_q"] = jnp.concatenate([q_nope, q_pe.reshape(ql, N_HEADS * rp),
                                 _rotate_half_cols(q_pe).reshape(ql, N_HEADS * rp)], axis=1).astype(BF)
    wuk = jnp.transpose(p["w_uk"][l], (1, 2, 0))
    lw["w_uk_pad"] = jnp.stack([
        jnp.pad(wuk[h], (((h % 2) * nope, LANES - nope - (h % 2) * nope), (0, 0))) for h in range(N_HEADS)
    ]).astype(BF)
    wuv = jnp.transpose(p["w_uv"][l], (1, 0, 2))
    lw["w_uv_pad"] = jnp.stack([
        jnp.pad(wuv[h], ((0, 0), ((h % 2) * vh, LANES - vh - (h % 2) * vh))) for h in range(N_HEADS)
    ]).astype(BF)
    lw["w_uv_t"] = jnp.transpose(p["w_uv"][l], (1, 2, 0)).astype(BF)
    pool_w = p["pool_w"][l]
    n_pg = pool_w.shape[0]
    lw["pool_bd"] = jax.scipy.linalg.block_diag(*[pool_w[g] for g in range(n_pg)]).astype(BF)
    lw["w_pool_out"] = p["w_pool_out"][l].astype(BF)
    lw["w_mla_out"] = p["w_mla_out"][l].astype(BF)
    lw["w_gmlp_out"] = p["w_gmlp_out"][l].astype(BF)
    lw["w_out"] = p["w_out"][l].astype(BF)
    w_s = p["w_spatial"][l]
    b_s = p["b_spatial"][l]
    n_gg, clen, _ = w_s.shape
    gc = gw // n_gg
    lw["ws"] = jnp.tril(w_s).astype(BF)
    lw["bs"] = jnp.repeat(b_s.T, gc, axis=1)
    ws_dec = jnp.tril(w_s[:, :tdec, :tdec])
    eye = jnp.eye(dbatch, dtype=F32)
    lw["ws_dec"] = jnp.stack([jnp.kron(eye, ws_dec[g]) for g in range(n_gg)]).astype(BF)
    lw["bs_dec"] = jnp.tile(jnp.repeat(b_s[:, :tdec].T, gc, axis=1), (dbatch, 1))
    lw["w_up_c"] = _ff_chunked(p["w_up"][l], d_ff).astype(BF)
    lw["conv_w_c"] = _ff_chunked(p["conv_w"][l], d_ff)
    lw["conv_b_c"] = _ff_chunked(p["conv_b"][l].reshape(1, -1), d_ff)
    lw["w_down_c"] = p["w_down"][l].reshape(d_ff // FF_CHUNK, FF_CHUNK, dm).astype(BF)
    return lw


def _inv_count(pos, pw):
    gwidth = pw // len(POOL_WINDOWS)
    win = jnp.repeat(jnp.asarray(POOL_WINDOWS, jnp.int32), gwidth)
    cnt = jnp.minimum(pos[:, None] + 1, win[None, :]).astype(F32)
    return 1.0 / cnt


def kernel(x_prompt, x_sample, cache_ckv, cache_kpe, state_pool, state_ffn, page_table, w_in, pool_w, pool_scale,
           w_pool_out, q_norm_g, w_uq, kv_norm_g, w_uk, w_uv, w_mla_out, v_norm_g, v_norm_b, w_spatial, b_spatial,
           w_gmlp_out, w_out, ln1_g, ln1_b, w_up, conv_w, conv_b, w_down, ln2_g, ln2_b):
    p = dict(w_in=w_in, pool_w=pool_w, pool_scale=pool_scale, w_pool_out=w_pool_out, q_norm_g=q_norm_g, w_uq=w_uq,
             kv_norm_g=kv_norm_g, w_uk=w_uk, w_uv=w_uv, w_mla_out=w_mla_out, v_norm_g=v_norm_g, v_norm_b=v_norm_b,
             w_spatial=w_spatial, b_spatial=b_spatial, w_gmlp_out=w_gmlp_out, w_out=w_out, ln1_g=ln1_g,
             ln1_b=ln1_b, w_up=w_up, conv_w=conv_w, conv_b=conv_b, w_down=w_down, ln2_g=ln2_g, ln2_b=ln2_b)
    batch, seq, dm = x_prompt.shape
    dbatch, tdec, _ = x_sample.shape
    depth = w_in.shape[0]
    ps = cache_ckv.shape[2]
    n_pages = page_table.shape[1]
    past = n_pages * ps
    kl, rp = cache_ckv.shape[3], cache_kpe.shape[3]
    pw = state_pool.shape[3]
    ql = q_norm_g.shape[1]
    gw = v_norm_g.shape[1]
    nope, vh = w_uk.shape[3], w_uv.shape[3]
    d_ff = w_down.shape[1]
    sizes = (pw, ql, kl, rp, gw, dm, nope, vh, d_ff)
    assert tdec == SUBLANES and pw == 2 * LANES and kl % LANES == 0 and N_HEADS * rp == 2 * LANES
    assert nope * 2 == LANES and vh * 2 == LANES and d_ff % FF_CHUNK == 0

    tm = min(512, seq)
    tq = min(512, seq)
    rows_s = dbatch * tdec
    n_grp = next(g for g in (16, 8, 4, 2, 1) if n_pages % g == 0)
    n_split = min(2, n_grp)
    assert seq % tm == 0 and seq % tq == 0
    cache_kpe_t = jnp.swapaxes(cache_kpe, 2, 3)

    pos_p = jnp.arange(seq, dtype=jnp.int32)
    pos_s = past + jnp.arange(tdec, dtype=jnp.int32)

    def tables(pos, reps):
        c, s = _rope_tables(pos, rp // 2)
        zeros = jnp.zeros((pos.shape[0], LANES - rp), F32)
        ck, sk = jnp.concatenate([c, zeros], 1), jnp.concatenate([s, zeros], 1)
        cq, sq = jnp.tile(c, (1, N_HEADS)), jnp.tile(s, (1, N_HEADS))
        ic = _inv_count(pos, pw)
        return [jnp.tile(t, (reps, 1)) for t in (ck, sk, cq, sq, ic)]

    cos_kp, sin_kp, cos_qp, sin_qp, icnt_p = tables(pos_p, 1)
    cos_ks, sin_ks, cos_qs, sin_qs, icnt_s = tables(pos_s, dbatch)

    xp = x_prompt.reshape(batch * seq, dm)
    xs = x_sample.reshape(rows_s, dm)
    outs = [[] for _ in range(9)]
    for l in range(depth):
        lw = _layer_weights(l, p, sizes, tdec, dbatch)
        a_in, cqn, ckvn, kpe, kv, vt, u, vn = _in_proj(xp, lw, cos_kp, sin_kp, tm=tm, vn_dtype=BF)
        (q,) = _q_proj(cqn, lw, cos_qp, sin_qp, tm=tm, combined=True)
        yb_pre = _prompt_attention(q, kv, vt, lw, batch=batch, seq=seq, tq=tq)
        xp = _merge_prompt(xp, a_in, icnt_p, u, vn, yb_pre, lw, tm=tm, seq=seq)
        xp, tail_p = _ffn(xp, None, lw, tm=tm, seq=seq, n_seq=batch)
        outs[0].append(ckvn.reshape(batch * seq // ps, ps, kl))
        outs[1].append(kpe.reshape(batch * seq // ps, ps, rp))
        outs[4].append(a_in.reshape(batch, seq, pw)[:, seq - POOL_KEEP:])
        outs[6].append(_ff_unchunked(tail_p))
        a_s, cqn_s, ckvn_s, kpe_s, _, _, u_s, vn_s = _in_proj(xs, lw, cos_ks, sin_ks, tm=rows_s, vn_dtype=F32)
        qlat_s, qpe_s = _q_proj(cqn_s, lw, cos_qs, sin_qs, tm=rows_s, combined=False)
        yb_s = _sample_attention(qlat_s, qpe_s, ckvn_s, kpe_s, cache_ckv, cache_kpe_t, page_table, l, lw,
                                 dbatch=dbatch, tdec=tdec, n_grp=n_grp, n_split=n_split)
        pool_ext = jnp.concatenate([jnp.zeros((dbatch, 1, pw), F32), state_pool[l],
                                    a_s.reshape(dbatch, tdec, pw)], axis=1)
        xs = _merge_sample(xs, pool_ext, icnt_s, u_s, vn_s.astype(BF), yb_s, lw, tdec=tdec)
        prefix = jnp.pad(state_ffn[l], ((0, 0), (SUBLANES - (CONV_W - 1), 0), (0, 0))).reshape(rows_s, 2 * d_ff)
        xs, tail_s = _ffn(xs, _ff_chunked(prefix, d_ff), lw, tm=rows_s, seq=tdec, n_seq=dbatch)
        outs[2].append(ckvn_s.reshape(dbatch, tdec, kl))
        outs[3].append(kpe_s.reshape(dbatch, tdec, rp))
        outs[5].append(pool_ext[:, -POOL_KEEP:])
        outs[7].append(_ff_unchunked(jnp.moveaxis(tail_s, 0, 1)))
        outs[8].append(vn_s.reshape(dbatch, tdec, gw))
    stacked = [jnp.stack(o) for o in outs]
    new_ckv_p, new_kpe_p, new_ckv_s, new_kpe_s, pool_p, pool_s, ffn_p, ffn_s, gv_s = stacked
    return (xp.reshape(batch, seq, dm), xs.reshape(dbatch, tdec, dm), new_ckv_p, new_kpe_p, new_ckv_s, new_kpe_s,
            pool_p, pool_s, ffn_p, ffn_s, gv_s)
```

```python
import functools

import jax
import jax.numpy as jnp
import numpy as np
from jax import lax
from jax.experimental import pallas as pl
from jax.experimental.pallas import tpu as pltpu

BF = jnp.bfloat16
F32 = jnp.float32

POOL_WINDOWS = (2, 4, 8, 16)
POOL_KEEP = max(POOL_WINDOWS) - 1
N_HEADS = 8
ROPE_THETA = 10000.0
RMS_EPS = 1e-6
LN_EPS = 1e-5
CONV_W = 3
SUBLANES = 8
LANES = 128
FF_CHUNK = 256
NEG_BIG = -1e30
SCORE_LOOKAHEAD = 4
VMEM_LIMIT = 56 * 1024 * 1024


def _params(*sem):
    return pltpu.CompilerParams(dimension_semantics=sem, vmem_limit_bytes=VMEM_LIMIT)


def _const_spec(shape):
    nd = len(shape)
    return pl.BlockSpec(shape, lambda *_: (0,) * nd, pipeline_mode=pl.Buffered(1))


def _rmsnorm(x, g):
    return x * lax.rsqrt(jnp.mean(x * x, axis=-1, keepdims=True) + RMS_EPS) * g


def _layernorm(x, g, b):
    mu = jnp.mean(x, axis=-1, keepdims=True)
    xc = x - mu
    var = jnp.mean(xc * xc, axis=-1, keepdims=True)
    return xc * lax.rsqrt(var + LN_EPS) * g + b


def _dot(a, b):
    return jnp.dot(a, b, preferred_element_type=F32)


def _dot_nt(a, b):
    return lax.dot_general(a, b, (((1,), (1,)), ((), ())), preferred_element_type=F32)


def _shift_rows(y3, prev3, k):
    t = lax.broadcasted_iota(jnp.int32, y3.shape, 1)
    return jnp.where(t >= k, pltpu.roll(y3, k, 1), pltpu.roll(prev3, k, 1))


def _prev_groups(y3, first):
    if y3.shape[0] == 1:
        return first
    return jnp.concatenate([first, y3[:-1]], axis=0)


def _in_proj_body(x_ref, w_ref, qg_ref, kvg_ref, vg_ref, vb_ref, cos_ref, sin_ref,
                  a_ref, cq_ref, ckv_ref, kpe_ref, kv_ref, vt_ref, u_ref, vn_ref, *, dims):
    pw, ql, kl, rp, gw = dims
    tm = x_ref.shape[0]
    n_part = 2 if tm % (2 * LANES) == 0 else 1
    part = tm // n_part
    zs = [_dot(x_ref[i * part:(i + 1) * part, :].astype(BF), w_ref[...]) for i in range(n_part)]
    for i, z in enumerate(zs):
        rows = slice(i * part, (i + 1) * part)
        o = 0
        a_ref[rows, :] = z[:, o:o + pw]
        o += pw
        cq_ref[rows, :] = _rmsnorm(z[:, o:o + ql], qg_ref[...]).astype(BF)
        o += ql
        ckvn = _rmsnorm(z[:, o:o + kl], kvg_ref[...])
        ckv_ref[rows, :] = ckvn
        o += kl
        u_ref[rows, :] = z[:, o:o + gw].astype(BF)
        o += gw
        vn_ref[rows, :] = _layernorm(z[:, o:o + gw], vg_ref[...], vb_ref[...]).astype(vn_ref.dtype)
        o += gw
        kr = z[:, o:o + LANES]
        kpe = kr * cos_ref[rows, :] + pltpu.roll(kr, LANES - rp, 1) * sin_ref[rows, :]
        kpe_ref[rows, :] = kpe[:, :rp]
        tiled = kpe
        for j in range(1, LANES // rp):
            tiled = tiled + pltpu.roll(kpe, j * rp, 1)
        tb = tiled.astype(BF)
        kv_ref[rows, 0:kl] = ckvn.astype(BF)
        vt_ref[:, rows] = ckvn.T.astype(BF)
        for j in range(N_HEADS * rp // LANES):
            kv_ref[rows, kl + j * LANES:kl + (j + 1) * LANES] = tb


def _in_proj(x2d, lw, cos_k, sin_k, *, tm, vn_dtype):
    rows, dm = x2d.shape
    dims = lw["dims"]
    pw, ql, kl, rp, gw = dims
    ntab = cos_k.shape[0] // tm
    ncol = lw["w_in_main"].shape[1]
    row_spec = lambda w: pl.BlockSpec((tm, w), lambda i: (i, 0))
    tab_spec = pl.BlockSpec((tm, LANES), lambda i: (i % ntab, 0))
    return pl.pallas_call(
        functools.partial(_in_proj_body, dims=dims),
        grid=(rows // tm,),
        in_specs=[row_spec(dm), _const_spec((dm, ncol)), _const_spec((1, ql)), _const_spec((1, kl)),
                  _const_spec((1, gw)), _const_spec((1, gw)), tab_spec, tab_spec],
        out_specs=[row_spec(pw), row_spec(ql), row_spec(kl), row_spec(rp), row_spec(kl + N_HEADS * rp),
                   pl.BlockSpec((kl, tm), lambda i: (0, i)), row_spec(gw), row_spec(gw)],
        out_shape=[jax.ShapeDtypeStruct((rows, pw), F32), jax.ShapeDtypeStruct((rows, ql), BF),
                   jax.ShapeDtypeStruct((rows, kl), F32), jax.ShapeDtypeStruct((rows, rp), F32),
                   jax.ShapeDtypeStruct((rows, kl + N_HEADS * rp), BF), jax.ShapeDtypeStruct((kl, rows), BF),
                   jax.ShapeDtypeStruct((rows, gw), BF), jax.ShapeDtypeStruct((rows, gw), vn_dtype)],
        compiler_params=_params("arbitrary"),
        name="in_proj",
    )(x2d, lw["w_in_main"], lw["q_norm_g"], lw["kv_norm_g"], lw["v_norm_g"], lw["v_norm_b"], cos_k, sin_k)


def _q_heads(cq, w_ref, wuk_ref, cos, sin, nope, rp, scale):
    hn = N_HEADS * nope
    hr = N_HEADS * rp
    q = _dot(cq, w_ref[...])
    qpe = (q[:, hn:hn + hr] * cos + q[:, hn + hr:hn + 2 * hr] * sin) * scale
    qlat = [_dot(q[:, (h // 2) * LANES:(h // 2 + 1) * LANES].astype(BF), wuk_ref[h]) * scale
            for h in range(N_HEADS)]
    return qlat, qpe


def _q_body(cq_ref, w_ref, wuk_ref, cos_ref, sin_ref, qlat_ref, qpe_ref, *, nope, rp, scale):
    qlat, qpe = _q_heads(cq_ref[...], w_ref, wuk_ref, cos_ref[...], sin_ref[...], nope, rp, scale)
    qpe_ref[...] = qpe
    for h in range(N_HEADS):
        qlat_ref[h] = qlat[h]


def _q_proj(cq, lw, cos_q, sin_q):
    rows, ql = cq.shape
    pw, _, kl, rp, gw = lw["dims"]
    hr = N_HEADS * rp
    args = [cq, lw["w_q"], lw["w_uk_pad"], cos_q, sin_q]
    return pl.pallas_call(
        functools.partial(_q_body, nope=lw["nope"], rp=rp, scale=lw["scale"]),
        grid=(1,),
        in_specs=[_const_spec(a.shape) for a in args],
        out_specs=[_const_spec((N_HEADS, rows, kl)), _const_spec((rows, hr))],
        out_shape=[jax.ShapeDtypeStruct((N_HEADS, rows, kl), F32), jax.ShapeDtypeStruct((rows, hr), F32)],
        compiler_params=_params("arbitrary"),
        name="q_proj",
    )(*args)


def _heads_out(o, wuv_ref, rows_per_head, full_m):
    outs = []
    ob = o.astype(BF)
    for j in range(N_HEADS // 2):
        acc = None
        for h in (2 * j, 2 * j + 1):
            sl = slice(h * rows_per_head, (h + 1) * rows_per_head)
            if full_m:
                y = _dot(ob, wuv_ref[h])[sl]
            else:
                y = _dot(ob[sl], wuv_ref[h])
            acc = y if acc is None else acc + y
        outs.append(acc)
    return outs


def _attn_body(qi_ref, ki_ref, cq_ref, cos_ref, sin_ref, wq_ref, wuk_ref, kv_ref, vt_ref, wuvt_ref, o_ref,
               q_ref, m_ref, l_ref, acc_ref, *, tq, kl, nope, rp, scale):
    t = pl.program_id(1)
    qi = qi_ref[t]
    ki = ki_ref[t]

    @pl.when(ki == 0)
    def _():
        m_ref[...] = jnp.full(m_ref.shape, NEG_BIG, F32)
        l_ref[...] = jnp.zeros(l_ref.shape, F32)
        acc_ref[...] = jnp.zeros(acc_ref.shape, F32)
        qlat, qpe = _q_heads(cq_ref[...], wq_ref, wuk_ref, cos_ref[...], sin_ref[...], nope, rp, scale)
        head_of_lane = lax.broadcasted_iota(jnp.int32, qpe.shape, 1) // rp
        for h in range(N_HEADS):
            q_ref[h, :, 0:kl] = qlat[h].astype(BF)
            q_ref[h, :, kl:] = jnp.where(head_of_lane == h, qpe, 0.0).astype(BF)

    def step(masked):
        half = tq // 2
        parts = [(0, half, half), (half, tq, tq)] if masked and half % LANES == 0 else [(0, tq, tq)]

        def scores(h):
            out = []
            for q0, q1, nk in parts:
                s = _dot_nt(kv_ref[0:nk, :], q_ref[h, q0:q1, :])
                if masked:
                    kpos = lax.broadcasted_iota(jnp.int32, s.shape, 0)
                    qpos = lax.broadcasted_iota(jnp.int32, s.shape, 1) + q0
                    s = jnp.where(kpos <= qpos, s, NEG_BIG)
                out.append(s)
            return out

        pending = [scores(h) for h in range(SCORE_LOOKAHEAD)]
        for h in range(N_HEADS):
            s_parts = pending.pop(0)
            if h + SCORE_LOOKAHEAD < N_HEADS:
                pending.append(scores(h + SCORE_LOOKAHEAD))
            for (q0, q1, nk), s in zip(parts, s_parts):
                m_prev = m_ref[h, :, q0:q1]
                m_new = jnp.maximum(m_prev, jnp.max(s, axis=0, keepdims=True))
                alpha = jnp.exp(m_prev - m_new)
                p = jnp.exp(s - m_new)
                l_ref[h, :, q0:q1] = alpha * l_ref[h, :, q0:q1] + jnp.sum(p, axis=0, keepdims=True)
                acc_ref[h, :, q0:q1] = alpha * acc_ref[h, :, q0:q1] + _dot(vt_ref[:, 0:nk], p.astype(BF))
                m_ref[h, :, q0:q1] = m_new

    @pl.when(ki < qi)
    def _():
        step(False)

    @pl.when(ki == qi)
    def _():
        step(True)
        outs = []
        for h in range(N_HEADS):
            o = (acc_ref[h] * (1.0 / l_ref[h])).astype(BF)
            outs.append(_dot(wuvt_ref[h], o))
        o_ref[...] = jnp.concatenate(outs, axis=0).T.astype(o_ref.dtype)


def _prompt_attention(cq, kv, vt, cos_q, sin_q, lw, *, batch, seq, tq):
    _, ql, kl, rp, _ = lw["dims"]
    width = kv.shape[-1]
    hr = N_HEADS * rp
    nq = seq // tq
    n_h, vh, _ = lw["w_uv_t"].shape
    pairs = [(i, k) for i in range(nq) for k in range(i + 1)]
    qi_tab = jnp.asarray([i for i, _ in pairs], jnp.int32)
    ki_tab = jnp.asarray([k for _, k in pairs], jnp.int32)
    tab_spec = pl.BlockSpec((tq, hr), lambda b, t, qt, kt: (qt[t], 0))
    grid_spec = pltpu.PrefetchScalarGridSpec(
        num_scalar_prefetch=2, grid=(batch, len(pairs)),
        in_specs=[pl.BlockSpec((tq, ql), lambda b, t, qt, kt: (b * nq + qt[t], 0)), tab_spec, tab_spec,
                  _const_spec(lw["w_q"].shape), _const_spec(lw["w_uk_pad"].shape),
                  pl.BlockSpec((tq, width), lambda b, t, qt, kt: (b * nq + kt[t], 0)),
                  pl.BlockSpec((kl, tq), lambda b, t, qt, kt: (0, b * nq + kt[t])),
                  _const_spec(lw["w_uv_t"].shape)],
        out_specs=pl.BlockSpec((tq, n_h * vh), lambda b, t, qt, kt: (b * nq + qt[t], 0)),
        scratch_shapes=[pltpu.VMEM((N_HEADS, tq, width), BF), pltpu.VMEM((N_HEADS, 1, tq), F32),
                        pltpu.VMEM((N_HEADS, 1, tq), F32), pltpu.VMEM((N_HEADS, kl, tq), F32)])
    return pl.pallas_call(
        functools.partial(_attn_body, tq=tq, kl=kl, nope=lw["nope"], rp=rp, scale=lw["scale"]),
        grid_spec=grid_spec,
        out_shape=jax.ShapeDtypeStruct((batch * seq, n_h * vh), BF),
        compiler_params=_params("arbitrary", "arbitrary"),
        name="prompt_attention",
    )(qi_tab, ki_tab, cq, cos_q, sin_q, lw["w_q"], lw["w_uk_pad"], kv, vt, lw["w_uv_t"])


def _decode_body(pt_ref, qlat_ref, qpe_ref, ckvn_ref, kpen_ref, wuv_ref, *rest, n_grp, n_split, tdec, rp, kl):
    ckv_refs = rest[:n_grp]
    kpe_refs = rest[n_grp:2 * n_grp]
    o_ref, q_s, qp_s, m_ref, l_ref, acc_ref = rest[2 * n_grp:]
    p_id = pl.program_id(1)
    rows = N_HEADS * tdec

    @pl.when(p_id == 0)
    def _():
        m_ref[...] = jnp.full(m_ref.shape, NEG_BIG, F32)
        l_ref[...] = jnp.zeros(l_ref.shape, F32)
        acc_ref[...] = jnp.zeros(acc_ref.shape, F32)
        q_s[...] = qlat_ref[...].reshape(rows, kl).astype(BF)
        qpe = qpe_ref[...]
        qp_s[...] = jnp.concatenate([qpe[:, h * rp:(h + 1) * rp] for h in range(N_HEADS)], axis=0).astype(BF)

    def update(g, s, keys):
        m_prev = m_ref[g]
        m_new = jnp.maximum(m_prev, jnp.max(s, axis=-1, keepdims=True))
        alpha = jnp.exp(m_prev - m_new)
        p = jnp.exp(s - m_new)
        l_ref[g] = alpha * l_ref[g] + jnp.sum(p, axis=-1, keepdims=True)
        acc = alpha * acc_ref[g]
        off = 0
        pb = p.astype(BF)
        for k in keys:
            acc = acc + _dot(pb[:, off:off + k.shape[0]], k)
            off += k.shape[0]
        acc_ref[g] = acc
        m_ref[g] = m_new

    q = q_s[...]
    qp = qp_s[...]
    per = n_grp // n_split
    keys = [r[...].astype(BF) for r in ckv_refs]
    scores = [_dot_nt(q, k) + _dot(qp, r[...].astype(BF)) for k, r in zip(keys, kpe_refs)]
    for g in range(n_split):
        update(g, jnp.concatenate(scores[g * per:(g + 1) * per], axis=1), keys[g * per:(g + 1) * per])

    @pl.when(p_id == pl.num_programs(1) - 1)
    def _():
        pad = 2 * tdec
        kn = jnp.concatenate([ckvn_ref[...], jnp.zeros((pad - tdec, kl), F32)], axis=0).astype(BF)
        kpn = jnp.concatenate([kpen_ref[...], jnp.zeros((pad - tdec, rp), F32)], axis=0).astype(BF)
        sn = _dot_nt(q, kn) + _dot_nt(qp, kpn)
        qpos = lax.broadcasted_iota(jnp.int32, (N_HEADS, tdec, pad), 1).reshape(rows, pad)
        kpos = lax.broadcasted_iota(jnp.int32, (rows, pad), 1)
        sn = jnp.where(kpos <= qpos, sn, NEG_BIG)
        update(0, sn, [kn])
        m_all = m_ref[0]
        for g in range(1, n_split):
            m_all = jnp.maximum(m_all, m_ref[g])
        l_all = jnp.zeros_like(m_all)
        o = jnp.zeros((rows, kl), F32)
        for g in range(n_split):
            w = jnp.exp(m_ref[g] - m_all)
            l_all = l_all + w * l_ref[g]
            o = o + w * acc_ref[g]
        o = o * (1.0 / l_all)
        for j, y in enumerate(_heads_out(o, wuv_ref, tdec, full_m=True)):
            o_ref[:, j * LANES:(j + 1) * LANES] = y.astype(o_ref.dtype)


def _sample_attention(qlat, qpe, ckvn, kpen, cache_ckv, cache_kpe_t, page_table, layer, lw, *, dbatch, tdec, n_grp,
                      n_split):
    _, _, kl, rp, _ = lw["dims"]
    n_pages = page_table.shape[1]
    ps = cache_ckv.shape[2]
    hv = lw["w_uv_pad"].shape[2] * N_HEADS // 2
    hr = N_HEADS * rp
    rows = N_HEADS * tdec

    def page_spec(shape, j):
        return pl.BlockSpec((None, None) + shape, lambda b, p, pt: (layer, pt[b, p * n_grp + j], 0, 0))

    in_specs = [pl.BlockSpec((N_HEADS, tdec, kl), lambda b, p, pt: (0, b, 0)),
                pl.BlockSpec((tdec, hr), lambda b, p, pt: (b, 0)),
                pl.BlockSpec((tdec, kl), lambda b, p, pt: (b, 0)),
                pl.BlockSpec((tdec, rp), lambda b, p, pt: (b, 0)),
                _const_spec(lw["w_uv_pad"].shape)]
    in_specs += [page_spec((ps, kl), j) for j in range(n_grp)]
    in_specs += [page_spec((rp, ps), j) for j in range(n_grp)]
    grid_spec = pltpu.PrefetchScalarGridSpec(
        num_scalar_prefetch=1, grid=(dbatch, n_pages // n_grp), in_specs=in_specs,
        out_specs=pl.BlockSpec((tdec, hv), lambda b, p, pt: (b, 0)),
        scratch_shapes=[pltpu.VMEM((rows, kl), BF), pltpu.VMEM((rows, rp), BF),
                        pltpu.VMEM((n_split, rows, 1), F32), pltpu.VMEM((n_split, rows, 1), F32),
                        pltpu.VMEM((n_split, rows, kl), F32)])
    return pl.pallas_call(
        functools.partial(_decode_body, n_grp=n_grp, n_split=n_split, tdec=tdec, rp=rp, kl=kl),
        grid_spec=grid_spec,
        out_shape=jax.ShapeDtypeStruct((dbatch * tdec, hv), F32),
        compiler_params=_params("arbitrary", "arbitrary"),
        name="sample_attention",
    )(page_table, qlat, qpe, ckvn, kpen, lw["w_uv_pad"], *([cache_ckv] * n_grp), *([cache_kpe_t] * n_grp))


def _window_select(sums, shape):
    gw = shape[-1] // len(POOL_WINDOWS)
    grp = lax.broadcasted_iota(jnp.int32, shape, len(shape) - 1) // gw
    out = sums[-1]
    for gi in range(len(POOL_WINDOWS) - 2, -1, -1):
        out = jnp.where(grp == gi, sums[gi], out)
    return out


def _merge_tail(x, d, u, vn, yb_pre, wg_ref, pbd_ref, psc_ref, wpo_ref, ws_ref, bs_ref, wgo_ref, wmo_ref,
                wo_ref, lng_ref, lnb_ref, alpha):
    dm = x.shape[1]
    xb = x.astype(BF)
    ya = _dot(d.astype(BF), pbd_ref[...]) * psc_ref[...]
    ya = _dot(ya.astype(BF), wpo_ref[...])
    m = jax.nn.sigmoid(_dot(xb, wg_ref[:, 0:dm])) * ya
    yb = _dot(yb_pre.astype(BF), wmo_ref[...])
    m = m + jax.nn.sigmoid(_dot(xb, wg_ref[:, dm:2 * dm])) * yb
    n_g, clen, _ = ws_ref.shape
    gc = vn.shape[1] // n_g
    grp = lax.broadcasted_iota(jnp.int32, (clen, vn.shape[1]), 1) // gc
    parts = []
    for c in range(vn.shape[0] // clen):
        vc = vn[c * clen:(c + 1) * clen]
        s = _dot(ws_ref[n_g - 1], vc)
        for g in range(n_g - 2, -1, -1):
            s = jnp.where(grp == g, _dot(ws_ref[g], vc), s)
        parts.append(s + bs_ref[...])
    s = parts[0] if len(parts) == 1 else jnp.concatenate(parts, axis=0)
    yc = _dot((u.astype(F32) * s).astype(BF), wgo_ref[...])
    m = m + jax.nn.sigmoid(_dot(xb, wg_ref[:, 2 * dm:3 * dm])) * yc
    y = alpha * x + _dot(m.astype(BF), wo_ref[...])
    return _layernorm(y, lng_ref[...], lnb_ref[...])


def _merge_prompt_body(x_ref, a_ref, aprev_ref, icnt_ref, u_ref, vn_ref, yb_ref, *rest, tiles_per_seq, alpha):
    w_refs, o_ref = rest[:-1], rest[-1]
    i = pl.program_id(0)
    a = a_ref[...]
    tm, pw = a.shape
    hist = jnp.where(i % tiles_per_seq == 0, 0.0, aprev_ref[...])
    n_hist = hist.shape[0] // SUBLANES
    ext = jnp.concatenate([hist, a], axis=0).reshape(tm // SUBLANES + n_hist, SUBLANES, pw)
    zero = jnp.zeros((1, SUBLANES, pw), F32)
    s2 = ext + _shift_rows(ext, _prev_groups(ext, zero), 1)
    s4 = s2 + _shift_rows(s2, _prev_groups(s2, zero), 2)
    s8 = s4 + _shift_rows(s4, _prev_groups(s4, zero), 4)
    s16 = s8 + _prev_groups(s8, zero)
    sel = _window_select([s[n_hist:] for s in (s2, s4, s8, s16)], (tm // SUBLANES, SUBLANES, pw))
    d = sel.reshape(tm, pw) * icnt_ref[...] - a
    o_ref[...] = _merge_tail(x_ref[...], d, u_ref[...], vn_ref[...], yb_ref[...], *w_refs, alpha)


def _merge_sample_body(x_ref, ext_ref, icnt_ref, u_ref, vn_ref, yb_ref, *rest, tdec, alpha):
    w_refs, o_ref = rest[:-1], rest[-1]
    nb, ext_len, pw = ext_ref.shape
    acc = None
    sums = []
    for j in range(max(POOL_WINDOWS)):
        cur = ext_ref[:, ext_len - tdec - j:ext_len - j, :]
        acc = cur if acc is None else acc + cur
        if j + 1 in POOL_WINDOWS:
            sums.append(acc)
    tok = ext_ref[:, ext_len - tdec:ext_len, :]
    sel = _window_select(sums, (nb, tdec, pw))
    d = sel.reshape(nb * tdec, pw) * icnt_ref[...] - tok.reshape(nb * tdec, pw)
    o_ref[...] = _merge_tail(x_ref[...], d, u_ref[...], vn_ref[...], yb_ref[...], *w_refs, alpha)


def _merge_weights(lw):
    return [lw["w_gates"], lw["pool_bd"], lw["pool_scale"], lw["w_pool_out"], lw["ws"], lw["bs"],
            lw["w_gmlp_out"], lw["w_mla_out"], lw["w_out"], lw["ln1_g"], lw["ln1_b"]]


def _merge_prompt(x2d, a_in, icnt, u, vn, yb_pre, lw, *, tm, seq):
    rows, dm = x2d.shape
    pw, _, _, _, gw = lw["dims"]
    tiles_per_seq = seq // tm
    hist_rows = 2 * SUBLANES
    hist_per_tile = tm // hist_rows
    weights = _merge_weights(lw)
    row_spec = lambda w: pl.BlockSpec((tm, w), lambda i: (i, 0))
    in_specs = [row_spec(dm), row_spec(pw),
                pl.BlockSpec((hist_rows, pw), lambda i: (jnp.maximum(i * hist_per_tile - 1, 0), 0)),
                pl.BlockSpec((tm, pw), lambda i: (i % tiles_per_seq, 0)),
                row_spec(gw), row_spec(gw), row_spec(yb_pre.shape[1])]
    in_specs += [_const_spec(w.shape) for w in weights]
    return pl.pallas_call(
        functools.partial(_merge_prompt_body, tiles_per_seq=tiles_per_seq, alpha=lw["alpha"]),
        grid=(rows // tm,), in_specs=in_specs, out_specs=row_spec(dm),
        out_shape=jax.ShapeDtypeStruct((rows, dm), F32),
        compiler_params=_params("arbitrary"),
        name="merge_prompt",
    )(x2d, a_in, a_in, icnt, u, vn, yb_pre, *weights)


def _merge_sample(x2d, ext, icnt, u, vn, yb_pre, lw, *, tdec):
    rows, dm = x2d.shape
    weights = [lw["w_gates"], lw["pool_bd"], lw["pool_scale"], lw["w_pool_out"], lw["ws_dec"], lw["bs_dec"],
               lw["w_gmlp_out"], lw["w_mla_out"], lw["w_out"], lw["ln1_g"], lw["ln1_b"]]
    args = [x2d, ext, icnt, u, vn, yb_pre] + weights
    return pl.pallas_call(
        functools.partial(_merge_sample_body, tdec=tdec, alpha=lw["alpha"]),
        grid=(1,), in_specs=[_const_spec(a.shape) for a in args], out_specs=_const_spec((rows, dm)),
        out_shape=jax.ShapeDtypeStruct((rows, dm), F32),
        compiler_params=_params("arbitrary"),
        name="merge_sample",
    )(*args)


def _conv_gate(cur, back1, back2, w, b):
    half = cur.shape[-1] // 2
    conv = (b + w[0:1] * back2 + w[1:2] * back1 + w[2:3] * cur).reshape(cur.shape[0] * SUBLANES, 2 * half)
    return jax.nn.silu(conv[:, :half]) * conv[:, half:]


def _shift_rows_ext(ext, k):
    r = pltpu.roll(ext, k, 1)
    t = lax.broadcasted_iota(jnp.int32, r[1:].shape, 1)
    return jnp.where(t >= k, r[1:], r[:-1])


def _ffn_sample_body(x_ref, prefix_ref, wup_ref, cw_ref, cb_ref, wdn_ref, lng_ref, lnb_ref, o_ref, tail_ref,
                     h_ref, *, alpha):
    n_chunks, _, cw2 = wup_ref.shape
    half = cw2 // 2
    tm = x_ref.shape[0]
    grp = tm // SUBLANES
    x = x_ref[...]
    xb = x.astype(BF)
    for c in range(n_chunks):
        a3 = _dot(xb, wup_ref[c]).reshape(grp, SUBLANES, cw2)
        prev = prefix_ref[c].reshape(grp, SUBLANES, cw2)
        h = _conv_gate(a3, _shift_rows(a3, prev, 1), _shift_rows(a3, prev, 2), cw_ref[c], cb_ref[c])
        h_ref[:, c * half:(c + 1) * half] = h.astype(BF)
        tail_ref[c] = a3[:, SUBLANES - (CONV_W - 1):, :]
    o_ref[...] = _layernorm(alpha * x + _dot(h_ref[...], wdn_ref[...]), lng_ref[...], lnb_ref[...])


def _ffn_prompt_body(x_ref, wup_ref, cw_ref, cb_ref, wdn_ref, lng_ref, lnb_ref, o_ref, tail_ref,
                     h_ref, carry_ref, *, tiles_per_seq, alpha):
    n_chunks, _, cw2 = wup_ref.shape
    half = cw2 // 2
    tm = x_ref.shape[0]
    x = x_ref[...]
    xb = x.astype(BF)

    @pl.when(pl.program_id(0) % tiles_per_seq == 0)
    def _():
        carry_ref[...] = jnp.zeros(carry_ref.shape, F32)

    def up(c):
        return _dot(xb, wup_ref[c])

    a_next = up(0)
    for c in range(n_chunks):
        a = a_next
        if c + 1 < n_chunks:
            a_next = up(c + 1)
        ext = jnp.concatenate([carry_ref[c], a], axis=0).reshape(tm // SUBLANES + 1, SUBLANES, cw2)
        h = _conv_gate(ext[1:], _shift_rows_ext(ext, 1), _shift_rows_ext(ext, 2), cw_ref[c], cb_ref[c])
        h_ref[:, c * half:(c + 1) * half] = h.astype(BF)
        last = a[tm - SUBLANES:]
        carry_ref[c] = last
        tail_ref[0, c] = last[SUBLANES - (CONV_W - 1):, :]
    o_ref[...] = _layernorm(alpha * x + _dot(h_ref[...], wdn_ref[...]), lng_ref[...], lnb_ref[...])


def _ffn(x2d, prefix, lw, *, tm, seq, n_seq):
    rows, dm = x2d.shape
    wup, cw, cb, wdn = lw["w_up_c"], lw["conv_w_c"], lw["conv_b_c"], lw["w_down"]
    n_chunks, _, cw2 = wup.shape
    has_prefix = prefix is not None
    tiles_per_seq = max(seq // tm, 1)
    keep = CONV_W - 1
    in_specs = [pl.BlockSpec((tm, dm), lambda i: (i, 0))]
    args = [x2d]
    if has_prefix:
        in_specs.append(_const_spec(prefix.shape))
        args.append(prefix)
        tail_shape = (n_chunks, rows // SUBLANES, keep, cw2)
        tail_spec = _const_spec(tail_shape)
    else:
        tail_shape = (n_seq, n_chunks, keep, cw2)
        tail_spec = pl.BlockSpec((1, n_chunks, keep, cw2), lambda i: (i // tiles_per_seq, 0, 0, 0))
    weights = [wup, cw, cb, wdn, lw["ln2_g"], lw["ln2_b"]]
    in_specs += [_const_spec(w.shape) for w in weights]
    scratch = [pltpu.VMEM((tm, wdn.shape[0]), BF)]
    if has_prefix:
        body = functools.partial(_ffn_sample_body, alpha=lw["alpha"])
    else:
        body = functools.partial(_ffn_prompt_body, tiles_per_seq=tiles_per_seq, alpha=lw["alpha"])
        scratch += [pltpu.VMEM((n_chunks, SUBLANES, cw2), F32)]
    return pl.pallas_call(
        body, grid=(rows // tm,), in_specs=in_specs,
        out_specs=[pl.BlockSpec((tm, dm), lambda i: (i, 0)), tail_spec],
        out_shape=[jax.ShapeDtypeStruct((rows, dm), F32), jax.ShapeDtypeStruct(tail_shape, F32)],
        scratch_shapes=scratch,
        compiler_params=_params("arbitrary"),
        name="ffn_sample" if has_prefix else "ffn_prompt",
    )(*args, *weights)


def _ff_chunked(v, d_ff):
    n = d_ff // FF_CHUNK
    v = v.reshape(v.shape[:-1] + (2, n, FF_CHUNK))
    v = jnp.moveaxis(v, -2, 0)
    return v.reshape(v.shape[:-2] + (2 * FF_CHUNK,))


def _ff_unchunked(v):
    n = v.shape[-3]
    v = v.reshape(v.shape[:-1] + (2, FF_CHUNK))
    v = jnp.moveaxis(v, -4, -2)
    return v.reshape(v.shape[:-3] + (2 * n * FF_CHUNK,))


def _rope_tables(pos, half):
    inv = ROPE_THETA ** (-jnp.arange(half, dtype=F32) / half)
    ang = pos.astype(F32)[:, None] * inv[None, :]
    cos, sin = jnp.cos(ang), jnp.sin(ang)
    return jnp.concatenate([cos, cos], axis=-1), jnp.concatenate([sin, sin], axis=-1)


def _rotate_half_cols(w):
    half = w.shape[-1] // 2
    return jnp.concatenate([-w[..., half:], w[..., :half]], axis=-1)


def _layer_weights(l, p, sizes, tdec, dbatch):
    pw, ql, kl, rp, gw, dm, nope, vh, d_ff = sizes
    w_in = p["w_in"][l]
    o = 0
    cols = {}
    for name, width in (("a", pw), ("cq", ql), ("ckv", kl), ("kr", rp), ("u", gw), ("v", gw), ("g", 3 * dm)):
        cols[name] = w_in[:, o:o + width]
        o += width
    kr_block = jnp.concatenate([cols["kr"], _rotate_half_cols(cols["kr"]),
                                jnp.zeros((dm, LANES - 2 * rp), F32)], axis=1)
    lw = {"dims": (pw, ql, kl, rp, gw), "nope": nope,
          "scale": float((nope + rp) ** -0.5), "alpha": float((2.0 * p["w_in"].shape[0]) ** 0.25)}
    lw["w_in_main"] = jnp.concatenate([cols["a"], cols["cq"], cols["ckv"], cols["u"], cols["v"], kr_block],
                                      axis=1).astype(BF)
    lw["w_gates"] = cols["g"].astype(BF)
    row = lambda v: v.reshape(1, -1)
    for k in ("q_norm_g", "kv_norm_g", "v_norm_g", "v_norm_b", "pool_scale", "ln1_g", "ln1_b", "ln2_g", "ln2_b"):
        lw[k] = row(p[k][l])
    w_uq = p["w_uq"][l].reshape(ql, N_HEADS, nope + rp)
    q_nope = w_uq[:, :, :nope].reshape(ql, N_HEADS * nope)
    q_pe = w_uq[:, :, nope:]
    lw["w_q"] = jnp.concatenate([q_nope, q_pe.reshape(ql, N_HEADS * rp),
                                 _rotate_half_cols(q_pe).reshape(ql, N_HEADS * rp)], axis=1).astype(BF)
    wuk = jnp.transpose(p["w_uk"][l], (1, 2, 0))
    lw["w_uk_pad"] = jnp.stack([
        jnp.pad(wuk[h], (((h % 2) * nope, LANES - nope - (h % 2) * nope), (0, 0))) for h in range(N_HEADS)
    ]).astype(BF)
    wuv = jnp.transpose(p["w_uv"][l], (1, 0, 2))
    lw["w_uv_pad"] = jnp.stack([
        jnp.pad(wuv[h], ((0, 0), ((h % 2) * vh, LANES - vh - (h % 2) * vh))) for h in range(N_HEADS)
    ]).astype(BF)
    lw["w_uv_t"] = jnp.transpose(p["w_uv"][l], (1, 2, 0)).astype(BF)
    pool_w = p["pool_w"][l]
    n_pg = pool_w.shape[0]
    lw["pool_bd"] = jax.scipy.linalg.block_diag(*[pool_w[g] for g in range(n_pg)]).astype(BF)
    lw["w_pool_out"] = p["w_pool_out"][l].astype(BF)
    lw["w_mla_out"] = p["w_mla_out"][l].astype(BF)
    lw["w_gmlp_out"] = p["w_gmlp_out"][l].astype(BF)
    lw["w_out"] = p["w_out"][l].astype(BF)
    w_s = p["w_spatial"][l]
    b_s = p["b_spatial"][l]
    n_gg, clen, _ = w_s.shape
    gc = gw // n_gg
    lw["ws"] = jnp.tril(w_s).astype(BF)
    lw["bs"] = jnp.repeat(b_s.T, gc, axis=1)
    ws_dec = jnp.tril(w_s[:, :tdec, :tdec])
    eye = jnp.eye(dbatch, dtype=F32)
    lw["ws_dec"] = jnp.stack([jnp.kron(eye, ws_dec[g]) for g in range(n_gg)]).astype(BF)
    lw["bs_dec"] = jnp.tile(jnp.repeat(b_s[:, :tdec].T, gc, axis=1), (dbatch, 1))
    lw["w_up_c"] = _ff_chunked(p["w_up"][l], d_ff).astype(BF)
    lw["conv_w_c"] = _ff_chunked(p["conv_w"][l], d_ff)
    lw["conv_b_c"] = _ff_chunked(p["conv_b"][l].reshape(1, -1), d_ff)
    lw["w_down"] = p["w_down"][l].astype(BF)
    return lw


def _inv_count(pos, pw):
    gwidth = pw // len(POOL_WINDOWS)
    win = jnp.repeat(jnp.asarray(POOL_WINDOWS, jnp.int32), gwidth)
    cnt = jnp.minimum(pos[:, None] + 1, win[None, :]).astype(F32)
    return 1.0 / cnt


def kernel(x_prompt, x_sample, cache_ckv, cache_kpe, state_pool, state_ffn, page_table, w_in, pool_w, pool_scale,
           w_pool_out, q_norm_g, w_uq, kv_norm_g, w_uk, w_uv, w_mla_out, v_norm_g, v_norm_b, w_spatial, b_spatial,
           w_gmlp_out, w_out, ln1_g, ln1_b, w_up, conv_w, conv_b, w_down, ln2_g, ln2_b):
    p = dict(w_in=w_in, pool_w=pool_w, pool_scale=pool_scale, w_pool_out=w_pool_out, q_norm_g=q_norm_g, w_uq=w_uq,
             kv_norm_g=kv_norm_g, w_uk=w_uk, w_uv=w_uv, w_mla_out=w_mla_out, v_norm_g=v_norm_g, v_norm_b=v_norm_b,
             w_spatial=w_spatial, b_spatial=b_spatial, w_gmlp_out=w_gmlp_out, w_out=w_out, ln1_g=ln1_g,
             ln1_b=ln1_b, w_up=w_up, conv_w=conv_w, conv_b=conv_b, w_down=w_down, ln2_g=ln2_g, ln2_b=ln2_b)
    batch, seq, dm = x_prompt.shape
    dbatch, tdec, _ = x_sample.shape
    depth = w_in.shape[0]
    ps = cache_ckv.shape[2]
    n_pages = page_table.shape[1]
    past = n_pages * ps
    kl, rp = cache_ckv.shape[3], cache_kpe.shape[3]
    pw = state_pool.shape[3]
    ql = q_norm_g.shape[1]
    gw = v_norm_g.shape[1]
    nope, vh = w_uk.shape[3], w_uv.shape[3]
    d_ff = w_down.shape[1]
    sizes = (pw, ql, kl, rp, gw, dm, nope, vh, d_ff)
    assert tdec == SUBLANES and pw == 2 * LANES and kl % LANES == 0 and N_HEADS * rp == 2 * LANES
    assert nope * 2 == LANES and vh * 2 == LANES and d_ff % FF_CHUNK == 0

    tm = min(512, seq)
    tq = min(512, seq)
    rows_s = dbatch * tdec
    n_grp = next(g for g in (16, 8, 4, 2, 1) if n_pages % g == 0)
    n_split = min(2, n_grp)
    assert seq % tm == 0 and seq % tq == 0
    cache_kpe_t = jnp.swapaxes(cache_kpe, 2, 3)

    pos_p = jnp.arange(seq, dtype=jnp.int32)
    pos_s = past + jnp.arange(tdec, dtype=jnp.int32)

    def tables(pos, reps):
        c, s = _rope_tables(pos, rp // 2)
        zeros = jnp.zeros((pos.shape[0], LANES - rp), F32)
        ck, sk = jnp.concatenate([c, zeros], 1), jnp.concatenate([s, zeros], 1)
        cq, sq = jnp.tile(c, (1, N_HEADS)), jnp.tile(s, (1, N_HEADS))
        ic = _inv_count(pos, pw)
        return [jnp.tile(t, (reps, 1)) for t in (ck, sk, cq, sq, ic)]

    cos_kp, sin_kp, cos_qp, sin_qp, icnt_p = tables(pos_p, 1)
    cos_ks, sin_ks, cos_qs, sin_qs, icnt_s = tables(pos_s, dbatch)

    xp = x_prompt.reshape(batch * seq, dm)
    xs = x_sample.reshape(rows_s, dm)
    outs = [[] for _ in range(9)]
    for l in range(depth):
        lw = _layer_weights(l, p, sizes, tdec, dbatch)
        a_in, cqn, ckvn, kpe, kv, vt, u, vn = _in_proj(xp, lw, cos_kp, sin_kp, tm=tm, vn_dtype=BF)
        yb_pre = _prompt_attention(cqn, kv, vt, cos_qp, sin_qp, lw, batch=batch, seq=seq, tq=tq)
        xp = _merge_prompt(xp, a_in, icnt_p, u, vn, yb_pre, lw, tm=tm, seq=seq)
        xp, tail_p = _ffn(xp, None, lw, tm=tm, seq=seq, n_seq=batch)
        outs[0].append(ckvn.reshape(batch * seq // ps, ps, kl))
        outs[1].append(kpe.reshape(batch * seq // ps, ps, rp))
        outs[4].append(a_in.reshape(batch, seq, pw)[:, seq - POOL_KEEP:])
        outs[6].append(_ff_unchunked(tail_p))
        a_s, cqn_s, ckvn_s, kpe_s, _, _, u_s, vn_s = _in_proj(xs, lw, cos_ks, sin_ks, tm=rows_s, vn_dtype=F32)
        qlat_s, qpe_s = _q_proj(cqn_s, lw, cos_qs, sin_qs)
        yb_s = _sample_attention(qlat_s, qpe_s, ckvn_s, kpe_s, cache_ckv, cache_kpe_t, page_table, l, lw,
                                 dbatch=dbatch, tdec=tdec, n_grp=n_grp, n_split=n_split)
        pool_ext = jnp.concatenate([jnp.zeros((dbatch, 1, pw), F32), state_pool[l],
                                    a_s.reshape(dbatch, tdec, pw)], axis=1)
        xs = _merge_sample(xs, pool_ext, icnt_s, u_s, vn_s.astype(BF), yb_s, lw, tdec=tdec)
        prefix = jnp.pad(state_ffn[l], ((0, 0), (SUBLANES - (CONV_W - 1), 0), (0, 0))).reshape(rows_s, 2 * d_ff)
        xs, tail_s = _ffn(xs, _ff_chunked(prefix, d_ff), lw, tm=rows_s, seq=tdec, n_seq=dbatch)
        outs[2].append(ckvn_s.reshape(dbatch, tdec, kl))
        outs[3].append(kpe_s.reshape(dbatch, tdec, rp))
        outs[5].append(pool_ext[:, -POOL_KEEP:])
        outs[7].append(_ff_unchunked(jnp.moveaxis(tail_s, 0, 1)))
        outs[8].append(vn_s.reshape(dbatch, tdec, gw))
    stacked = [jnp.stack(o) for o in outs]
    new_ckv_p, new_kpe_p, new_ckv_s, new_kpe_s, pool_p, pool_s, ffn_p, ffn_s, gv_s = stacked
    return (xp.reshape(batch, seq, dm), xs.reshape(dbatch, tdec, dm), new_ckv_p, new_kpe_p, new_ckv_s, new_kpe_s,
            pool_p, pool_s, ffn_p, ffn_s, gv_s)
```

```python
import functools

import jax
import jax.numpy as jnp
import numpy as np
from jax import lax
from jax.experimental import pallas as pl
from jax.experimental.pallas import tpu as pltpu

BF = jnp.bfloat16
F32 = jnp.float32

POOL_WINDOWS = (2, 4, 8, 16)
POOL_KEEP = max(POOL_WINDOWS) - 1
N_HEADS = 8
ROPE_THETA = 10000.0
RMS_EPS = 1e-6
LN_EPS = 1e-5
CONV_W = 3
SUBLANES = 8
LANES = 128
FF_CHUNK = 256
NEG_BIG = -1e30
SCORE_LOOKAHEAD = 4
VMEM_LIMIT = 56 * 1024 * 1024


def _params(*sem):
    return pltpu.CompilerParams(dimension_semantics=sem, vmem_limit_bytes=VMEM_LIMIT)


def _const_spec(shape):
    nd = len(shape)
    return pl.BlockSpec(shape, lambda *_: (0,) * nd, pipeline_mode=pl.Buffered(1))


def _rmsnorm(x, g):
    return x * lax.rsqrt(jnp.mean(x * x, axis=-1, keepdims=True) + RMS_EPS) * g


def _layernorm(x, g, b):
    mu = jnp.mean(x, axis=-1, keepdims=True)
    xc = x - mu
    var = jnp.mean(xc * xc, axis=-1, keepdims=True)
    return xc * lax.rsqrt(var + LN_EPS) * g + b


def _dot(a, b):
    return jnp.dot(a, b, preferred_element_type=F32)


def _dot_nt(a, b):
    return lax.dot_general(a, b, (((1,), (1,)), ((), ())), preferred_element_type=F32)


def _shift_rows(y3, prev3, k):
    t = lax.broadcasted_iota(jnp.int32, y3.shape, 1)
    return jnp.where(t >= k, pltpu.roll(y3, k, 1), pltpu.roll(prev3, k, 1))


def _prev_groups(y3, first):
    if y3.shape[0] == 1:
        return first
    return jnp.concatenate([first, y3[:-1]], axis=0)


def _in_proj_body(x_ref, w_ref, qg_ref, kvg_ref, vg_ref, vb_ref, cos_ref, sin_ref,
                  a_ref, cq_ref, ckv_ref, kpe_ref, kv_ref, vt_ref, u_ref, vn_ref, *, dims):
    pw, ql, kl, rp, gw = dims
    tm = x_ref.shape[0]
    n_part = 2 if tm % (2 * LANES) == 0 else 1
    part = tm // n_part
    zs = [_dot(x_ref[i * part:(i + 1) * part, :].astype(BF), w_ref[...]) for i in range(n_part)]
    for i, z in enumerate(zs):
        rows = slice(i * part, (i + 1) * part)
        o = 0
        a_ref[rows, :] = z[:, o:o + pw]
        o += pw
        cq_ref[rows, :] = _rmsnorm(z[:, o:o + ql], qg_ref[...]).astype(BF)
        o += ql
        ckvn = _rmsnorm(z[:, o:o + kl], kvg_ref[...])
        ckv_ref[rows, :] = ckvn
        o += kl
        u_ref[rows, :] = z[:, o:o + gw].astype(BF)
        o += gw
        vn_ref[rows, :] = _layernorm(z[:, o:o + gw], vg_ref[...], vb_ref[...]).astype(vn_ref.dtype)
        o += gw
        kr = z[:, o:o + LANES]
        kpe = kr * cos_ref[rows, :] + pltpu.roll(kr, LANES - rp, 1) * sin_ref[rows, :]
        kpe_ref[rows, :] = kpe[:, :rp]
        tiled = kpe
        for j in range(1, LANES // rp):
            tiled = tiled + pltpu.roll(kpe, j * rp, 1)
        tb = tiled.astype(BF)
        kv_ref[rows, 0:kl] = ckvn.astype(BF)
        vt_ref[0, :, rows] = ckvn.T.astype(BF)
        for j in range(N_HEADS * rp // LANES):
            kv_ref[rows, kl + j * LANES:kl + (j + 1) * LANES] = tb


def _in_proj(x2d, lw, cos_k, sin_k, *, tm, vn_dtype):
    rows, dm = x2d.shape
    dims = lw["dims"]
    pw, ql, kl, rp, gw = dims
    ntab = cos_k.shape[0] // tm
    ncol = lw["w_in_main"].shape[1]
    row_spec = lambda w: pl.BlockSpec((tm, w), lambda i: (i, 0))
    tab_spec = pl.BlockSpec((tm, LANES), lambda i: (i % ntab, 0))
    return pl.pallas_call(
        functools.partial(_in_proj_body, dims=dims),
        grid=(rows // tm,),
        in_specs=[row_spec(dm), _const_spec((dm, ncol)), _const_spec((1, ql)), _const_spec((1, kl)),
                  _const_spec((1, gw)), _const_spec((1, gw)), tab_spec, tab_spec],
        out_specs=[row_spec(pw), row_spec(ql), row_spec(kl), row_spec(rp), row_spec(kl + N_HEADS * rp),
                   pl.BlockSpec((1, kl, tm), lambda i: (i, 0, 0)), row_spec(gw), row_spec(gw)],
        out_shape=[jax.ShapeDtypeStruct((rows, pw), F32), jax.ShapeDtypeStruct((rows, ql), BF),
                   jax.ShapeDtypeStruct((rows, kl), F32), jax.ShapeDtypeStruct((rows, rp), F32),
                   jax.ShapeDtypeStruct((rows, kl + N_HEADS * rp), BF), jax.ShapeDtypeStruct((rows // tm, kl, tm), BF),
                   jax.ShapeDtypeStruct((rows, gw), BF), jax.ShapeDtypeStruct((rows, gw), vn_dtype)],
        compiler_params=_params("arbitrary"),
        name="in_proj",
    )(x2d, lw["w_in_main"], lw["q_norm_g"], lw["kv_norm_g"], lw["v_norm_g"], lw["v_norm_b"], cos_k, sin_k)


def _q_heads(cq, w_ref, wuk_ref, cos, sin, nope, rp, scale):
    hn = N_HEADS * nope
    hr = N_HEADS * rp
    q = _dot(cq, w_ref[...])
    qpe = (q[:, hn:hn + hr] * cos + q[:, hn + hr:hn + 2 * hr] * sin) * scale
    qlat = [_dot(q[:, (h // 2) * LANES:(h // 2 + 1) * LANES].astype(BF), wuk_ref[h]) * scale
            for h in range(N_HEADS)]
    return qlat, qpe


def _q_body(cq_ref, w_ref, wuk_ref, cos_ref, sin_ref, qlat_ref, qpe_ref, *, nope, rp, scale):
    qlat, qpe = _q_heads(cq_ref[...], w_ref, wuk_ref, cos_ref[...], sin_ref[...], nope, rp, scale)
    qpe_ref[...] = qpe
    for h in range(N_HEADS):
        qlat_ref[h] = qlat[h]


def _q_proj(cq, lw, cos_q, sin_q):
    rows, ql = cq.shape
    pw, _, kl, rp, gw = lw["dims"]
    hr = N_HEADS * rp
    args = [cq, lw["w_q"], lw["w_uk_pad"], cos_q, sin_q]
    return pl.pallas_call(
        functools.partial(_q_body, nope=lw["nope"], rp=rp, scale=lw["scale"]),
        grid=(1,),
        in_specs=[_const_spec(a.shape) for a in args],
        out_specs=[_const_spec((N_HEADS, rows, kl)), _const_spec((rows, hr))],
        out_shape=[jax.ShapeDtypeStruct((N_HEADS, rows, kl), F32), jax.ShapeDtypeStruct((rows, hr), F32)],
        compiler_params=_params("arbitrary"),
        name="q_proj",
    )(*args)


def _heads_out(o, wuv_ref, rows_per_head, full_m):
    outs = []
    ob = o.astype(BF)
    for j in range(N_HEADS // 2):
        acc = None
        for h in (2 * j, 2 * j + 1):
            sl = slice(h * rows_per_head, (h + 1) * rows_per_head)
            if full_m:
                y = _dot(ob, wuv_ref[h])[sl]
            else:
                y = _dot(ob[sl], wuv_ref[h])
            acc = y if acc is None else acc + y
        outs.append(acc)
    return outs


def _attn_body(cq_ref, cos_ref, sin_ref, wq_ref, wuk_ref, kv_ref, vt_ref, wuvt_ref, o_ref,
               q_ref, m_ref, l_ref, acc_ref, *, tq, kl, nope, rp, scale):
    qi = pl.program_id(1)
    m_ref[...] = jnp.full(m_ref.shape, NEG_BIG, F32)
    l_ref[...] = jnp.zeros(l_ref.shape, F32)
    acc_ref[...] = jnp.zeros(acc_ref.shape, F32)
    qlat, qpe = _q_heads(cq_ref[...], wq_ref, wuk_ref, cos_ref[...], sin_ref[...], nope, rp, scale)
    head_of_lane = lax.broadcasted_iota(jnp.int32, qpe.shape, 1) // rp
    for h in range(N_HEADS):
        q_ref[h, :, 0:kl] = qlat[h].astype(BF)
        q_ref[h, :, kl:] = jnp.where(head_of_lane == h, qpe, 0.0).astype(BF)

    def step(masked, ki):
        half = tq // 2
        parts = [(0, half, half), (half, tq, tq)] if masked and half % LANES == 0 else [(0, tq, tq)]

        def scores(h):
            out = []
            for q0, q1, nk in parts:
                s = _dot_nt(kv_ref[ki, 0:nk, :], q_ref[h, q0:q1, :])
                if masked:
                    kpos = lax.broadcasted_iota(jnp.int32, s.shape, 0)
                    qpos = lax.broadcasted_iota(jnp.int32, s.shape, 1) + q0
                    s = jnp.where(kpos <= qpos, s, NEG_BIG)
                out.append(s)
            return out

        pending = [scores(h) for h in range(SCORE_LOOKAHEAD)]
        for h in range(N_HEADS):
            s_parts = pending.pop(0)
            if h + SCORE_LOOKAHEAD < N_HEADS:
                pending.append(scores(h + SCORE_LOOKAHEAD))
            for (q0, q1, nk), s in zip(parts, s_parts):
                m_prev = m_ref[h, :, q0:q1]
                m_new = jnp.maximum(m_prev, jnp.max(s, axis=0, keepdims=True))
                alpha = jnp.exp(m_prev - m_new)
                p = jnp.exp(s - m_new)
                l_ref[h, :, q0:q1] = alpha * l_ref[h, :, q0:q1] + jnp.sum(p, axis=0, keepdims=True)
                acc_ref[h, :, q0:q1] = alpha * acc_ref[h, :, q0:q1] + _dot(vt_ref[ki, :, 0:nk], p.astype(BF))
                m_ref[h, :, q0:q1] = m_new

    def unmasked(ki, carry):
        step(False, ki)
        return carry

    lax.fori_loop(0, qi, unmasked, 0)
    step(True, qi)
    outs = []
    for h in range(N_HEADS):
        o = (acc_ref[h] * (1.0 / l_ref[h])).astype(BF)
        outs.append(_dot(wuvt_ref[h], o))
    o_ref[...] = jnp.concatenate(outs, axis=0).T.astype(o_ref.dtype)


def _prompt_attention(cq, kv, vt, cos_q, sin_q, lw, *, batch, seq, tq):
    _, ql, kl, rp, _ = lw["dims"]
    width = kv.shape[-1]
    hr = N_HEADS * rp
    nq = seq // tq
    n_h, vh, _ = lw["w_uv_t"].shape
    assert vt.shape == (batch * nq, kl, tq)
    tab_spec = pl.BlockSpec((tq, hr), lambda b, i: (i, 0))
    return pl.pallas_call(
        functools.partial(_attn_body, tq=tq, kl=kl, nope=lw["nope"], rp=rp, scale=lw["scale"]),
        grid=(batch, nq),
        in_specs=[pl.BlockSpec((tq, ql), lambda b, i: (b * nq + i, 0)), tab_spec, tab_spec,
                  _const_spec(lw["w_q"].shape), _const_spec(lw["w_uk_pad"].shape),
                  pl.BlockSpec((nq, tq, width), lambda b, i: (b, 0, 0)),
                  pl.BlockSpec((nq, kl, tq), lambda b, i: (b, 0, 0)),
                  _const_spec(lw["w_uv_t"].shape)],
        out_specs=pl.BlockSpec((tq, n_h * vh), lambda b, i: (b * nq + i, 0)),
        out_shape=jax.ShapeDtypeStruct((batch * seq, n_h * vh), BF),
        scratch_shapes=[pltpu.VMEM((N_HEADS, tq, width), BF), pltpu.VMEM((N_HEADS, 1, tq), F32),
                        pltpu.VMEM((N_HEADS, 1, tq), F32), pltpu.VMEM((N_HEADS, kl, tq), F32)],
        compiler_params=_params("arbitrary", "arbitrary"),
        name="prompt_attention",
    )(cq, cos_q, sin_q, lw["w_q"], lw["w_uk_pad"], kv.reshape(batch * nq, tq, width), vt, lw["w_uv_t"])


def _decode_body(pt_ref, qlat_ref, qpe_ref, ckvn_ref, kpen_ref, wuv_ref, ckv_hbm, kpe_hbm, o_ref,
                 kbuf, pbuf, sem_k, sem_p, *, layer, n_grp, n_groups, n_split, tdec, rp, kl):
    b = pl.program_id(0)
    rows = N_HEADS * tdec

    def page_copies(sample, group, slot, table=True):
        copies = []
        for j in range(n_grp):
            pid = pt_ref[sample, group * n_grp + j] if table else 0
            copies.append(pltpu.make_async_copy(ckv_hbm.at[layer, pid], kbuf.at[slot, j], sem_k.at[slot]))
            copies.append(pltpu.make_async_copy(kpe_hbm.at[layer, pid], pbuf.at[slot, j], sem_p.at[slot]))
        return copies

    @pl.when(b == 0)
    def _():
        for c in page_copies(0, 0, 0):
            c.start()

    q = qlat_ref[...].reshape(rows, kl).astype(BF)
    qpe = qpe_ref[...]
    qp = jnp.concatenate([qpe[:, h * rp:(h + 1) * rp] for h in range(N_HEADS)], axis=0).astype(BF)

    def update(state, s, keys):
        m_prev, l_prev, acc = state
        m_new = jnp.maximum(m_prev, jnp.max(s, axis=-1, keepdims=True))
        alpha = jnp.exp(m_prev - m_new)
        p = jnp.exp(s - m_new)
        l_new = alpha * l_prev + jnp.sum(p, axis=-1, keepdims=True)
        acc = alpha * acc
        off = 0
        pb = p.astype(BF)
        for k in keys:
            acc = acc + _dot(pb[:, off:off + k.shape[0]], k)
            off += k.shape[0]
        return m_new, l_new, acc

    states = [(jnp.full((rows, 1), NEG_BIG, F32), jnp.zeros((rows, 1), F32), jnp.zeros((rows, kl), F32))
              for _ in range(n_split)]
    per = n_grp // n_split
    for g in range(n_groups):
        slot = g % 2
        if g + 1 < n_groups:
            for c in page_copies(b, g + 1, 1 - slot):
                c.start()
        else:
            @pl.when(b + 1 < pl.num_programs(0))
            def _():
                for c in page_copies(b + 1, 0, 1 - slot):
                    c.start()
        for c in page_copies(b, g, slot, table=False):
            c.wait()
        keys = [kbuf[slot, j].astype(BF) for j in range(n_grp)]
        scores = [_dot_nt(q, k) + _dot(qp, pbuf[slot, j].astype(BF)) for j, k in enumerate(keys)]
        for i in range(n_split):
            states[i] = update(states[i], jnp.concatenate(scores[i * per:(i + 1) * per], axis=1),
                               keys[i * per:(i + 1) * per])

    pad = 2 * tdec
    kn = jnp.concatenate([ckvn_ref[...], jnp.zeros((pad - tdec, kl), F32)], axis=0).astype(BF)
    kpn = jnp.concatenate([kpen_ref[...], jnp.zeros((pad - tdec, rp), F32)], axis=0).astype(BF)
    sn = _dot_nt(q, kn) + _dot_nt(qp, kpn)
    qpos = lax.broadcasted_iota(jnp.int32, (N_HEADS, tdec, pad), 1).reshape(rows, pad)
    kpos = lax.broadcasted_iota(jnp.int32, (rows, pad), 1)
    states[0] = update(states[0], jnp.where(kpos <= qpos, sn, NEG_BIG), [kn])
    m_all = states[0][0]
    for m_i, _, _ in states[1:]:
        m_all = jnp.maximum(m_all, m_i)
    l_all = jnp.zeros_like(m_all)
    o = jnp.zeros((rows, kl), F32)
    for m_i, l_i, acc_i in states:
        w = jnp.exp(m_i - m_all)
        l_all = l_all + w * l_i
        o = o + w * acc_i
    o = o * (1.0 / l_all)
    for j, y in enumerate(_heads_out(o, wuv_ref, tdec, full_m=True)):
        o_ref[:, j * LANES:(j + 1) * LANES] = y.astype(o_ref.dtype)


def _sample_attention(qlat, qpe, ckvn, kpen, cache_ckv, cache_kpe_t, page_table, layer, lw, *, dbatch, tdec, n_grp,
                      n_split):
    _, _, kl, rp, _ = lw["dims"]
    n_pages = page_table.shape[1]
    ps = cache_ckv.shape[2]
    hv = lw["w_uv_pad"].shape[2] * N_HEADS // 2
    hr = N_HEADS * rp
    rows = N_HEADS * tdec

    n_groups = n_pages // n_grp
    assert n_groups % 2 == 0
    in_specs = [pl.BlockSpec((N_HEADS, tdec, kl), lambda b, pt: (0, b, 0)),
                pl.BlockSpec((tdec, hr), lambda b, pt: (b, 0)),
                pl.BlockSpec((tdec, kl), lambda b, pt: (b, 0)),
                pl.BlockSpec((tdec, rp), lambda b, pt: (b, 0)),
                _const_spec(lw["w_uv_pad"].shape),
                pl.BlockSpec(memory_space=pl.ANY), pl.BlockSpec(memory_space=pl.ANY)]
    grid_spec = pltpu.PrefetchScalarGridSpec(
        num_scalar_prefetch=1, grid=(dbatch,), in_specs=in_specs,
        out_specs=pl.BlockSpec((tdec, hv), lambda b, pt: (b, 0)),
        scratch_shapes=[pltpu.VMEM((2, n_grp, ps, kl), F32), pltpu.VMEM((2, n_grp, rp, ps), F32),
                        pltpu.SemaphoreType.DMA((2,)), pltpu.SemaphoreType.DMA((2,))])
    return pl.pallas_call(
        functools.partial(_decode_body, layer=layer, n_grp=n_grp, n_groups=n_groups, n_split=n_split, tdec=tdec,
                          rp=rp, kl=kl),
        grid_spec=grid_spec,
        out_shape=jax.ShapeDtypeStruct((dbatch * tdec, hv), F32),
        compiler_params=_params("arbitrary"),
        name="sample_attention",
    )(page_table, qlat, qpe, ckvn, kpen, lw["w_uv_pad"], cache_ckv, cache_kpe_t)


def _window_select(sums, shape):
    gw = shape[-1] // len(POOL_WINDOWS)
    grp = lax.broadcasted_iota(jnp.int32, shape, len(shape) - 1) // gw
    out = sums[-1]
    for gi in range(len(POOL_WINDOWS) - 2, -1, -1):
        out = jnp.where(grp == gi, sums[gi], out)
    return out


def _merge_tail(x, d, u, vn, yb_pre, wg_ref, pbd_ref, psc_ref, wpo_ref, ws_ref, bs_ref, wgo_ref, wmo_ref,
                wo_ref, lng_ref, lnb_ref, alpha):
    dm = x.shape[1]
    xb = x.astype(BF)
    ya = _dot(d.astype(BF), pbd_ref[...]) * psc_ref[...]
    ya = _dot(ya.astype(BF), wpo_ref[...])
    m = jax.nn.sigmoid(_dot(xb, wg_ref[:, 0:dm])) * ya
    yb = _dot(yb_pre.astype(BF), wmo_ref[...])
    m = m + jax.nn.sigmoid(_dot(xb, wg_ref[:, dm:2 * dm])) * yb
    n_g, clen, _ = ws_ref.shape
    gc = vn.shape[1] // n_g
    grp = lax.broadcasted_iota(jnp.int32, (clen, vn.shape[1]), 1) // gc
    parts = []
    for c in range(vn.shape[0] // clen):
        vc = vn[c * clen:(c + 1) * clen]
        s = _dot(ws_ref[n_g - 1], vc)
        for g in range(n_g - 2, -1, -1):
            s = jnp.where(grp == g, _dot(ws_ref[g], vc), s)
        parts.append(s + bs_ref[...])
    s = parts[0] if len(parts) == 1 else jnp.concatenate(parts, axis=0)
    yc = _dot((u.astype(F32) * s).astype(BF), wgo_ref[...])
    m = m + jax.nn.sigmoid(_dot(xb, wg_ref[:, 2 * dm:3 * dm])) * yc
    y = alpha * x + _dot(m.astype(BF), wo_ref[...])
    return _layernorm(y, lng_ref[...], lnb_ref[...])


def _merge_prompt_body(x_ref, a_ref, aprev_ref, icnt_ref, u_ref, vn_ref, yb_ref, *rest, tiles_per_seq, alpha):
    w_refs, o_ref = rest[:-1], rest[-1]
    i = pl.program_id(0)
    a = a_ref[...]
    tm, pw = a.shape
    hist = jnp.where(i % tiles_per_seq == 0, 0.0, aprev_ref[...])
    n_hist = hist.shape[0] // SUBLANES
    ext = jnp.concatenate([hist, a], axis=0).reshape(tm // SUBLANES + n_hist, SUBLANES, pw)
    zero = jnp.zeros((1, SUBLANES, pw), F32)
    s2 = ext + _shift_rows(ext, _prev_groups(ext, zero), 1)
    s4 = s2 + _shift_rows(s2, _prev_groups(s2, zero), 2)
    s8 = s4 + _shift_rows(s4, _prev_groups(s4, zero), 4)
    s16 = s8 + _prev_groups(s8, zero)
    sel = _window_select([s[n_hist:] for s in (s2, s4, s8, s16)], (tm // SUBLANES, SUBLANES, pw))
    d = sel.reshape(tm, pw) * icnt_ref[...] - a
    o_ref[...] = _merge_tail(x_ref[...], d, u_ref[...], vn_ref[...], yb_ref[...], *w_refs, alpha)


def _merge_sample_body(x_ref, ext_ref, icnt_ref, u_ref, vn_ref, yb_ref, *rest, tdec, alpha):
    w_refs, o_ref = rest[:-1], rest[-1]
    nb, ext_len, pw = ext_ref.shape
    acc = None
    sums = []
    for j in range(max(POOL_WINDOWS)):
        cur = ext_ref[:, ext_len - tdec - j:ext_len - j, :]
        acc = cur if acc is None else acc + cur
        if j + 1 in POOL_WINDOWS:
            sums.append(acc)
    tok = ext_ref[:, ext_len - tdec:ext_len, :]
    sel = _window_select(sums, (nb, tdec, pw))
    d = sel.reshape(nb * tdec, pw) * icnt_ref[...] - tok.reshape(nb * tdec, pw)
    o_ref[...] = _merge_tail(x_ref[...], d, u_ref[...], vn_ref[...], yb_ref[...], *w_refs, alpha)


def _merge_weights(lw):
    return [lw["w_gates"], lw["pool_bd"], lw["pool_scale"], lw["w_pool_out"], lw["ws"], lw["bs"],
            lw["w_gmlp_out"], lw["w_mla_out"], lw["w_out"], lw["ln1_g"], lw["ln1_b"]]


def _merge_prompt(x2d, a_in, icnt, u, vn, yb_pre, lw, *, tm, seq):
    rows, dm = x2d.shape
    pw, _, _, _, gw = lw["dims"]
    tiles_per_seq = seq // tm
    hist_rows = 2 * SUBLANES
    hist_per_tile = tm // hist_rows
    weights = _merge_weights(lw)
    row_spec = lambda w: pl.BlockSpec((tm, w), lambda i: (i, 0))
    in_specs = [row_spec(dm), row_spec(pw),
                pl.BlockSpec((hist_rows, pw), lambda i: (jnp.maximum(i * hist_per_tile - 1, 0), 0)),
                pl.BlockSpec((tm, pw), lambda i: (i % tiles_per_seq, 0)),
                row_spec(gw), row_spec(gw), row_spec(yb_pre.shape[1])]
    in_specs += [_const_spec(w.shape) for w in weights]
    return pl.pallas_call(
        functools.partial(_merge_prompt_body, tiles_per_seq=tiles_per_seq, alpha=lw["alpha"]),
        grid=(rows // tm,), in_specs=in_specs, out_specs=row_spec(dm),
        out_shape=jax.ShapeDtypeStruct((rows, dm), F32),
        compiler_params=_params("arbitrary"),
        name="merge_prompt",
    )(x2d, a_in, a_in, icnt, u, vn, yb_pre, *weights)


def _merge_sample(x2d, ext, icnt, u, vn, yb_pre, lw, *, tdec):
    rows, dm = x2d.shape
    weights = [lw["w_gates"], lw["pool_bd"], lw["pool_scale"], lw["w_pool_out"], lw["ws_dec"], lw["bs_dec"],
               lw["w_gmlp_out"], lw["w_mla_out"], lw["w_out"], lw["ln1_g"], lw["ln1_b"]]
    args = [x2d, ext, icnt, u, vn, yb_pre] + weights
    return pl.pallas_call(
        functools.partial(_merge_sample_body, tdec=tdec, alpha=lw["alpha"]),
        grid=(1,), in_specs=[_const_spec(a.shape) for a in args], out_specs=_const_spec((rows, dm)),
        out_shape=jax.ShapeDtypeStruct((rows, dm), F32),
        compiler_params=_params("arbitrary"),
        name="merge_sample",
    )(*args)


def _conv_gate(cur, back1, back2, w, b):
    half = cur.shape[-1] // 2
    conv = (b + w[0:1] * back2 + w[1:2] * back1 + w[2:3] * cur).reshape(cur.shape[0] * SUBLANES, 2 * half)
    return jax.nn.silu(conv[:, :half]) * conv[:, half:]


def _shift_rows_ext(ext, k):
    r = pltpu.roll(ext, k, 1)
    t = lax.broadcasted_iota(jnp.int32, r[1:].shape, 1)
    return jnp.where(t >= k, r[1:], r[:-1])


def _ff_cols(ref_or_val, c, d_ff):
    lo = c * FF_CHUNK
    return jnp.concatenate([ref_or_val[:, lo:lo + FF_CHUNK], ref_or_val[:, d_ff + lo:d_ff + lo + FF_CHUNK]],
                           axis=-1)


def _ff_store_tail(tail, c, d_ff, idx, val):
    lo = c * FF_CHUNK
    tail[idx + (slice(lo, lo + FF_CHUNK),)] = val[..., :FF_CHUNK]
    tail[idx + (slice(d_ff + lo, d_ff + lo + FF_CHUNK),)] = val[..., FF_CHUNK:]


def _ffn_sample_body(x_ref, prefix_ref, wup_ref, cw_ref, cb_ref, wdn_ref, lng_ref, lnb_ref, o_ref, tail_ref,
                     h_ref, *, alpha):
    d_ff = wdn_ref.shape[0]
    cw2 = 2 * FF_CHUNK
    tm = x_ref.shape[0]
    grp = tm // SUBLANES
    x = x_ref[...]
    xb = x.astype(BF)
    for c in range(d_ff // FF_CHUNK):
        a3 = _dot(xb, _ff_cols(wup_ref, c, d_ff)).reshape(grp, SUBLANES, cw2)
        prev = _ff_cols(prefix_ref, c, d_ff).reshape(grp, SUBLANES, cw2)
        h = _conv_gate(a3, _shift_rows(a3, prev, 1), _shift_rows(a3, prev, 2), _ff_cols(cw_ref, c, d_ff),
                       _ff_cols(cb_ref, c, d_ff))
        h_ref[:, c * FF_CHUNK:(c + 1) * FF_CHUNK] = h.astype(BF)
        _ff_store_tail(tail_ref, c, d_ff, (slice(None), slice(None)), a3[:, SUBLANES - (CONV_W - 1):, :])
    o_ref[...] = _layernorm(alpha * x + _dot(h_ref[...], wdn_ref[...]), lng_ref[...], lnb_ref[...])


def _ffn_prompt_body(x_ref, wup_ref, cw_ref, cb_ref, wdn_ref, lng_ref, lnb_ref, o_ref, tail_ref,
                     h_ref, carry_ref, *, tiles_per_seq, alpha):
    d_ff = wdn_ref.shape[0]
    n_chunks = d_ff // FF_CHUNK
    cw2 = 2 * FF_CHUNK
    tm = x_ref.shape[0]
    x = x_ref[...]
    xb = x.astype(BF)

    @pl.when(pl.program_id(0) % tiles_per_seq == 0)
    def _():
        carry_ref[...] = jnp.zeros(carry_ref.shape, F32)

    def up(c):
        return _dot(xb, _ff_cols(wup_ref, c, d_ff))

    a_next = up(0)
    for c in range(n_chunks):
        a = a_next
        if c + 1 < n_chunks:
            a_next = up(c + 1)
        ext = jnp.concatenate([carry_ref[c], a], axis=0).reshape(tm // SUBLANES + 1, SUBLANES, cw2)
        h = _conv_gate(ext[1:], _shift_rows_ext(ext, 1), _shift_rows_ext(ext, 2), _ff_cols(cw_ref, c, d_ff),
                       _ff_cols(cb_ref, c, d_ff))
        h_ref[:, c * FF_CHUNK:(c + 1) * FF_CHUNK] = h.astype(BF)
        last = a[tm - SUBLANES:]
        carry_ref[c] = last
        _ff_store_tail(tail_ref, c, d_ff, (0, slice(None)), last[SUBLANES - (CONV_W - 1):, :])
    o_ref[...] = _layernorm(alpha * x + _dot(h_ref[...], wdn_ref[...]), lng_ref[...], lnb_ref[...])


def _ffn(x2d, prefix, lw, *, tm, seq, n_seq):
    rows, dm = x2d.shape
    wup, cw, cb, wdn = lw["w_up"], lw["conv_w"], lw["conv_b"], lw["w_down"]
    d_ff = wdn.shape[0]
    n_chunks = d_ff // FF_CHUNK
    cw2 = 2 * FF_CHUNK
    has_prefix = prefix is not None
    tiles_per_seq = max(seq // tm, 1)
    keep = CONV_W - 1
    in_specs = [pl.BlockSpec((tm, dm), lambda i: (i, 0))]
    args = [x2d]
    if has_prefix:
        in_specs.append(_const_spec(prefix.shape))
        args.append(prefix)
        tail_shape = (rows // SUBLANES, keep, 2 * d_ff)
        tail_spec = _const_spec(tail_shape)
    else:
        tail_shape = (n_seq, keep, 2 * d_ff)
        tail_spec = pl.BlockSpec((1, keep, 2 * d_ff), lambda i: (i // tiles_per_seq, 0, 0))
    weights = [wup, cw, cb, wdn, lw["ln2_g"], lw["ln2_b"]]
    in_specs += [_const_spec(w.shape) for w in weights]
    scratch = [pltpu.VMEM((tm, d_ff), BF)]
    if has_prefix:
        body = functools.partial(_ffn_sample_body, alpha=lw["alpha"])
    else:
        body = functools.partial(_ffn_prompt_body, tiles_per_seq=tiles_per_seq, alpha=lw["alpha"])
        scratch += [pltpu.VMEM((n_chunks, SUBLANES, cw2), F32)]
    return pl.pallas_call(
        body, grid=(rows // tm,), in_specs=in_specs,
        out_specs=[pl.BlockSpec((tm, dm), lambda i: (i, 0)), tail_spec],
        out_shape=[jax.ShapeDtypeStruct((rows, dm), F32), jax.ShapeDtypeStruct(tail_shape, F32)],
        scratch_shapes=scratch,
        compiler_params=_params("arbitrary"),
        name="ffn_sample" if has_prefix else "ffn_prompt",
    )(*args, *weights)


def _rope_tables(pos, half):
    inv = ROPE_THETA ** (-jnp.arange(half, dtype=F32) / half)
    ang = pos.astype(F32)[:, None] * inv[None, :]
    cos, sin = jnp.cos(ang), jnp.sin(ang)
    return jnp.concatenate([cos, cos], axis=-1), jnp.concatenate([sin, sin], axis=-1)


def _rotate_half_cols(w):
    half = w.shape[-1] // 2
    return jnp.concatenate([-w[..., half:], w[..., :half]], axis=-1)


def _layer_weights(l, p, sizes, tdec, dbatch):
    pw, ql, kl, rp, gw, dm, nope, vh, d_ff = sizes
    w_in = p["w_in"][l]
    o = 0
    cols = {}
    for name, width in (("a", pw), ("cq", ql), ("ckv", kl), ("kr", rp), ("u", gw), ("v", gw), ("g", 3 * dm)):
        cols[name] = w_in[:, o:o + width]
        o += width
    kr_block = jnp.concatenate([cols["kr"], _rotate_half_cols(cols["kr"]),
                                jnp.zeros((dm, LANES - 2 * rp), F32)], axis=1)
    lw = {"dims": (pw, ql, kl, rp, gw), "nope": nope,
          "scale": float((nope + rp) ** -0.5), "alpha": float((2.0 * p["w_in"].shape[0]) ** 0.25)}
    lw["w_in_main"] = jnp.concatenate([cols["a"], cols["cq"], cols["ckv"], cols["u"], cols["v"], kr_block],
                                      axis=1).astype(BF)
    lw["w_gates"] = cols["g"].astype(BF)
    row = lambda v: v.reshape(1, -1)
    for k in ("q_norm_g", "kv_norm_g", "v_norm_g", "v_norm_b", "pool_scale", "ln1_g", "ln1_b", "ln2_g", "ln2_b"):
        lw[k] = row(p[k][l])
    w_uq = p["w_uq"][l].reshape(ql, N_HEADS, nope + rp)
    q_nope = w_uq[:, :, :nope].reshape(ql, N_HEADS * nope)
    q_pe = w_uq[:, :, nope:]
    lw["w_q"] = jnp.concatenate([q_nope, q_pe.reshape(ql, N_HEADS * rp),
                                 _rotate_half_cols(q_pe).reshape(ql, N_HEADS * rp)], axis=1).astype(BF)
    wuk = jnp.transpose(p["w_uk"][l], (1, 2, 0))
    lw["w_uk_pad"] = jnp.stack([
        jnp.pad(wuk[h], (((h % 2) * nope, LANES - nope - (h % 2) * nope), (0, 0))) for h in range(N_HEADS)
    ]).astype(BF)
    wuv = jnp.transpose(p["w_uv"][l], (1, 0, 2))
    lw["w_uv_pad"] = jnp.stack([
        jnp.pad(wuv[h], ((0, 0), ((h % 2) * vh, LANES - vh - (h % 2) * vh))) for h in range(N_HEADS)
    ]).astype(BF)
    lw["w_uv_t"] = jnp.transpose(p["w_uv"][l], (1, 2, 0)).astype(BF)
    pool_w = p["pool_w"][l]
    n_pg = pool_w.shape[0]
    lw["pool_bd"] = jax.scipy.linalg.block_diag(*[pool_w[g] for g in range(n_pg)]).astype(BF)
    lw["w_pool_out"] = p["w_pool_out"][l].astype(BF)
    lw["w_mla_out"] = p["w_mla_out"][l].astype(BF)
    lw["w_gmlp_out"] = p["w_gmlp_out"][l].astype(BF)
    lw["w_out"] = p["w_out"][l].astype(BF)
    w_s = p["w_spatial"][l]
    b_s = p["b_spatial"][l]
    n_gg, clen, _ = w_s.shape
    gc = gw // n_gg
    lw["ws"] = jnp.tril(w_s).astype(BF)
    lw["bs"] = jnp.repeat(b_s.T, gc, axis=1)
    ws_dec = jnp.tril(w_s[:, :tdec, :tdec])
    eye = jnp.eye(dbatch, dtype=F32)
    lw["ws_dec"] = jnp.stack([jnp.kron(eye, ws_dec[g]) for g in range(n_gg)]).astype(BF)
    lw["bs_dec"] = jnp.tile(jnp.repeat(b_s[:, :tdec].T, gc, axis=1), (dbatch, 1))
    lw["w_up"] = p["w_up"][l].astype(BF)
    lw["conv_w"] = p["conv_w"][l]
    lw["conv_b"] = p["conv_b"][l].reshape(1, -1)
    lw["w_down"] = p["w_down"][l].astype(BF)
    return lw


def _inv_count(pos, pw):
    gwidth = pw // len(POOL_WINDOWS)
    win = jnp.repeat(jnp.asarray(POOL_WINDOWS, jnp.int32), gwidth)
    cnt = jnp.minimum(pos[:, None] + 1, win[None, :]).astype(F32)
    return 1.0 / cnt


def kernel(x_prompt, x_sample, cache_ckv, cache_kpe, state_pool, state_ffn, page_table, w_in, pool_w, pool_scale,
           w_pool_out, q_norm_g, w_uq, kv_norm_g, w_uk, w_uv, w_mla_out, v_norm_g, v_norm_b, w_spatial, b_spatial,
           w_gmlp_out, w_out, ln1_g, ln1_b, w_up, conv_w, conv_b, w_down, ln2_g, ln2_b):
    p = dict(w_in=w_in, pool_w=pool_w, pool_scale=pool_scale, w_pool_out=w_pool_out, q_norm_g=q_norm_g, w_uq=w_uq,
             kv_norm_g=kv_norm_g, w_uk=w_uk, w_uv=w_uv, w_mla_out=w_mla_out, v_norm_g=v_norm_g, v_norm_b=v_norm_b,
             w_spatial=w_spatial, b_spatial=b_spatial, w_gmlp_out=w_gmlp_out, w_out=w_out, ln1_g=ln1_g,
             ln1_b=ln1_b, w_up=w_up, conv_w=conv_w, conv_b=conv_b, w_down=w_down, ln2_g=ln2_g, ln2_b=ln2_b)
    batch, seq, dm = x_prompt.shape
    dbatch, tdec, _ = x_sample.shape
    depth = w_in.shape[0]
    ps = cache_ckv.shape[2]
    n_pages = page_table.shape[1]
    past = n_pages * ps
    kl, rp = cache_ckv.shape[3], cache_kpe.shape[3]
    pw = state_pool.shape[3]
    ql = q_norm_g.shape[1]
    gw = v_norm_g.shape[1]
    nope, vh = w_uk.shape[3], w_uv.shape[3]
    d_ff = w_down.shape[1]
    sizes = (pw, ql, kl, rp, gw, dm, nope, vh, d_ff)
    assert tdec == SUBLANES and pw == 2 * LANES and kl % LANES == 0 and N_HEADS * rp == 2 * LANES
    assert nope * 2 == LANES and vh * 2 == LANES and d_ff % FF_CHUNK == 0

    tm = min(512, seq)
    tq = min(512, seq)
    rows_s = dbatch * tdec
    n_grp = next(g for g in (16, 8, 4, 2, 1) if n_pages % (2 * g) == 0)
    n_split = min(2, n_grp)
    assert seq % tm == 0 and tq == tm
    cache_kpe_t = jnp.swapaxes(cache_kpe, 2, 3)

    pos_p = jnp.arange(seq, dtype=jnp.int32)
    pos_s = past + jnp.arange(tdec, dtype=jnp.int32)

    def tables(pos, reps):
        c, s = _rope_tables(pos, rp // 2)
        zeros = jnp.zeros((pos.shape[0], LANES - rp), F32)
        ck, sk = jnp.concatenate([c, zeros], 1), jnp.concatenate([s, zeros], 1)
        cq, sq = jnp.tile(c, (1, N_HEADS)), jnp.tile(s, (1, N_HEADS))
        ic = _inv_count(pos, pw)
        return [jnp.tile(t, (reps, 1)) for t in (ck, sk, cq, sq, ic)]

    cos_kp, sin_kp, cos_qp, sin_qp, icnt_p = tables(pos_p, 1)
    cos_ks, sin_ks, cos_qs, sin_qs, icnt_s = tables(pos_s, dbatch)

    xp = x_prompt.reshape(batch * seq, dm)
    xs = x_sample.reshape(rows_s, dm)
    outs = [[] for _ in range(9)]
    for l in range(depth):
        lw = _layer_weights(l, p, sizes, tdec, dbatch)
        a_in, cqn, ckvn, kpe, kv, vt, u, vn = _in_proj(xp, lw, cos_kp, sin_kp, tm=tm, vn_dtype=BF)
        yb_pre = _prompt_attention(cqn, kv, vt, cos_qp, sin_qp, lw, batch=batch, seq=seq, tq=tq)
        xp = _merge_prompt(xp, a_in, icnt_p, u, vn, yb_pre, lw, tm=tm, seq=seq)
        xp, tail_p = _ffn(xp, None, lw, tm=tm, seq=seq, n_seq=batch)
        outs[0].append(ckvn.reshape(batch * seq // ps, ps, kl))
        outs[1].append(kpe.reshape(batch * seq // ps, ps, rp))
        outs[4].append(a_in.reshape(batch, seq, pw)[:, seq - POOL_KEEP:])
        outs[6].append(tail_p)
        a_s, cqn_s, ckvn_s, kpe_s, _, _, u_s, vn_s = _in_proj(xs, lw, cos_ks, sin_ks, tm=rows_s, vn_dtype=F32)
        qlat_s, qpe_s = _q_proj(cqn_s, lw, cos_qs, sin_qs)
        yb_s = _sample_attention(qlat_s, qpe_s, ckvn_s, kpe_s, cache_ckv, cache_kpe_t, page_table, l, lw,
                                 dbatch=dbatch, tdec=tdec, n_grp=n_grp, n_split=n_split)
        pool_ext = jnp.concatenate([jnp.zeros((dbatch, 1, pw), F32), state_pool[l],
                                    a_s.reshape(dbatch, tdec, pw)], axis=1)
        xs = _merge_sample(xs, pool_ext, icnt_s, u_s, vn_s.astype(BF), yb_s, lw, tdec=tdec)
        prefix = jnp.pad(state_ffn[l], ((0, 0), (SUBLANES - (CONV_W - 1), 0), (0, 0))).reshape(rows_s, 2 * d_ff)
        xs, tail_s = _ffn(xs, prefix, lw, tm=rows_s, seq=tdec, n_seq=dbatch)
        outs[2].append(ckvn_s.reshape(dbatch, tdec, kl))
        outs[3].append(kpe_s.reshape(dbatch, tdec, rp))
        outs[5].append(pool_ext[:, -POOL_KEEP:])
        outs[7].append(tail_s)
        outs[8].append(vn_s.reshape(dbatch, tdec, gw))
    stacked = [jnp.stack(o) for o in outs]
    new_ckv_p, new_kpe_p, new_ckv_s, new_kpe_s, pool_p, pool_s, ffn_p, ffn_s, gv_s = stacked
    return (xp.reshape(batch, seq, dm), xs.reshape(dbatch, tdec, dm), new_ckv_p, new_kpe_p, new_ckv_s, new_kpe_s,
            pool_p, pool_s, ffn_p, ffn_s, gv_s)
```

```python
import functools

import jax
import jax.numpy as jnp
import numpy as np
from jax import lax
from jax.experimental import pallas as pl
from jax.experimental.pallas import tpu as pltpu

BF = jnp.bfloat16
F32 = jnp.float32

POOL_WINDOWS = (2, 4, 8, 16)
POOL_KEEP = max(POOL_WINDOWS) - 1
N_HEADS = 8
ROPE_THETA = 10000.0
RMS_EPS = 1e-6
LN_EPS = 1e-5
CONV_W = 3
SUBLANES = 8
LANES = 128
FF_CHUNK = 256
NEG_BIG = -1e30
SCORE_LOOKAHEAD = 8
DECODE_SLOTS = 4
DECODE_AHEAD = 2
VMEM_LIMIT = 56 * 1024 * 1024


def _params(*sem):
    return pltpu.CompilerParams(dimension_semantics=sem, vmem_limit_bytes=VMEM_LIMIT)


class _LayerView:
    def __init__(self, stacked, layer):
        self.stacked, self.layer = stacked, layer

    @property
    def shape(self):
        return self.stacked.shape[1:]


def _const_spec(x):
    if isinstance(x, _LayerView):
        nd, layer = len(x.shape), x.layer
        return pl.BlockSpec((None,) + tuple(x.shape), lambda *_: (layer,) + (0,) * nd,
                            pipeline_mode=pl.Buffered(1))
    shape = tuple(getattr(x, "shape", x))
    nd = len(shape)
    return pl.BlockSpec(shape, lambda *_: (0,) * nd, pipeline_mode=pl.Buffered(1))


def _operands(args):
    return [a.stacked if isinstance(a, _LayerView) else a for a in args]


def _rmsnorm(x, g):
    return x * lax.rsqrt(jnp.mean(x * x, axis=-1, keepdims=True) + RMS_EPS) * g


def _layernorm(x, g, b):
    mu = jnp.mean(x, axis=-1, keepdims=True)
    xc = x - mu
    var = jnp.mean(xc * xc, axis=-1, keepdims=True)
    return xc * lax.rsqrt(var + LN_EPS) * g + b


def _dot(a, b):
    return jnp.dot(a, b, preferred_element_type=F32)


def _dot_nt(a, b):
    return lax.dot_general(a, b, (((1,), (1,)), ((), ())), preferred_element_type=F32)


def _shift_rows(y3, prev3, k):
    t = lax.broadcasted_iota(jnp.int32, y3.shape, 1)
    return jnp.where(t >= k, pltpu.roll(y3, k, 1), pltpu.roll(prev3, k, 1))


def _prev_groups(y3, first):
    if y3.shape[0] == 1:
        return first
    return jnp.concatenate([first, y3[:-1]], axis=0)


def _in_proj_body(x_ref, w_ref, qg_ref, kvg_ref, vg_ref, vb_ref, cos_ref, sin_ref,
                  a_ref, cq_ref, ckv_ref, kpe_ref, kv_ref, vt_ref, u_ref, vn_ref, *, dims):
    pw, ql, kl, rp, gw = dims
    tm = x_ref.shape[0]
    n_part = 2 if tm % (2 * LANES) == 0 else 1
    part = tm // n_part
    zs = [_dot(x_ref[i * part:(i + 1) * part, :].astype(BF), w_ref[...]) for i in range(n_part)]
    for i, z in enumerate(zs):
        rows = slice(i * part, (i + 1) * part)
        o = 0
        a_ref[rows, :] = z[:, o:o + pw]
        o += pw
        cq_ref[rows, :] = _rmsnorm(z[:, o:o + ql], qg_ref[...]).astype(BF)
        o += ql
        ckvn = _rmsnorm(z[:, o:o + kl], kvg_ref[...])
        ckv_ref[rows, :] = ckvn
        o += kl
        u_ref[rows, :] = z[:, o:o + gw].astype(BF)
        o += gw
        vn_ref[rows, :] = _layernorm(z[:, o:o + gw], vg_ref[...], vb_ref[...]).astype(vn_ref.dtype)
        o += gw
        kr = z[:, o:o + LANES]
        kpe = kr * cos_ref[rows, :] + pltpu.roll(kr, LANES - rp, 1) * sin_ref[rows, :]
        kpe_ref[rows, :] = kpe[:, :rp]
        tiled = kpe
        for j in range(1, LANES // rp):
            tiled = tiled + pltpu.roll(kpe, j * rp, 1)
        tb = tiled.astype(BF)
        kv_ref[rows, 0:kl] = ckvn.astype(BF)
        vt_ref[0, :, rows] = ckvn.T.astype(BF)
        for j in range(N_HEADS * rp // LANES):
            kv_ref[rows, kl + j * LANES:kl + (j + 1) * LANES] = tb


def _in_proj(x2d, lw, cos_k, sin_k, *, tm, vn_dtype):
    rows, dm = x2d.shape
    dims = lw["dims"]
    pw, ql, kl, rp, gw = dims
    ntab = cos_k.shape[0] // tm
    weights = [lw["w_in_main"], lw["q_norm_g"], lw["kv_norm_g"], lw["v_norm_g"], lw["v_norm_b"]]
    row_spec = lambda w: pl.BlockSpec((tm, w), lambda i: (i, 0))
    tab_spec = pl.BlockSpec((tm, LANES), lambda i: (i % ntab, 0))
    return pl.pallas_call(
        functools.partial(_in_proj_body, dims=dims),
        grid=(rows // tm,),
        in_specs=[row_spec(dm)] + [_const_spec(w) for w in weights] + [tab_spec, tab_spec],
        out_specs=[row_spec(pw), row_spec(ql), row_spec(kl), row_spec(rp), row_spec(kl + N_HEADS * rp),
                   pl.BlockSpec((1, kl, tm), lambda i: (i, 0, 0)), row_spec(gw), row_spec(gw)],
        out_shape=[jax.ShapeDtypeStruct((rows, pw), F32), jax.ShapeDtypeStruct((rows, ql), BF),
                   jax.ShapeDtypeStruct((rows, kl), F32), jax.ShapeDtypeStruct((rows, rp), F32),
                   jax.ShapeDtypeStruct((rows, kl + N_HEADS * rp), BF), jax.ShapeDtypeStruct((rows // tm, kl, tm), BF),
                   jax.ShapeDtypeStruct((rows, gw), BF), jax.ShapeDtypeStruct((rows, gw), vn_dtype)],
        compiler_params=_params("arbitrary"),
        name="in_proj",
    )(x2d, *_operands(weights), cos_k, sin_k)


def _q_heads(cq, w_ref, wuk_ref, cos, sin, nope, rp, scale):
    hn = N_HEADS * nope
    hr = N_HEADS * rp
    q = _dot(cq, w_ref[...])
    qpe = (q[:, hn:hn + hr] * cos + q[:, hn + hr:hn + 2 * hr] * sin) * scale
    qlat = [_dot(q[:, (h // 2) * LANES:(h // 2 + 1) * LANES].astype(BF), wuk_ref[h]) * scale
            for h in range(N_HEADS)]
    return qlat, qpe


def _q_body(cq_ref, w_ref, wuk_ref, cos_ref, sin_ref, qlat_ref, qpe_ref, *, nope, rp, scale):
    qlat, qpe = _q_heads(cq_ref[...], w_ref, wuk_ref, cos_ref[...], sin_ref[...], nope, rp, scale)
    qpe_ref[...] = qpe
    for h in range(N_HEADS):
        qlat_ref[h] = qlat[h]


def _q_proj(cq, lw, cos_q, sin_q):
    rows, ql = cq.shape
    pw, _, kl, rp, gw = lw["dims"]
    hr = N_HEADS * rp
    args = [cq, lw["w_q"], lw["w_uk_pad"], cos_q, sin_q]
    return pl.pallas_call(
        functools.partial(_q_body, nope=lw["nope"], rp=rp, scale=lw["scale"]),
        grid=(1,),
        in_specs=[_const_spec(a) for a in args],
        out_specs=[_const_spec((N_HEADS, rows, kl)), _const_spec((rows, hr))],
        out_shape=[jax.ShapeDtypeStruct((N_HEADS, rows, kl), F32), jax.ShapeDtypeStruct((rows, hr), F32)],
        compiler_params=_params("arbitrary"),
        name="q_proj",
    )(*_operands(args))


def _heads_out(o, wuv_ref, rows_per_head, full_m):
    outs = []
    ob = o.astype(BF)
    for j in range(N_HEADS // 2):
        acc = None
        for h in (2 * j, 2 * j + 1):
            sl = slice(h * rows_per_head, (h + 1) * rows_per_head)
            if full_m:
                y = _dot(ob, wuv_ref[h])[sl]
            else:
                y = _dot(ob[sl], wuv_ref[h])
            acc = y if acc is None else acc + y
        outs.append(acc)
    return outs


def _attn_body(cq_ref, cos_ref, sin_ref, wq_ref, wuk_ref, kv_ref, vt_ref, wuvt_ref, o_ref,
               q_ref, m_ref, l_ref, acc_ref, *, tq, kl, nope, rp, scale):
    qi = pl.program_id(1)
    m_ref[...] = jnp.full(m_ref.shape, NEG_BIG, F32)
    l_ref[...] = jnp.zeros(l_ref.shape, F32)
    acc_ref[...] = jnp.zeros(acc_ref.shape, F32)
    qlat, qpe = _q_heads(cq_ref[...], wq_ref, wuk_ref, cos_ref[...], sin_ref[...], nope, rp, scale)
    head_of_lane = lax.broadcasted_iota(jnp.int32, qpe.shape, 1) // rp
    for h in range(N_HEADS):
        q_ref[h, :, 0:kl] = qlat[h].astype(BF)
        q_ref[h, :, kl:] = jnp.where(head_of_lane == h, qpe, 0.0).astype(BF)

    def step(masked, ki):
        half = tq // 2
        parts = [(0, half, half), (half, tq, tq)] if masked and half % LANES == 0 else [(0, tq, tq)]

        def scores(h):
            out = []
            for q0, q1, nk in parts:
                s = _dot_nt(kv_ref[ki, 0:nk, :], q_ref[h, q0:q1, :])
                if masked:
                    kpos = lax.broadcasted_iota(jnp.int32, s.shape, 0)
                    qpos = lax.broadcasted_iota(jnp.int32, s.shape, 1) + q0
                    s = jnp.where(kpos <= qpos, s, NEG_BIG)
                out.append(s)
            return out

        pending = [scores(h) for h in range(SCORE_LOOKAHEAD)]
        for h in range(N_HEADS):
            s_parts = pending.pop(0)
            if h + SCORE_LOOKAHEAD < N_HEADS:
                pending.append(scores(h + SCORE_LOOKAHEAD))
            for (q0, q1, nk), s in zip(parts, s_parts):
                m_prev = m_ref[h, :, q0:q1]
                m_new = jnp.maximum(m_prev, jnp.max(s, axis=0, keepdims=True))
                alpha = jnp.exp(m_prev - m_new)
                p = jnp.exp(s - m_new)
                l_ref[h, :, q0:q1] = alpha * l_ref[h, :, q0:q1] + jnp.sum(p, axis=0, keepdims=True)
                acc_ref[h, :, q0:q1] = alpha * acc_ref[h, :, q0:q1] + _dot(vt_ref[ki, :, 0:nk], p.astype(BF))
                m_ref[h, :, q0:q1] = m_new

    def unmasked(ki, carry):
        step(False, ki)
        return carry

    lax.fori_loop(0, qi, unmasked, 0)
    step(True, qi)
    outs = []
    for h in range(N_HEADS):
        o = (acc_ref[h] * (1.0 / l_ref[h])).astype(BF)
        outs.append(_dot(wuvt_ref[h], o))
    o_ref[...] = jnp.concatenate(outs, axis=0).T.astype(o_ref.dtype)


def _prompt_attention(cq, kv, vt, cos_q, sin_q, lw, *, batch, seq, tq):
    _, ql, kl, rp, _ = lw["dims"]
    width = kv.shape[-1]
    hr = N_HEADS * rp
    nq = seq // tq
    n_h, vh, _ = lw["w_uv_t"].shape
    assert vt.shape == (batch * nq, kl, tq)
    tab_spec = pl.BlockSpec((tq, hr), lambda b, i: (i, 0))
    return pl.pallas_call(
        functools.partial(_attn_body, tq=tq, kl=kl, nope=lw["nope"], rp=rp, scale=lw["scale"]),
        grid=(batch, nq),
        in_specs=[pl.BlockSpec((tq, ql), lambda b, i: (b * nq + i, 0)), tab_spec, tab_spec,
                  _const_spec(lw["w_q"]), _const_spec(lw["w_uk_pad"]),
                  pl.BlockSpec((nq, tq, width), lambda b, i: (b, 0, 0)),
                  pl.BlockSpec((nq, kl, tq), lambda b, i: (b, 0, 0)),
                  _const_spec(lw["w_uv_t"])],
        out_specs=pl.BlockSpec((tq, n_h * vh), lambda b, i: (b * nq + i, 0)),
        out_shape=jax.ShapeDtypeStruct((batch * seq, n_h * vh), BF),
        scratch_shapes=[pltpu.VMEM((N_HEADS, tq, width), BF), pltpu.VMEM((N_HEADS, 1, tq), F32),
                        pltpu.VMEM((N_HEADS, 1, tq), F32), pltpu.VMEM((N_HEADS, kl, tq), F32)],
        compiler_params=_params("arbitrary", "arbitrary"),
        name="prompt_attention",
    )(*_operands([cq, cos_q, sin_q, lw["w_q"], lw["w_uk_pad"], kv.reshape(batch * nq, tq, width), vt,
                  lw["w_uv_t"]]))


def _decode_body(pt_ref, qlat_ref, qpe_ref, ckvn_ref, kpen_ref, wuv_ref, ckv_hbm, kpe_hbm, o_ref,
                 kbuf, pbuf, sem_k, sem_p, *, layer, n_grp, n_groups, n_split, tdec, rp, kl):
    b = pl.program_id(0)
    rows = N_HEADS * tdec

    def page_copies(sample, group, slot, table=True):
        copies = []
        for j in range(n_grp):
            pid = pt_ref[sample, group * n_grp + j] if table else 0
            copies.append(pltpu.make_async_copy(ckv_hbm.at[layer, pid], kbuf.at[slot, j], sem_k.at[slot]))
            copies.append(pltpu.make_async_copy(kpe_hbm.at[layer, pid], pbuf.at[slot, j], sem_p.at[slot]))
        return copies

    def start_ahead(g):
        nxt = g + DECODE_AHEAD
        slot = nxt % DECODE_SLOTS
        if nxt < n_groups:
            for c in page_copies(b, nxt, slot):
                c.start()
        else:
            @pl.when(b + 1 < pl.num_programs(0))
            def _():
                for c in page_copies(b + 1, nxt - n_groups, slot):
                    c.start()

    @pl.when(b == 0)
    def _():
        for g in range(DECODE_AHEAD):
            for c in page_copies(0, g, g % DECODE_SLOTS):
                c.start()

    q = qlat_ref[...].reshape(rows, kl).astype(BF)
    qpe = qpe_ref[...]
    qp = jnp.concatenate([qpe[:, h * rp:(h + 1) * rp] for h in range(N_HEADS)], axis=0).astype(BF)

    def update(state, s, keys):
        m_prev, l_prev, acc = state
        m_new = jnp.maximum(m_prev, jnp.max(s, axis=-1, keepdims=True))
        alpha = jnp.exp(m_prev - m_new)
        p = jnp.exp(s - m_new)
        l_new = alpha * l_prev + jnp.sum(p, axis=-1, keepdims=True)
        acc = alpha * acc
        off = 0
        pb = p.astype(BF)
        for k in keys:
            acc = acc + _dot(pb[:, off:off + k.shape[0]], k)
            off += k.shape[0]
        return m_new, l_new, acc

    states = [(jnp.full((rows, 1), NEG_BIG, F32), jnp.zeros((rows, 1), F32), jnp.zeros((rows, kl), F32))
              for _ in range(n_split)]
    per = n_grp // n_split
    for g in range(n_groups):
        slot = g % DECODE_SLOTS
        start_ahead(g)
        for c in page_copies(b, g, slot, table=False):
            c.wait()
        keys = [kbuf[slot, j].astype(BF) for j in range(n_grp)]
        scores = [_dot_nt(q, k) + _dot(qp, pbuf[slot, j].astype(BF)) for j, k in enumerate(keys)]
        for i in range(n_split):
            states[i] = update(states[i], jnp.concatenate(scores[i * per:(i + 1) * per], axis=1),
                               keys[i * per:(i + 1) * per])

    pad = 2 * tdec
    kn = jnp.concatenate([ckvn_ref[...], jnp.zeros((pad - tdec, kl), F32)], axis=0).astype(BF)
    kpn = jnp.concatenate([kpen_ref[...], jnp.zeros((pad - tdec, rp), F32)], axis=0).astype(BF)
    sn = _dot_nt(q, kn) + _dot_nt(qp, kpn)
    qpos = lax.broadcasted_iota(jnp.int32, (N_HEADS, tdec, pad), 1).reshape(rows, pad)
    kpos = lax.broadcasted_iota(jnp.int32, (rows, pad), 1)
    states[0] = update(states[0], jnp.where(kpos <= qpos, sn, NEG_BIG), [kn])
    m_all = states[0][0]
    for m_i, _, _ in states[1:]:
        m_all = jnp.maximum(m_all, m_i)
    l_all = jnp.zeros_like(m_all)
    o = jnp.zeros((rows, kl), F32)
    for m_i, l_i, acc_i in states:
        w = jnp.exp(m_i - m_all)
        l_all = l_all + w * l_i
        o = o + w * acc_i
    o = o * (1.0 / l_all)
    for j, y in enumerate(_heads_out(o, wuv_ref, tdec, full_m=True)):
        o_ref[:, j * LANES:(j + 1) * LANES] = y.astype(o_ref.dtype)


def _sample_attention(qlat, qpe, ckvn, kpen, cache_ckv, cache_kpe_t, page_table, layer, lw, *, dbatch, tdec, n_grp,
                      n_split):
    _, _, kl, rp, _ = lw["dims"]
    n_pages = page_table.shape[1]
    ps = cache_ckv.shape[2]
    hv = lw["w_uv_pad"].shape[2] * N_HEADS // 2
    hr = N_HEADS * rp
    rows = N_HEADS * tdec

    n_groups = n_pages // n_grp
    assert n_groups % DECODE_SLOTS == 0
    in_specs = [pl.BlockSpec((N_HEADS, tdec, kl), lambda b, pt: (0, b, 0)),
                pl.BlockSpec((tdec, hr), lambda b, pt: (b, 0)),
                pl.BlockSpec((tdec, kl), lambda b, pt: (b, 0)),
                pl.BlockSpec((tdec, rp), lambda b, pt: (b, 0)),
                _const_spec(lw["w_uv_pad"]),
                pl.BlockSpec(memory_space=pl.ANY), pl.BlockSpec(memory_space=pl.ANY)]
    grid_spec = pltpu.PrefetchScalarGridSpec(
        num_scalar_prefetch=1, grid=(dbatch,), in_specs=in_specs,
        out_specs=pl.BlockSpec((tdec, hv), lambda b, pt: (b, 0)),
        scratch_shapes=[pltpu.VMEM((DECODE_SLOTS, n_grp, ps, kl), F32), pltpu.VMEM((DECODE_SLOTS, n_grp, rp, ps), F32),
                        pltpu.SemaphoreType.DMA((DECODE_SLOTS,)), pltpu.SemaphoreType.DMA((DECODE_SLOTS,))])
    return pl.pallas_call(
        functools.partial(_decode_body, layer=layer, n_grp=n_grp, n_groups=n_groups, n_split=n_split, tdec=tdec,
                          rp=rp, kl=kl),
        grid_spec=grid_spec,
        out_shape=jax.ShapeDtypeStruct((dbatch * tdec, hv), F32),
        compiler_params=_params("arbitrary"),
        name="sample_attention",
    )(*_operands([page_table, qlat, qpe, ckvn, kpen, lw["w_uv_pad"], cache_ckv, cache_kpe_t]))


def _window_select(sums, shape):
    gw = shape[-1] // len(POOL_WINDOWS)
    grp = lax.broadcasted_iota(jnp.int32, shape, len(shape) - 1) // gw
    out = sums[-1]
    for gi in range(len(POOL_WINDOWS) - 2, -1, -1):
        out = jnp.where(grp == gi, sums[gi], out)
    return out


def _merge_tail(x, d, u, vn, yb_pre, wg_ref, pbd_ref, psc_ref, wpo_ref, ws_ref, bs_ref, wgo_ref, wmo_ref,
                wo_ref, lng_ref, lnb_ref, alpha):
    dm = x.shape[1]
    xb = x.astype(BF)
    ya = _dot(d.astype(BF), pbd_ref[...]) * psc_ref[...]
    ya = _dot(ya.astype(BF), wpo_ref[...])
    m = jax.nn.sigmoid(_dot(xb, wg_ref[:, 0:dm])) * ya
    yb = _dot(yb_pre.astype(BF), wmo_ref[...])
    m = m + jax.nn.sigmoid(_dot(xb, wg_ref[:, dm:2 * dm])) * yb
    n_g, clen, _ = ws_ref.shape
    gc = vn.shape[1] // n_g
    grp = lax.broadcasted_iota(jnp.int32, (clen, vn.shape[1]), 1) // gc
    parts = []
    for c in range(vn.shape[0] // clen):
        vc = vn[c * clen:(c + 1) * clen]
        s = _dot(ws_ref[n_g - 1], vc)
        for g in range(n_g - 2, -1, -1):
            s = jnp.where(grp == g, _dot(ws_ref[g], vc), s)
        parts.append(s + bs_ref[...])
    s = parts[0] if len(parts) == 1 else jnp.concatenate(parts, axis=0)
    yc = _dot((u.astype(F32) * s).astype(BF), wgo_ref[...])
    m = m + jax.nn.sigmoid(_dot(xb, wg_ref[:, 2 * dm:3 * dm])) * yc
    y = alpha * x + _dot(m.astype(BF), wo_ref[...])
    return _layernorm(y, lng_ref[...], lnb_ref[...])


def _merge_prompt_body(x_ref, a_ref, aprev_ref, icnt_ref, u_ref, vn_ref, yb_ref, *rest, tiles_per_seq, alpha):
    w_refs, o_ref = rest[:-1], rest[-1]
    i = pl.program_id(0)
    a = a_ref[...]
    tm, pw = a.shape
    hist = jnp.where(i % tiles_per_seq == 0, 0.0, aprev_ref[...])
    n_hist = hist.shape[0] // SUBLANES
    ext = jnp.concatenate([hist, a], axis=0).reshape(tm // SUBLANES + n_hist, SUBLANES, pw)
    zero = jnp.zeros((1, SUBLANES, pw), F32)
    s2 = ext + _shift_rows(ext, _prev_groups(ext, zero), 1)
    s4 = s2 + _shift_rows(s2, _prev_groups(s2, zero), 2)
    s8 = s4 + _shift_rows(s4, _prev_groups(s4, zero), 4)
    s16 = s8 + _prev_groups(s8, zero)
    sel = _window_select([s[n_hist:] for s in (s2, s4, s8, s16)], (tm // SUBLANES, SUBLANES, pw))
    d = sel.reshape(tm, pw) * icnt_ref[...] - a
    o_ref[...] = _merge_tail(x_ref[...], d, u_ref[...], vn_ref[...], yb_ref[...], *w_refs, alpha)


def _merge_sample_body(x_ref, ext_ref, icnt_ref, u_ref, vn_ref, yb_ref, *rest, tdec, alpha):
    w_refs, o_ref = rest[:-1], rest[-1]
    nb, ext_len, pw = ext_ref.shape
    acc = None
    sums = []
    for j in range(max(POOL_WINDOWS)):
        cur = ext_ref[:, ext_len - tdec - j:ext_len - j, :]
        acc = cur if acc is None else acc + cur
        if j + 1 in POOL_WINDOWS:
            sums.append(acc)
    tok = ext_ref[:, ext_len - tdec:ext_len, :]
    sel = _window_select(sums, (nb, tdec, pw))
    d = sel.reshape(nb * tdec, pw) * icnt_ref[...] - tok.reshape(nb * tdec, pw)
    o_ref[...] = _merge_tail(x_ref[...], d, u_ref[...], vn_ref[...], yb_ref[...], *w_refs, alpha)


def _merge_weights(lw):
    return [lw["w_gates"], lw["pool_bd"], lw["pool_scale"], lw["w_pool_out"], lw["ws"], lw["bs"],
            lw["w_gmlp_out"], lw["w_mla_out"], lw["w_out"], lw["ln1_g"], lw["ln1_b"]]


def _merge_prompt(x2d, a_in, icnt, u, vn, yb_pre, lw, *, tm, seq):
    rows, dm = x2d.shape
    pw, _, _, _, gw = lw["dims"]
    tiles_per_seq = seq // tm
    hist_rows = 2 * SUBLANES
    hist_per_tile = tm // hist_rows
    weights = _merge_weights(lw)
    row_spec = lambda w: pl.BlockSpec((tm, w), lambda i: (i, 0))
    in_specs = [row_spec(dm), row_spec(pw),
                pl.BlockSpec((hist_rows, pw), lambda i: (jnp.maximum(i * hist_per_tile - 1, 0), 0)),
                pl.BlockSpec((tm, pw), lambda i: (i % tiles_per_seq, 0)),
                row_spec(gw), row_spec(gw), row_spec(yb_pre.shape[1])]
    in_specs += [_const_spec(w) for w in weights]
    return pl.pallas_call(
        functools.partial(_merge_prompt_body, tiles_per_seq=tiles_per_seq, alpha=lw["alpha"]),
        grid=(rows // tm,), in_specs=in_specs, out_specs=row_spec(dm),
        out_shape=jax.ShapeDtypeStruct((rows, dm), F32),
        compiler_params=_params("arbitrary"),
        name="merge_prompt",
    )(x2d, a_in, a_in, icnt, u, vn, yb_pre, *_operands(weights))


def _merge_sample(x2d, ext, icnt, u, vn, yb_pre, lw, *, tdec):
    rows, dm = x2d.shape
    weights = [lw["w_gates"], lw["pool_bd"], lw["pool_scale"], lw["w_pool_out"], lw["ws_dec"], lw["bs_dec"],
               lw["w_gmlp_out"], lw["w_mla_out"], lw["w_out"], lw["ln1_g"], lw["ln1_b"]]
    args = [x2d, ext, icnt, u, vn, yb_pre] + weights
    return pl.pallas_call(
        functools.partial(_merge_sample_body, tdec=tdec, alpha=lw["alpha"]),
        grid=(1,), in_specs=[_const_spec(a) for a in args], out_specs=_const_spec((rows, dm)),
        out_shape=jax.ShapeDtypeStruct((rows, dm), F32),
        compiler_params=_params("arbitrary"),
        name="merge_sample",
    )(*_operands(args))


def _conv_gate(cur, back1, back2, w, b):
    half = cur.shape[-1] // 2
    conv = (b + w[0:1] * back2 + w[1:2] * back1 + w[2:3] * cur).reshape(cur.shape[0] * SUBLANES, 2 * half)
    return jax.nn.silu(conv[:, :half]) * conv[:, half:]


def _shift_rows_ext(ext, k):
    r = pltpu.roll(ext, k, 1)
    t = lax.broadcasted_iota(jnp.int32, r[1:].shape, 1)
    return jnp.where(t >= k, r[1:], r[:-1])


def _ff_cols(ref_or_val, c, d_ff):
    lo = c * FF_CHUNK
    return jnp.concatenate([ref_or_val[:, lo:lo + FF_CHUNK], ref_or_val[:, d_ff + lo:d_ff + lo + FF_CHUNK]],
                           axis=-1)


def _ff_store_tail(tail, c, d_ff, idx, val):
    lo = c * FF_CHUNK
    tail[idx + (slice(lo, lo + FF_CHUNK),)] = val[..., :FF_CHUNK]
    tail[idx + (slice(d_ff + lo, d_ff + lo + FF_CHUNK),)] = val[..., FF_CHUNK:]


def _ffn_sample_body(x_ref, prefix_ref, wup_ref, cw_ref, cb_ref, wdn_ref, lng_ref, lnb_ref, o_ref, tail_ref,
                     h_ref, *, alpha):
    d_ff = wdn_ref.shape[0]
    cw2 = 2 * FF_CHUNK
    tm = x_ref.shape[0]
    grp = tm // SUBLANES
    x = x_ref[...]
    xb = x.astype(BF)
    for c in range(d_ff // FF_CHUNK):
        a3 = _dot(xb, _ff_cols(wup_ref, c, d_ff)).reshape(grp, SUBLANES, cw2)
        prev = _ff_cols(prefix_ref, c, d_ff).reshape(grp, SUBLANES, cw2)
        h = _conv_gate(a3, _shift_rows(a3, prev, 1), _shift_rows(a3, prev, 2), _ff_cols(cw_ref, c, d_ff),
                       _ff_cols(cb_ref, c, d_ff))
        h_ref[:, c * FF_CHUNK:(c + 1) * FF_CHUNK] = h.astype(BF)
        _ff_store_tail(tail_ref, c, d_ff, (slice(None), slice(None)), a3[:, SUBLANES - (CONV_W - 1):, :])
    o_ref[...] = _layernorm(alpha * x + _dot(h_ref[...], wdn_ref[...]), lng_ref[...], lnb_ref[...])


def _ffn_prompt_body(x_ref, wup_ref, cw_ref, cb_ref, wdn_ref, lng_ref, lnb_ref, o_ref, tail_ref,
                     h_ref, carry_ref, *, tiles_per_seq, alpha):
    d_ff = wdn_ref.shape[0]
    n_chunks = d_ff // FF_CHUNK
    cw2 = 2 * FF_CHUNK
    tm = x_ref.shape[0]
    x = x_ref[...]
    xb = x.astype(BF)

    @pl.when(pl.program_id(0) % tiles_per_seq == 0)
    def _():
        carry_ref[...] = jnp.zeros(carry_ref.shape, F32)

    def up(c):
        return _dot(xb, _ff_cols(wup_ref, c, d_ff))

    a_next = up(0)
    for c in range(n_chunks):
        a = a_next
        if c + 1 < n_chunks:
            a_next = up(c + 1)
        ext = jnp.concatenate([carry_ref[c], a], axis=0).reshape(tm // SUBLANES + 1, SUBLANES, cw2)
        h = _conv_gate(ext[1:], _shift_rows_ext(ext, 1), _shift_rows_ext(ext, 2), _ff_cols(cw_ref, c, d_ff),
                       _ff_cols(cb_ref, c, d_ff))
        h_ref[:, c * FF_CHUNK:(c + 1) * FF_CHUNK] = h.astype(BF)
        last = a[tm - SUBLANES:]
        carry_ref[c] = last
        _ff_store_tail(tail_ref, c, d_ff, (0, slice(None)), last[SUBLANES - (CONV_W - 1):, :])
    o_ref[...] = _layernorm(alpha * x + _dot(h_ref[...], wdn_ref[...]), lng_ref[...], lnb_ref[...])


def _ffn(x2d, prefix, lw, *, tm, seq, n_seq):
    rows, dm = x2d.shape
    wup, cw, cb, wdn = lw["w_up"], lw["conv_w"], lw["conv_b"], lw["w_down"]
    d_ff = wdn.shape[0]
    n_chunks = d_ff // FF_CHUNK
    cw2 = 2 * FF_CHUNK
    has_prefix = prefix is not None
    tiles_per_seq = max(seq // tm, 1)
    keep = CONV_W - 1
    in_specs = [pl.BlockSpec((tm, dm), lambda i: (i, 0))]
    args = [x2d]
    if has_prefix:
        in_specs.append(_const_spec(prefix.shape))
        args.append(prefix)
        tail_shape = (rows // SUBLANES, keep, 2 * d_ff)
        tail_spec = _const_spec(tail_shape)
    else:
        tail_shape = (n_seq, keep, 2 * d_ff)
        tail_spec = pl.BlockSpec((1, keep, 2 * d_ff), lambda i: (i // tiles_per_seq, 0, 0))
    weights = [wup, cw, cb, wdn, lw["ln2_g"], lw["ln2_b"]]
    in_specs += [_const_spec(w) for w in weights]
    scratch = [pltpu.VMEM((tm, d_ff), BF)]
    if has_prefix:
        body = functools.partial(_ffn_sample_body, alpha=lw["alpha"])
    else:
        body = functools.partial(_ffn_prompt_body, tiles_per_seq=tiles_per_seq, alpha=lw["alpha"])
        scratch += [pltpu.VMEM((n_chunks, SUBLANES, cw2), F32)]
    return pl.pallas_call(
        body, grid=(rows // tm,), in_specs=in_specs,
        out_specs=[pl.BlockSpec((tm, dm), lambda i: (i, 0)), tail_spec],
        out_shape=[jax.ShapeDtypeStruct((rows, dm), F32), jax.ShapeDtypeStruct(tail_shape, F32)],
        scratch_shapes=scratch,
        compiler_params=_params("arbitrary"),
        name="ffn_sample" if has_prefix else "ffn_prompt",
    )(*args, *_operands(weights))


def _rope_tables(pos, half):
    inv = ROPE_THETA ** (-jnp.arange(half, dtype=F32) / half)
    ang = pos.astype(F32)[:, None] * inv[None, :]
    cos, sin = jnp.cos(ang), jnp.sin(ang)
    return jnp.concatenate([cos, cos], axis=-1), jnp.concatenate([sin, sin], axis=-1)


def _rotate_half_cols(w):
    half = w.shape[-1] // 2
    return jnp.concatenate([-w[..., half:], w[..., :half]], axis=-1)


def _paired_halves(x, axis):
    even = jnp.zeros_like(x)
    return jnp.where((jnp.arange(x.shape[1]) % 2 == 0).reshape((1, -1) + (1,) * (x.ndim - 2)),
                     jnp.concatenate([x, even], axis=axis), jnp.concatenate([even, x], axis=axis))


def _stacked_weights(p, sizes, tdec, dbatch):
    pw, ql, kl, rp, gw, dm, nope, vh, d_ff = sizes
    depth = p["w_in"].shape[0]
    w_in = p["w_in"]
    o = 0
    cols = {}
    for name, width in (("a", pw), ("cq", ql), ("ckv", kl), ("kr", rp), ("u", gw), ("v", gw), ("g", 3 * dm)):
        cols[name] = w_in[:, :, o:o + width]
        o += width
    kr_block = jnp.concatenate([cols["kr"], _rotate_half_cols(cols["kr"]),
                                jnp.zeros((depth, dm, LANES - 2 * rp), F32)], axis=2)
    sw = {}
    sw["w_in_main"] = jnp.concatenate([cols["a"], cols["cq"], cols["ckv"], cols["u"], cols["v"], kr_block],
                                      axis=2).astype(BF)
    sw["w_gates"] = cols["g"].astype(BF)
    for k in ("q_norm_g", "kv_norm_g", "v_norm_g", "v_norm_b", "pool_scale", "ln1_g", "ln1_b", "ln2_g", "ln2_b",
              "conv_b"):
        sw[k] = p[k].reshape(depth, 1, -1)
    w_uq = p["w_uq"].reshape(depth, ql, N_HEADS, nope + rp)
    q_nope = w_uq[..., :nope].reshape(depth, ql, N_HEADS * nope)
    q_pe = w_uq[..., nope:]
    sw["w_q"] = jnp.concatenate([q_nope, q_pe.reshape(depth, ql, N_HEADS * rp),
                                 _rotate_half_cols(q_pe).reshape(depth, ql, N_HEADS * rp)], axis=2).astype(BF)
    sw["w_uk_pad"] = _paired_halves(jnp.transpose(p["w_uk"], (0, 2, 3, 1)), axis=2).astype(BF)
    sw["w_uv_pad"] = _paired_halves(jnp.transpose(p["w_uv"], (0, 2, 1, 3)), axis=3).astype(BF)
    sw["w_uv_t"] = jnp.transpose(p["w_uv"], (0, 2, 3, 1)).astype(BF)
    pool_w = p["pool_w"]
    n_pg, pgw = pool_w.shape[1], pool_w.shape[2]
    sw["pool_bd"] = jnp.einsum("gh,dgij->dgihj", jnp.eye(n_pg, dtype=F32),
                               pool_w).reshape(depth, n_pg * pgw, n_pg * pgw).astype(BF)
    for k in ("w_pool_out", "w_mla_out", "w_gmlp_out", "w_out", "w_up", "w_down"):
        sw[k] = p[k].astype(BF)
    sw["conv_w"] = p["conv_w"]
    w_s = p["w_spatial"]
    b_s = p["b_spatial"]
    n_gg = w_s.shape[1]
    gc = gw // n_gg
    sw["ws"] = jnp.tril(w_s).astype(BF)
    sw["bs"] = jnp.repeat(jnp.swapaxes(b_s, 1, 2), gc, axis=2)
    ws_dec = jnp.tril(w_s[:, :, :tdec, :tdec])
    sw["ws_dec"] = jnp.einsum("ab,dgts->dgatbs", jnp.eye(dbatch, dtype=F32),
                              ws_dec).reshape(depth, n_gg, dbatch * tdec, dbatch * tdec).astype(BF)
    sw["bs_dec"] = jnp.tile(jnp.repeat(jnp.swapaxes(b_s[:, :, :tdec], 1, 2), gc, axis=2), (1, dbatch, 1))
    return sw


def _layer_weights(l, sw, sizes, depth):
    pw, ql, kl, rp, gw, dm, nope, vh, d_ff = sizes
    lw = {k: _LayerView(v, l) for k, v in sw.items()}
    lw.update({"dims": (pw, ql, kl, rp, gw), "nope": nope, "scale": float((nope + rp) ** -0.5),
               "alpha": float((2.0 * depth) ** 0.25)})
    return lw


def _inv_count(pos, pw):
    gwidth = pw // len(POOL_WINDOWS)
    win = jnp.repeat(jnp.asarray(POOL_WINDOWS, jnp.int32), gwidth)
    cnt = jnp.minimum(pos[:, None] + 1, win[None, :]).astype(F32)
    return 1.0 / cnt


def kernel(x_prompt, x_sample, cache_ckv, cache_kpe, state_pool, state_ffn, page_table, w_in, pool_w, pool_scale,
           w_pool_out, q_norm_g, w_uq, kv_norm_g, w_uk, w_uv, w_mla_out, v_norm_g, v_norm_b, w_spatial, b_spatial,
           w_gmlp_out, w_out, ln1_g, ln1_b, w_up, conv_w, conv_b, w_down, ln2_g, ln2_b):
    p = dict(w_in=w_in, pool_w=pool_w, pool_scale=pool_scale, w_pool_out=w_pool_out, q_norm_g=q_norm_g, w_uq=w_uq,
             kv_norm_g=kv_norm_g, w_uk=w_uk, w_uv=w_uv, w_mla_out=w_mla_out, v_norm_g=v_norm_g, v_norm_b=v_norm_b,
             w_spatial=w_spatial, b_spatial=b_spatial, w_gmlp_out=w_gmlp_out, w_out=w_out, ln1_g=ln1_g,
             ln1_b=ln1_b, w_up=w_up, conv_w=conv_w, conv_b=conv_b, w_down=w_down, ln2_g=ln2_g, ln2_b=ln2_b)
    batch, seq, dm = x_prompt.shape
    dbatch, tdec, _ = x_sample.shape
    depth = w_in.shape[0]
    ps = cache_ckv.shape[2]
    n_pages = page_table.shape[1]
    past = n_pages * ps
    kl, rp = cache_ckv.shape[3], cache_kpe.shape[3]
    pw = state_pool.shape[3]
    ql = q_norm_g.shape[1]
    gw = v_norm_g.shape[1]
    nope, vh = w_uk.shape[3], w_uv.shape[3]
    d_ff = w_down.shape[1]
    sizes = (pw, ql, kl, rp, gw, dm, nope, vh, d_ff)
    assert tdec == SUBLANES and pw == 2 * LANES and kl % LANES == 0 and N_HEADS * rp == 2 * LANES
    assert nope * 2 == LANES and vh * 2 == LANES and d_ff % FF_CHUNK == 0

    tm = min(512, seq)
    tq = min(512, seq)
    rows_s = dbatch * tdec
    n_grp = next(g for g in (16, 8, 4, 2, 1) if n_pages % (DECODE_SLOTS * g) == 0)
    n_split = min(2, n_grp)
    assert seq % tm == 0 and tq == tm
    cache_kpe_t = jnp.swapaxes(cache_kpe, 2, 3)

    pos_p = jnp.arange(seq, dtype=jnp.int32)
    pos_s = past + jnp.arange(tdec, dtype=jnp.int32)

    def tables(pos, reps):
        c, s = _rope_tables(pos, rp // 2)
        zeros = jnp.zeros((pos.shape[0], LANES - rp), F32)
        ck, sk = jnp.concatenate([c, zeros], 1), jnp.concatenate([s, zeros], 1)
        cq, sq = jnp.tile(c, (1, N_HEADS)), jnp.tile(s, (1, N_HEADS))
        ic = _inv_count(pos, pw)
        return [jnp.tile(t, (reps, 1)) for t in (ck, sk, cq, sq, ic)]

    cos_kp, sin_kp, cos_qp, sin_qp, icnt_p = tables(pos_p, 1)
    cos_ks, sin_ks, cos_qs, sin_qs, icnt_s = tables(pos_s, dbatch)

    xp = x_prompt.reshape(batch * seq, dm)
    xs = x_sample.reshape(rows_s, dm)
    outs = [[] for _ in range(9)]
    sw = _stacked_weights(p, sizes, tdec, dbatch)
    for l in range(depth):
        lw = _layer_weights(l, sw, sizes, depth)
        a_in, cqn, ckvn, kpe, kv, vt, u, vn = _in_proj(xp, lw, cos_kp, sin_kp, tm=tm, vn_dtype=BF)
        yb_pre = _prompt_attention(cqn, kv, vt, cos_qp, sin_qp, lw, batch=batch, seq=seq, tq=tq)
        xp = _merge_prompt(xp, a_in, icnt_p, u, vn, yb_pre, lw, tm=tm, seq=seq)
        xp, tail_p = _ffn(xp, None, lw, tm=tm, seq=seq, n_seq=batch)
        outs[0].append(ckvn.reshape(batch * seq // ps, ps, kl))
        outs[1].append(kpe.reshape(batch * seq // ps, ps, rp))
        outs[4].append(a_in.reshape(batch, seq, pw)[:, seq - POOL_KEEP:])
        outs[6].append(tail_p)
        a_s, cqn_s, ckvn_s, kpe_s, _, _, u_s, vn_s = _in_proj(xs, lw, cos_ks, sin_ks, tm=rows_s, vn_dtype=F32)
        qlat_s, qpe_s = _q_proj(cqn_s, lw, cos_qs, sin_qs)
        yb_s = _sample_attention(qlat_s, qpe_s, ckvn_s, kpe_s, cache_ckv, cache_kpe_t, page_table, l, lw,
                                 dbatch=dbatch, tdec=tdec, n_grp=n_grp, n_split=n_split)
        pool_ext = jnp.concatenate([jnp.zeros((dbatch, 1, pw), F32), state_pool[l],
                                    a_s.reshape(dbatch, tdec, pw)], axis=1)
        xs = _merge_sample(xs, pool_ext, icnt_s, u_s, vn_s.astype(BF), yb_s, lw, tdec=tdec)
        prefix = jnp.pad(state_ffn[l], ((0, 0), (SUBLANES - (CONV_W - 1), 0), (0, 0))).reshape(rows_s, 2 * d_ff)
        xs, tail_s = _ffn(xs, prefix, lw, tm=rows_s, seq=tdec, n_seq=dbatch)
        outs[2].append(ckvn_s.reshape(dbatch, tdec, kl))
        outs[3].append(kpe_s.reshape(dbatch, tdec, rp))
        outs[5].append(pool_ext[:, -POOL_KEEP:])
        outs[7].append(tail_s)
        outs[8].append(vn_s.reshape(dbatch, tdec, gw))
    stacked = [jnp.stack(o) for o in outs]
    new_ckv_p, new_kpe_p, new_ckv_s, new_kpe_s, pool_p, pool_s, ffn_p, ffn_s, gv_s = stacked
    return (xp.reshape(batch, seq, dm), xs.reshape(dbatch, tdec, dm), new_ckv_p, new_kpe_p, new_ckv_s, new_kpe_s,
            pool_p, pool_s, ffn_p, ffn_s, gv_s)
```

```python
import functools

import jax
import jax.numpy as jnp
import numpy as np
from jax import lax
from jax.experimental import pallas as pl
from jax.experimental.pallas import tpu as pltpu

BF = jnp.bfloat16
F32 = jnp.float32

POOL_WINDOWS = (2, 4, 8, 16)
POOL_KEEP = max(POOL_WINDOWS) - 1
N_HEADS = 8
ROPE_THETA = 10000.0
RMS_EPS = 1e-6
LN_EPS = 1e-5
CONV_W = 3
SUBLANES = 8
LANES = 128
FF_CHUNK = 256
NEG_BIG = -1e30
SCORE_LOOKAHEAD = 8
DECODE_SLOTS = 4
DECODE_AHEAD = 2
VMEM_LIMIT = 56 * 1024 * 1024


def _params(*sem):
    return pltpu.CompilerParams(dimension_semantics=sem, vmem_limit_bytes=VMEM_LIMIT)


class _LayerView:
    def __init__(self, stacked, layer):
        self.stacked, self.layer = stacked, layer

    @property
    def shape(self):
        return self.stacked.shape[1:]


def _const_spec(x):
    if isinstance(x, _LayerView):
        nd, layer = len(x.shape), x.layer
        return pl.BlockSpec((None,) + tuple(x.shape), lambda *_: (layer,) + (0,) * nd,
                            pipeline_mode=pl.Buffered(1))
    shape = tuple(getattr(x, "shape", x))
    nd = len(shape)
    return pl.BlockSpec(shape, lambda *_: (0,) * nd, pipeline_mode=pl.Buffered(1))


def _operands(args):
    return [a.stacked if isinstance(a, _LayerView) else a for a in args]


def _rmsnorm(x, g):
    return x * lax.rsqrt(jnp.mean(x * x, axis=-1, keepdims=True) + RMS_EPS) * g


def _layernorm(x, g, b):
    mu = jnp.mean(x, axis=-1, keepdims=True)
    xc = x - mu
    var = jnp.mean(xc * xc, axis=-1, keepdims=True)
    return xc * lax.rsqrt(var + LN_EPS) * g + b


def _dot(a, b):
    return jnp.dot(a, b, preferred_element_type=F32)


def _dot_nt(a, b):
    return lax.dot_general(a, b, (((1,), (1,)), ((), ())), preferred_element_type=F32)


def _shift_rows(y3, prev3, k):
    t = lax.broadcasted_iota(jnp.int32, y3.shape, 1)
    return jnp.where(t >= k, pltpu.roll(y3, k, 1), pltpu.roll(prev3, k, 1))


def _prev_groups(y3, first):
    if y3.shape[0] == 1:
        return first
    return jnp.concatenate([first, y3[:-1]], axis=0)


def _in_proj_body(x_ref, w_ref, qg_ref, kvg_ref, vg_ref, vb_ref, cos_ref, sin_ref,
                  a_ref, cq_ref, cqt_ref, ckv_ref, kpe_ref, kv_ref, vt_ref, u_ref, vn_ref, *, dims):
    pw, ql, kl, rp, gw = dims
    tm = x_ref.shape[0]
    n_part = 2 if tm % (2 * LANES) == 0 else 1
    part = tm // n_part
    zs = [_dot(x_ref[i * part:(i + 1) * part, :].astype(BF), w_ref[...]) for i in range(n_part)]
    for i, z in enumerate(zs):
        rows = slice(i * part, (i + 1) * part)
        o = 0
        a_ref[rows, :] = z[:, o:o + pw]
        o += pw
        cqn = _rmsnorm(z[:, o:o + ql], qg_ref[...])
        cq_ref[rows, :] = cqn.astype(BF)
        cqt_ref[0, :, rows] = cqn.T.astype(BF)
        o += ql
        ckvn = _rmsnorm(z[:, o:o + kl], kvg_ref[...])
        ckv_ref[rows, :] = ckvn
        o += kl
        u_ref[rows, :] = z[:, o:o + gw].astype(BF)
        o += gw
        vn_ref[rows, :] = _layernorm(z[:, o:o + gw], vg_ref[...], vb_ref[...]).astype(vn_ref.dtype)
        o += gw
        kr = z[:, o:o + LANES]
        kpe = kr * cos_ref[rows, :] + pltpu.roll(kr, LANES - rp, 1) * sin_ref[rows, :]
        kpe_ref[rows, :] = kpe[:, :rp]
        tiled = kpe
        for j in range(1, LANES // rp):
            tiled = tiled + pltpu.roll(kpe, j * rp, 1)
        tb = tiled.astype(BF)
        kv_ref[rows, 0:kl] = ckvn.astype(BF)
        vt_ref[0, :, rows] = ckvn.T.astype(BF)
        for j in range(N_HEADS * rp // LANES):
            kv_ref[rows, kl + j * LANES:kl + (j + 1) * LANES] = tb


def _in_proj(x2d, lw, cos_k, sin_k, *, tm, vn_dtype):
    rows, dm = x2d.shape
    dims = lw["dims"]
    pw, ql, kl, rp, gw = dims
    ntab = cos_k.shape[0] // tm
    weights = [lw["w_in_main"], lw["q_norm_g"], lw["kv_norm_g"], lw["v_norm_g"], lw["v_norm_b"]]
    row_spec = lambda w: pl.BlockSpec((tm, w), lambda i: (i, 0))
    tab_spec = pl.BlockSpec((tm, LANES), lambda i: (i % ntab, 0))
    return pl.pallas_call(
        functools.partial(_in_proj_body, dims=dims),
        grid=(rows // tm,),
        in_specs=[row_spec(dm)] + [_const_spec(w) for w in weights] + [tab_spec, tab_spec],
        out_specs=[row_spec(pw), row_spec(ql), pl.BlockSpec((1, ql, tm), lambda i: (i, 0, 0)), row_spec(kl),
                   row_spec(rp), row_spec(kl + N_HEADS * rp),
                   pl.BlockSpec((1, kl, tm), lambda i: (i, 0, 0)), row_spec(gw), row_spec(gw)],
        out_shape=[jax.ShapeDtypeStruct((rows, pw), F32), jax.ShapeDtypeStruct((rows, ql), BF),
                   jax.ShapeDtypeStruct((rows // tm, ql, tm), BF),
                   jax.ShapeDtypeStruct((rows, kl), F32), jax.ShapeDtypeStruct((rows, rp), F32),
                   jax.ShapeDtypeStruct((rows, kl + N_HEADS * rp), BF), jax.ShapeDtypeStruct((rows // tm, kl, tm), BF),
                   jax.ShapeDtypeStruct((rows, gw), BF), jax.ShapeDtypeStruct((rows, gw), vn_dtype)],
        compiler_params=_params("arbitrary"),
        name="in_proj",
    )(x2d, *_operands(weights), cos_k, sin_k)


def _q_heads(cq, w_ref, wuk_ref, cos, sin, nope, rp, scale):
    hn = N_HEADS * nope
    hr = N_HEADS * rp
    q = _dot(cq, w_ref[...])
    qpe = (q[:, hn:hn + hr] * cos + q[:, hn + hr:hn + 2 * hr] * sin) * scale
    qlat = [_dot(q[:, (h // 2) * LANES:(h // 2 + 1) * LANES].astype(BF), wuk_ref[h]) * scale
            for h in range(N_HEADS)]
    return qlat, qpe


def _q_body(cq_ref, w_ref, wuk_ref, cos_ref, sin_ref, qlat_ref, qpe_ref, *, nope, rp, scale):
    qlat, qpe = _q_heads(cq_ref[...], w_ref, wuk_ref, cos_ref[...], sin_ref[...], nope, rp, scale)
    qpe_ref[...] = qpe
    for h in range(N_HEADS):
        qlat_ref[h] = qlat[h]


def _q_proj(cq, lw, cos_q, sin_q):
    rows, ql = cq.shape
    pw, _, kl, rp, gw = lw["dims"]
    hr = N_HEADS * rp
    args = [cq, lw["w_q"], lw["w_uk_pad"], cos_q, sin_q]
    return pl.pallas_call(
        functools.partial(_q_body, nope=lw["nope"], rp=rp, scale=lw["scale"]),
        grid=(1,),
        in_specs=[_const_spec(a) for a in args],
        out_specs=[_const_spec((N_HEADS, rows, kl)), _const_spec((rows, hr))],
        out_shape=[jax.ShapeDtypeStruct((N_HEADS, rows, kl), F32), jax.ShapeDtypeStruct((rows, hr), F32)],
        compiler_params=_params("arbitrary"),
        name="q_proj",
    )(*_operands(args))


def _heads_out(o, wuv_ref, rows_per_head, full_m):
    outs = []
    ob = o.astype(BF)
    for j in range(N_HEADS // 2):
        acc = None
        for h in (2 * j, 2 * j + 1):
            sl = slice(h * rows_per_head, (h + 1) * rows_per_head)
            if full_m:
                y = _dot(ob, wuv_ref[h])[sl]
            else:
                y = _dot(ob[sl], wuv_ref[h])
            acc = y if acc is None else acc + y
        outs.append(acc)
    return outs


def _q_heads_t(cqt, wqt_ref, wukt_ref, cos_t, sin_t, nope, rp, scale):
    hn = N_HEADS * nope
    hr = N_HEADS * rp
    qt = _dot(wqt_ref[...], cqt)
    qpe_t = (qt[hn:hn + hr] * cos_t + qt[hn + hr:hn + 2 * hr] * sin_t) * scale
    qlat_t = [_dot(wukt_ref[h], qt[(h // 2) * LANES:(h // 2 + 1) * LANES].astype(BF)) * scale
              for h in range(N_HEADS)]
    return qlat_t, qpe_t


def _attn_body(cqt_ref, cos_ref, sin_ref, wqt_ref, wukt_ref, kv_ref, vt_ref, wuvt_ref, o_ref,
               q_ref, m_ref, l_ref, acc_ref, *, tq, kl, nope, rp, scale):
    qi = pl.program_id(1)
    m_ref[...] = jnp.full(m_ref.shape, NEG_BIG, F32)
    l_ref[...] = jnp.zeros(l_ref.shape, F32)
    acc_ref[...] = jnp.zeros(acc_ref.shape, F32)
    qlat_t, qpe_t = _q_heads_t(cqt_ref[0], wqt_ref, wukt_ref, cos_ref[...], sin_ref[...], nope, rp, scale)
    head_of_row = lax.broadcasted_iota(jnp.int32, qpe_t.shape, 0) // rp
    for h in range(N_HEADS):
        q_ref[h, 0:kl, :] = qlat_t[h].astype(BF)
        q_ref[h, kl:, :] = jnp.where(head_of_row == h, qpe_t, 0.0).astype(BF)

    def step(masked, ki):
        half = tq // 2
        parts = [(0, half, half), (half, tq, tq)] if masked and half % LANES == 0 else [(0, tq, tq)]

        def scores(h):
            out = []
            for q0, q1, nk in parts:
                s = _dot(kv_ref[ki, 0:nk, :], q_ref[h, :, q0:q1])
                if masked:
                    kpos = lax.broadcasted_iota(jnp.int32, s.shape, 0)
                    qpos = lax.broadcasted_iota(jnp.int32, s.shape, 1) + q0
                    s = jnp.where(kpos <= qpos, s, NEG_BIG)
                out.append(s)
            return out

        pending = [scores(h) for h in range(SCORE_LOOKAHEAD)]
        for h in range(N_HEADS):
            s_parts = pending.pop(0)
            if h + SCORE_LOOKAHEAD < N_HEADS:
                pending.append(scores(h + SCORE_LOOKAHEAD))
            for (q0, q1, nk), s in zip(parts, s_parts):
                m_prev = m_ref[h, :, q0:q1]
                m_new = jnp.maximum(m_prev, jnp.max(s, axis=0, keepdims=True))
                alpha = jnp.exp(m_prev - m_new)
                p = jnp.exp(s - m_new)
                l_ref[h, :, q0:q1] = alpha * l_ref[h, :, q0:q1] + jnp.sum(p, axis=0, keepdims=True)
                acc_ref[h, :, q0:q1] = alpha * acc_ref[h, :, q0:q1] + _dot(vt_ref[ki, :, 0:nk], p.astype(BF))
                m_ref[h, :, q0:q1] = m_new

    def unmasked(ki, carry):
        step(False, ki)
        return carry

    step(True, qi)
    lax.fori_loop(0, qi, unmasked, 0)
    outs = []
    for h in range(N_HEADS):
        o = (acc_ref[h] * (1.0 / l_ref[h])).astype(BF)
        outs.append(_dot(wuvt_ref[h], o))
    o_ref[...] = jnp.concatenate(outs, axis=0).T.astype(o_ref.dtype)


def _prompt_attention(cqt, kv, vt, cos_qt, sin_qt, lw, *, batch, seq, tq):
    _, ql, kl, rp, _ = lw["dims"]
    width = kv.shape[-1]
    hr = N_HEADS * rp
    nq = seq // tq
    n_h, vh, _ = lw["w_uv_t"].shape
    assert vt.shape == (batch * nq, kl, tq) and cqt.shape == (batch * nq, ql, tq)
    tab_spec = pl.BlockSpec((hr, tq), lambda b, i: (0, i))
    return pl.pallas_call(
        functools.partial(_attn_body, tq=tq, kl=kl, nope=lw["nope"], rp=rp, scale=lw["scale"]),
        grid=(batch, nq),
        in_specs=[pl.BlockSpec((1, ql, tq), lambda b, i: (b * nq + i, 0, 0)), tab_spec, tab_spec,
                  _const_spec(lw["w_q_t"]), _const_spec(lw["w_uk_pad_t"]),
                  pl.BlockSpec((nq, tq, width), lambda b, i: (b, 0, 0)),
                  pl.BlockSpec((nq, kl, tq), lambda b, i: (b, 0, 0)),
                  _const_spec(lw["w_uv_t"])],
        out_specs=pl.BlockSpec((tq, n_h * vh), lambda b, i: (b * nq + i, 0)),
        out_shape=jax.ShapeDtypeStruct((batch * seq, n_h * vh), BF),
        scratch_shapes=[pltpu.VMEM((N_HEADS, width, tq), BF), pltpu.VMEM((N_HEADS, 1, tq), F32),
                        pltpu.VMEM((N_HEADS, 1, tq), F32), pltpu.VMEM((N_HEADS, kl, tq), F32)],
        compiler_params=_params("arbitrary", "arbitrary"),
        name="prompt_attention",
    )(*_operands([cqt, cos_qt, sin_qt, lw["w_q_t"], lw["w_uk_pad_t"], kv.reshape(batch * nq, tq, width), vt,
                  lw["w_uv_t"]]))


def _decode_body(pt_ref, qlat_ref, qpe_ref, ckvn_ref, kpen_ref, wuv_ref, ckv_hbm, kpe_hbm, o_ref,
                 kbuf, pbuf, sem_k, sem_p, *, layer, n_grp, n_groups, n_split, tdec, rp, kl):
    b = pl.program_id(0)
    rows = N_HEADS * tdec

    def page_copies(sample, group, slot, table=True):
        copies = []
        for j in range(n_grp):
            pid = pt_ref[sample, group * n_grp + j] if table else 0
            copies.append(pltpu.make_async_copy(ckv_hbm.at[layer, pid], kbuf.at[slot, j], sem_k.at[slot]))
            copies.append(pltpu.make_async_copy(kpe_hbm.at[layer, pid], pbuf.at[slot, j], sem_p.at[slot]))
        return copies

    def start_ahead(g):
        nxt = g + DECODE_AHEAD
        slot = nxt % DECODE_SLOTS
        if nxt < n_groups:
            for c in page_copies(b, nxt, slot):
                c.start()
        else:
            @pl.when(b + 1 < pl.num_programs(0))
            def _():
                for c in page_copies(b + 1, nxt - n_groups, slot):
                    c.start()

    @pl.when(b == 0)
    def _():
        for g in range(DECODE_AHEAD):
            for c in page_copies(0, g, g % DECODE_SLOTS):
                c.start()

    q = qlat_ref[...].reshape(rows, kl).astype(BF)
    qpe = qpe_ref[...]
    qp = jnp.concatenate([qpe[:, h * rp:(h + 1) * rp] for h in range(N_HEADS)], axis=0).astype(BF)

    def update(state, s, keys):
        m_prev, l_prev, acc = state
        m_new = jnp.maximum(m_prev, jnp.max(s, axis=-1, keepdims=True))
        alpha = jnp.exp(m_prev - m_new)
        p = jnp.exp(s - m_new)
        l_new = alpha * l_prev + jnp.sum(p, axis=-1, keepdims=True)
        acc = alpha * acc
        off = 0
        pb = p.astype(BF)
        for k in keys:
            acc = acc + _dot(pb[:, off:off + k.shape[0]], k)
            off += k.shape[0]
        return m_new, l_new, acc

    states = [(jnp.full((rows, 1), NEG_BIG, F32), jnp.zeros((rows, 1), F32), jnp.zeros((rows, kl), F32))
              for _ in range(n_split)]
    per = n_grp // n_split
    for g in range(n_groups):
        slot = g % DECODE_SLOTS
        start_ahead(g)
        for c in page_copies(b, g, slot, table=False):
            c.wait()
        keys = [kbuf[slot, j].astype(BF) for j in range(n_grp)]
        scores = [_dot_nt(q, k) + _dot(qp, pbuf[slot, j].astype(BF)) for j, k in enumerate(keys)]
        for i in range(n_split):
            states[i] = update(states[i], jnp.concatenate(scores[i * per:(i + 1) * per], axis=1),
                               keys[i * per:(i + 1) * per])

    pad = 2 * tdec
    kn = jnp.concatenate([ckvn_ref[...], jnp.zeros((pad - tdec, kl), F32)], axis=0).astype(BF)
    kpn = jnp.concatenate([kpen_ref[...], jnp.zeros((pad - tdec, rp), F32)], axis=0).astype(BF)
    sn = _dot_nt(q, kn) + _dot_nt(qp, kpn)
    qpos = lax.broadcasted_iota(jnp.int32, (N_HEADS, tdec, pad), 1).reshape(rows, pad)
    kpos = lax.broadcasted_iota(jnp.int32, (rows, pad), 1)
    states[0] = update(states[0], jnp.where(kpos <= qpos, sn, NEG_BIG), [kn])
    m_all = states[0][0]
    for m_i, _, _ in states[1:]:
        m_all = jnp.maximum(m_all, m_i)
    l_all = jnp.zeros_like(m_all)
    o = jnp.zeros((rows, kl), F32)
    for m_i, l_i, acc_i in states:
        w = jnp.exp(m_i - m_all)
        l_all = l_all + w * l_i
        o = o + w * acc_i
    o = o * (1.0 / l_all)
    for j, y in enumerate(_heads_out(o, wuv_ref, tdec, full_m=True)):
        o_ref[:, j * LANES:(j + 1) * LANES] = y.astype(o_ref.dtype)


def _sample_attention(qlat, qpe, ckvn, kpen, cache_ckv, cache_kpe_t, page_table, layer, lw, *, dbatch, tdec, n_grp,
                      n_split):
    _, _, kl, rp, _ = lw["dims"]
    n_pages = page_table.shape[1]
    ps = cache_ckv.shape[2]
    hv = lw["w_uv_pad"].shape[2] * N_HEADS // 2
    hr = N_HEADS * rp
    rows = N_HEADS * tdec

    n_groups = n_pages // n_grp
    assert n_groups % DECODE_SLOTS == 0
    in_specs = [pl.BlockSpec((N_HEADS, tdec, kl), lambda b, pt: (0, b, 0)),
                pl.BlockSpec((tdec, hr), lambda b, pt: (b, 0)),
                pl.BlockSpec((tdec, kl), lambda b, pt: (b, 0)),
                pl.BlockSpec((tdec, rp), lambda b, pt: (b, 0)),
                _const_spec(lw["w_uv_pad"]),
                pl.BlockSpec(memory_space=pl.ANY), pl.BlockSpec(memory_space=pl.ANY)]
    grid_spec = pltpu.PrefetchScalarGridSpec(
        num_scalar_prefetch=1, grid=(dbatch,), in_specs=in_specs,
        out_specs=pl.BlockSpec((tdec, hv), lambda b, pt: (b, 0)),
        scratch_shapes=[pltpu.VMEM((DECODE_SLOTS, n_grp, ps, kl), F32), pltpu.VMEM((DECODE_SLOTS, n_grp, rp, ps), F32),
                        pltpu.SemaphoreType.DMA((DECODE_SLOTS,)), pltpu.SemaphoreType.DMA((DECODE_SLOTS,))])
    return pl.pallas_call(
        functools.partial(_decode_body, layer=layer, n_grp=n_grp, n_groups=n_groups, n_split=n_split, tdec=tdec,
                          rp=rp, kl=kl),
        grid_spec=grid_spec,
        out_shape=jax.ShapeDtypeStruct((dbatch * tdec, hv), F32),
        compiler_params=_params("arbitrary"),
        name="sample_attention",
    )(*_operands([page_table, qlat, qpe, ckvn, kpen, lw["w_uv_pad"], cache_ckv, cache_kpe_t]))


def _window_select(sums, shape):
    gw = shape[-1] // len(POOL_WINDOWS)
    grp = lax.broadcasted_iota(jnp.int32, shape, len(shape) - 1) // gw
    out = sums[-1]
    for gi in range(len(POOL_WINDOWS) - 2, -1, -1):
        out = jnp.where(grp == gi, sums[gi], out)
    return out


def _merge_tail(x, d, u, vn, yb_pre, wg_ref, pbd_ref, psc_ref, wpo_ref, ws_ref, bs_ref, wgo_ref, wmo_ref,
                wo_ref, lng_ref, lnb_ref, alpha):
    dm = x.shape[1]
    xb = x.astype(BF)
    ya = _dot(d.astype(BF), pbd_ref[...]) * psc_ref[...]
    ya = _dot(ya.astype(BF), wpo_ref[...])
    m = jax.nn.sigmoid(_dot(xb, wg_ref[:, 0:dm])) * ya
    yb = _dot(yb_pre.astype(BF), wmo_ref[...])
    m = m + jax.nn.sigmoid(_dot(xb, wg_ref[:, dm:2 * dm])) * yb
    n_g, clen, _ = ws_ref.shape
    gc = vn.shape[1] // n_g
    grp = lax.broadcasted_iota(jnp.int32, (clen, vn.shape[1]), 1) // gc
    parts = []
    for c in range(vn.shape[0] // clen):
        vc = vn[c * clen:(c + 1) * clen]
        s = _dot(ws_ref[n_g - 1], vc)
        for g in range(n_g - 2, -1, -1):
            s = jnp.where(grp == g, _dot(ws_ref[g], vc), s)
        parts.append(s + bs_ref[...])
    s = parts[0] if len(parts) == 1 else jnp.concatenate(parts, axis=0)
    yc = _dot((u.astype(F32) * s).astype(BF), wgo_ref[...])
    m = m + jax.nn.sigmoid(_dot(xb, wg_ref[:, 2 * dm:3 * dm])) * yc
    y = alpha * x + _dot(m.astype(BF), wo_ref[...])
    return _layernorm(y, lng_ref[...], lnb_ref[...])


def _merge_prompt_body(x_ref, a_ref, aprev_ref, icnt_ref, u_ref, vn_ref, yb_ref, *rest, tiles_per_seq, alpha):
    w_refs, o_ref = rest[:-1], rest[-1]
    i = pl.program_id(0)
    a = a_ref[...]
    tm, pw = a.shape
    hist = jnp.where(i % tiles_per_seq == 0, 0.0, aprev_ref[...])
    n_hist = hist.shape[0] // SUBLANES
    ext = jnp.concatenate([hist, a], axis=0).reshape(tm // SUBLANES + n_hist, SUBLANES, pw)
    zero = jnp.zeros((1, SUBLANES, pw), F32)
    s2 = ext + _shift_rows(ext, _prev_groups(ext, zero), 1)
    s4 = s2 + _shift_rows(s2, _prev_groups(s2, zero), 2)
    s8 = s4 + _shift_rows(s4, _prev_groups(s4, zero), 4)
    s16 = s8 + _prev_groups(s8, zero)
    sel = _window_select([s[n_hist:] for s in (s2, s4, s8, s16)], (tm // SUBLANES, SUBLANES, pw))
    d = sel.reshape(tm, pw) * icnt_ref[...] - a
    o_ref[...] = _merge_tail(x_ref[...], d, u_ref[...], vn_ref[...], yb_ref[...], *w_refs, alpha)


def _merge_sample_body(x_ref, ext_ref, icnt_ref, u_ref, vn_ref, yb_ref, *rest, tdec, alpha):
    w_refs, o_ref = rest[:-1], rest[-1]
    nb, ext_len, pw = ext_ref.shape
    acc = None
    sums = []
    for j in range(max(POOL_WINDOWS)):
        cur = ext_ref[:, ext_len - tdec - j:ext_len - j, :]
        acc = cur if acc is None else acc + cur
        if j + 1 in POOL_WINDOWS:
            sums.append(acc)
    tok = ext_ref[:, ext_len - tdec:ext_len, :]
    sel = _window_select(sums, (nb, tdec, pw))
    d = sel.reshape(nb * tdec, pw) * icnt_ref[...] - tok.reshape(nb * tdec, pw)
    o_ref[...] = _merge_tail(x_ref[...], d, u_ref[...], vn_ref[...], yb_ref[...], *w_refs, alpha)


def _merge_weights(lw):
    return [lw["w_gates"], lw["pool_bd"], lw["pool_scale"], lw["w_pool_out"], lw["ws"], lw["bs"],
            lw["w_gmlp_out"], lw["w_mla_out"], lw["w_out"], lw["ln1_g"], lw["ln1_b"]]


def _merge_prompt(x2d, a_in, icnt, u, vn, yb_pre, lw, *, tm, seq):
    rows, dm = x2d.shape
    pw, _, _, _, gw = lw["dims"]
    tiles_per_seq = seq // tm
    hist_rows = 2 * SUBLANES
    hist_per_tile = tm // hist_rows
    weights = _merge_weights(lw)
    row_spec = lambda w: pl.BlockSpec((tm, w), lambda i: (i, 0))
    in_specs = [row_spec(dm), row_spec(pw),
                pl.BlockSpec((hist_rows, pw), lambda i: (jnp.maximum(i * hist_per_tile - 1, 0), 0)),
                pl.BlockSpec((tm, pw), lambda i: (i % tiles_per_seq, 0)),
                row_spec(gw), row_spec(gw), row_spec(yb_pre.shape[1])]
    in_specs += [_const_spec(w) for w in weights]
    return pl.pallas_call(
        functools.partial(_merge_prompt_body, tiles_per_seq=tiles_per_seq, alpha=lw["alpha"]),
        grid=(rows // tm,), in_specs=in_specs, out_specs=row_spec(dm),
        out_shape=jax.ShapeDtypeStruct((rows, dm), F32),
        compiler_params=_params("arbitrary"),
        name="merge_prompt",
    )(x2d, a_in, a_in, icnt, u, vn, yb_pre, *_operands(weights))


def _merge_sample(x2d, ext, icnt, u, vn, yb_pre, lw, *, tdec):
    rows, dm = x2d.shape
    weights = [lw["w_gates"], lw["pool_bd"], lw["pool_scale"], lw["w_pool_out"], lw["ws_dec"], lw["bs_dec"],
               lw["w_gmlp_out"], lw["w_mla_out"], lw["w_out"], lw["ln1_g"], lw["ln1_b"]]
    args = [x2d, ext, icnt, u, vn, yb_pre] + weights
    return pl.pallas_call(
        functools.partial(_merge_sample_body, tdec=tdec, alpha=lw["alpha"]),
        grid=(1,), in_specs=[_const_spec(a) for a in args], out_specs=_const_spec((rows, dm)),
        out_shape=jax.ShapeDtypeStruct((rows, dm), F32),
        compiler_params=_params("arbitrary"),
        name="merge_sample",
    )(*_operands(args))


def _conv_gate(cur, back1, back2, w, b):
    half = cur.shape[-1] // 2
    conv = (b + w[0:1] * back2 + w[1:2] * back1 + w[2:3] * cur).reshape(cur.shape[0] * SUBLANES, 2 * half)
    return jax.nn.silu(conv[:, :half]) * conv[:, half:]


def _shift_rows_ext(ext, k):
    r = pltpu.roll(ext, k, 1)
    t = lax.broadcasted_iota(jnp.int32, r[1:].shape, 1)
    return jnp.where(t >= k, r[1:], r[:-1])


def _ff_cols(ref_or_val, c, d_ff):
    lo = c * FF_CHUNK
    return jnp.concatenate([ref_or_val[:, lo:lo + FF_CHUNK], ref_or_val[:, d_ff + lo:d_ff + lo + FF_CHUNK]],
                           axis=-1)


def _ff_store_tail(tail, c, d_ff, idx, val):
    lo = c * FF_CHUNK
    tail[idx + (slice(lo, lo + FF_CHUNK),)] = val[..., :FF_CHUNK]
    tail[idx + (slice(d_ff + lo, d_ff + lo + FF_CHUNK),)] = val[..., FF_CHUNK:]


def _ffn_sample_body(x_ref, prefix_ref, wup_ref, cw_ref, cb_ref, wdn_ref, lng_ref, lnb_ref, o_ref, tail_ref,
                     h_ref, *, alpha):
    d_ff = wdn_ref.shape[0]
    cw2 = 2 * FF_CHUNK
    tm = x_ref.shape[0]
    grp = tm // SUBLANES
    x = x_ref[...]
    xb = x.astype(BF)
    for c in range(d_ff // FF_CHUNK):
        a3 = _dot(xb, _ff_cols(wup_ref, c, d_ff)).reshape(grp, SUBLANES, cw2)
        prev = _ff_cols(prefix_ref, c, d_ff).reshape(grp, SUBLANES, cw2)
        h = _conv_gate(a3, _shift_rows(a3, prev, 1), _shift_rows(a3, prev, 2), _ff_cols(cw_ref, c, d_ff),
                       _ff_cols(cb_ref, c, d_ff))
        h_ref[:, c * FF_CHUNK:(c + 1) * FF_CHUNK] = h.astype(BF)
        _ff_store_tail(tail_ref, c, d_ff, (slice(None), slice(None)), a3[:, SUBLANES - (CONV_W - 1):, :])
    o_ref[...] = _layernorm(alpha * x + _dot(h_ref[...], wdn_ref[...]), lng_ref[...], lnb_ref[...])


def _ffn_prompt_body(x_ref, wup_ref, cw_ref, cb_ref, wdn_ref, lng_ref, lnb_ref, o_ref, tail_ref,
                     h_ref, carry_ref, *, tiles_per_seq, alpha):
    d_ff = wdn_ref.shape[0]
    n_chunks = d_ff // FF_CHUNK
    cw2 = 2 * FF_CHUNK
    tm = x_ref.shape[0]
    x = x_ref[...]
    xb = x.astype(BF)

    @pl.when(pl.program_id(0) % tiles_per_seq == 0)
    def _():
        carry_ref[...] = jnp.zeros(carry_ref.shape, F32)

    def up(c):
        return _dot(xb, _ff_cols(wup_ref, c, d_ff))

    a_next = up(0)
    for c in range(n_chunks):
        a = a_next
        if c + 1 < n_chunks:
            a_next = up(c + 1)
        ext = jnp.concatenate([carry_ref[c], a], axis=0).reshape(tm // SUBLANES + 1, SUBLANES, cw2)
        h = _conv_gate(ext[1:], _shift_rows_ext(ext, 1), _shift_rows_ext(ext, 2), _ff_cols(cw_ref, c, d_ff),
                       _ff_cols(cb_ref, c, d_ff))
        h_ref[:, c * FF_CHUNK:(c + 1) * FF_CHUNK] = h.astype(BF)
        last = a[tm - SUBLANES:]
        carry_ref[c] = last
        _ff_store_tail(tail_ref, c, d_ff, (0, slice(None)), last[SUBLANES - (CONV_W - 1):, :])
    o_ref[...] = _layernorm(alpha * x + _dot(h_ref[...], wdn_ref[...]), lng_ref[...], lnb_ref[...])


def _ffn(x2d, prefix, lw, *, tm, seq, n_seq):
    rows, dm = x2d.shape
    wup, cw, cb, wdn = lw["w_up"], lw["conv_w"], lw["conv_b"], lw["w_down"]
    d_ff = wdn.shape[0]
    n_chunks = d_ff // FF_CHUNK
    cw2 = 2 * FF_CHUNK
    has_prefix = prefix is not None
    tiles_per_seq = max(seq // tm, 1)
    keep = CONV_W - 1
    in_specs = [pl.BlockSpec((tm, dm), lambda i: (i, 0))]
    args = [x2d]
    if has_prefix:
        in_specs.append(_const_spec(prefix.shape))
        args.append(prefix)
        tail_shape = (rows // SUBLANES, keep, 2 * d_ff)
        tail_spec = _const_spec(tail_shape)
    else:
        tail_shape = (n_seq, keep, 2 * d_ff)
        tail_spec = pl.BlockSpec((1, keep, 2 * d_ff), lambda i: (i // tiles_per_seq, 0, 0))
    weights = [wup, cw, cb, wdn, lw["ln2_g"], lw["ln2_b"]]
    in_specs += [_const_spec(w) for w in weights]
    scratch = [pltpu.VMEM((tm, d_ff), BF)]
    if has_prefix:
        body = functools.partial(_ffn_sample_body, alpha=lw["alpha"])
    else:
        body = functools.partial(_ffn_prompt_body, tiles_per_seq=tiles_per_seq, alpha=lw["alpha"])
        scratch += [pltpu.VMEM((n_chunks, SUBLANES, cw2), F32)]
    return pl.pallas_call(
        body, grid=(rows // tm,), in_specs=in_specs,
        out_specs=[pl.BlockSpec((tm, dm), lambda i: (i, 0)), tail_spec],
        out_shape=[jax.ShapeDtypeStruct((rows, dm), F32), jax.ShapeDtypeStruct(tail_shape, F32)],
        scratch_shapes=scratch,
        compiler_params=_params("arbitrary"),
        name="ffn_sample" if has_prefix else "ffn_prompt",
    )(*args, *_operands(weights))


def _rope_tables(pos, half):
    inv = ROPE_THETA ** (-jnp.arange(half, dtype=F32) / half)
    ang = pos.astype(F32)[:, None] * inv[None, :]
    cos, sin = jnp.cos(ang), jnp.sin(ang)
    return jnp.concatenate([cos, cos], axis=-1), jnp.concatenate([sin, sin], axis=-1)


def _rotate_half_cols(w):
    half = w.shape[-1] // 2
    return jnp.concatenate([-w[..., half:], w[..., :half]], axis=-1)


def _paired_halves(x, axis):
    even = jnp.zeros_like(x)
    return jnp.where((jnp.arange(x.shape[1]) % 2 == 0).reshape((1, -1) + (1,) * (x.ndim - 2)),
                     jnp.concatenate([x, even], axis=axis), jnp.concatenate([even, x], axis=axis))


def _stacked_weights(p, sizes, tdec, dbatch):
    pw, ql, kl, rp, gw, dm, nope, vh, d_ff = sizes
    depth = p["w_in"].shape[0]
    w_in = p["w_in"]
    o = 0
    cols = {}
    for name, width in (("a", pw), ("cq", ql), ("ckv", kl), ("kr", rp), ("u", gw), ("v", gw), ("g", 3 * dm)):
        cols[name] = w_in[:, :, o:o + width]
        o += width
    kr_block = jnp.concatenate([cols["kr"], _rotate_half_cols(cols["kr"]),
                                jnp.zeros((depth, dm, LANES - 2 * rp), F32)], axis=2)
    sw = {}
    sw["w_in_main"] = jnp.concatenate([cols["a"], cols["cq"], cols["ckv"], cols["u"], cols["v"], kr_block],
                                      axis=2).astype(BF)
    sw["w_gates"] = cols["g"].astype(BF)
    for k in ("q_norm_g", "kv_norm_g", "v_norm_g", "v_norm_b", "pool_scale", "ln1_g", "ln1_b", "ln2_g", "ln2_b",
              "conv_b"):
        sw[k] = p[k].reshape(depth, 1, -1)
    w_uq = p["w_uq"].reshape(depth, ql, N_HEADS, nope + rp)
    q_nope = w_uq[..., :nope].reshape(depth, ql, N_HEADS * nope)
    q_pe = w_uq[..., nope:]
    sw["w_q"] = jnp.concatenate([q_nope, q_pe.reshape(depth, ql, N_HEADS * rp),
                                 _rotate_half_cols(q_pe).reshape(depth, ql, N_HEADS * rp)], axis=2).astype(BF)
    sw["w_q_t"] = jnp.swapaxes(sw["w_q"], 1, 2)
    sw["w_uk_pad"] = _paired_halves(jnp.transpose(p["w_uk"], (0, 2, 3, 1)), axis=2).astype(BF)
    sw["w_uk_pad_t"] = jnp.swapaxes(sw["w_uk_pad"], 2, 3)
    sw["w_uv_pad"] = _paired_halves(jnp.transpose(p["w_uv"], (0, 2, 1, 3)), axis=3).astype(BF)
    sw["w_uv_t"] = jnp.transpose(p["w_uv"], (0, 2, 3, 1)).astype(BF)
    pool_w = p["pool_w"]
    n_pg, pgw = pool_w.shape[1], pool_w.shape[2]
    sw["pool_bd"] = jnp.einsum("gh,dgij->dgihj", jnp.eye(n_pg, dtype=F32),
                               pool_w).reshape(depth, n_pg * pgw, n_pg * pgw).astype(BF)
    for k in ("w_pool_out", "w_mla_out", "w_gmlp_out", "w_out", "w_up", "w_down"):
        sw[k] = p[k].astype(BF)
    sw["conv_w"] = p["conv_w"]
    w_s = p["w_spatial"]
    b_s = p["b_spatial"]
    n_gg = w_s.shape[1]
    gc = gw // n_gg
    sw["ws"] = jnp.tril(w_s).astype(BF)
    sw["bs"] = jnp.repeat(jnp.swapaxes(b_s, 1, 2), gc, axis=2)
    ws_dec = jnp.tril(w_s[:, :, :tdec, :tdec])
    sw["ws_dec"] = jnp.einsum("ab,dgts->dgatbs", jnp.eye(dbatch, dtype=F32),
                              ws_dec).reshape(depth, n_gg, dbatch * tdec, dbatch * tdec).astype(BF)
    sw["bs_dec"] = jnp.tile(jnp.repeat(jnp.swapaxes(b_s[:, :, :tdec], 1, 2), gc, axis=2), (1, dbatch, 1))
    return sw


def _layer_weights(l, sw, sizes, depth):
    pw, ql, kl, rp, gw, dm, nope, vh, d_ff = sizes
    lw = {k: _LayerView(v, l) for k, v in sw.items()}
    lw.update({"dims": (pw, ql, kl, rp, gw), "nope": nope, "scale": float((nope + rp) ** -0.5),
               "alpha": float((2.0 * depth) ** 0.25)})
    return lw


def _inv_count(pos, pw):
    gwidth = pw // len(POOL_WINDOWS)
    win = jnp.repeat(jnp.asarray(POOL_WINDOWS, jnp.int32), gwidth)
    cnt = jnp.minimum(pos[:, None] + 1, win[None, :]).astype(F32)
    return 1.0 / cnt


def kernel(x_prompt, x_sample, cache_ckv, cache_kpe, state_pool, state_ffn, page_table, w_in, pool_w, pool_scale,
           w_pool_out, q_norm_g, w_uq, kv_norm_g, w_uk, w_uv, w_mla_out, v_norm_g, v_norm_b, w_spatial, b_spatial,
           w_gmlp_out, w_out, ln1_g, ln1_b, w_up, conv_w, conv_b, w_down, ln2_g, ln2_b):
    p = dict(w_in=w_in, pool_w=pool_w, pool_scale=pool_scale, w_pool_out=w_pool_out, q_norm_g=q_norm_g, w_uq=w_uq,
             kv_norm_g=kv_norm_g, w_uk=w_uk, w_uv=w_uv, w_mla_out=w_mla_out, v_norm_g=v_norm_g, v_norm_b=v_norm_b,
             w_spatial=w_spatial, b_spatial=b_spatial, w_gmlp_out=w_gmlp_out, w_out=w_out, ln1_g=ln1_g,
             ln1_b=ln1_b, w_up=w_up, conv_w=conv_w, conv_b=conv_b, w_down=w_down, ln2_g=ln2_g, ln2_b=ln2_b)
    batch, seq, dm = x_prompt.shape
    dbatch, tdec, _ = x_sample.shape
    depth = w_in.shape[0]
    ps = cache_ckv.shape[2]
    n_pages = page_table.shape[1]
    past = n_pages * ps
    kl, rp = cache_ckv.shape[3], cache_kpe.shape[3]
    pw = state_pool.shape[3]
    ql = q_norm_g.shape[1]
    gw = v_norm_g.shape[1]
    nope, vh = w_uk.shape[3], w_uv.shape[3]
    d_ff = w_down.shape[1]
    sizes = (pw, ql, kl, rp, gw, dm, nope, vh, d_ff)
    assert tdec == SUBLANES and pw == 2 * LANES and kl % LANES == 0 and N_HEADS * rp == 2 * LANES
    assert nope * 2 == LANES and vh * 2 == LANES and d_ff % FF_CHUNK == 0

    tm = min(512, seq)
    tq = min(512, seq)
    rows_s = dbatch * tdec
    n_grp = next(g for g in (16, 8, 4, 2, 1) if n_pages % (DECODE_SLOTS * g) == 0)
    n_split = min(2, n_grp)
    assert seq % tm == 0 and tq == tm
    cache_kpe_t = jnp.swapaxes(cache_kpe, 2, 3)

    pos_p = jnp.arange(seq, dtype=jnp.int32)
    pos_s = past + jnp.arange(tdec, dtype=jnp.int32)

    def tables(pos, reps):
        c, s = _rope_tables(pos, rp // 2)
        zeros = jnp.zeros((pos.shape[0], LANES - rp), F32)
        ck, sk = jnp.concatenate([c, zeros], 1), jnp.concatenate([s, zeros], 1)
        cq, sq = jnp.tile(c, (1, N_HEADS)), jnp.tile(s, (1, N_HEADS))
        ic = _inv_count(pos, pw)
        return [jnp.tile(t, (reps, 1)) for t in (ck, sk, cq, sq, ic)]

    cos_kp, sin_kp, cos_qp, sin_qp, icnt_p = tables(pos_p, 1)
    cos_qpt, sin_qpt = cos_qp.T, sin_qp.T
    cos_ks, sin_ks, cos_qs, sin_qs, icnt_s = tables(pos_s, dbatch)

    xp = x_prompt.reshape(batch * seq, dm)
    xs = x_sample.reshape(rows_s, dm)
    outs = [[] for _ in range(9)]
    sw = _stacked_weights(p, sizes, tdec, dbatch)
    for l in range(depth):
        lw = _layer_weights(l, sw, sizes, depth)
        a_in, _, cqt, ckvn, kpe, kv, vt, u, vn = _in_proj(xp, lw, cos_kp, sin_kp, tm=tm, vn_dtype=BF)
        yb_pre = _prompt_attention(cqt, kv, vt, cos_qpt, sin_qpt, lw, batch=batch, seq=seq, tq=tq)
        xp = _merge_prompt(xp, a_in, icnt_p, u, vn, yb_pre, lw, tm=tm, seq=seq)
        xp, tail_p = _ffn(xp, None, lw, tm=tm, seq=seq, n_seq=batch)
        outs[0].append(ckvn.reshape(batch * seq // ps, ps, kl))
        outs[1].append(kpe.reshape(batch * seq // ps, ps, rp))
        outs[4].append(a_in.reshape(batch, seq, pw)[:, seq - POOL_KEEP:])
        outs[6].append(tail_p)
        a_s, cqn_s, _, ckvn_s, kpe_s, _, _, u_s, vn_s = _in_proj(xs, lw, cos_ks, sin_ks, tm=rows_s, vn_dtype=F32)
        qlat_s, qpe_s = _q_proj(cqn_s, lw, cos_qs, sin_qs)
        yb_s = _sample_attention(qlat_s, qpe_s, ckvn_s, kpe_s, cache_ckv, cache_kpe_t, page_table, l, lw,
                                 dbatch=dbatch, tdec=tdec, n_grp=n_grp, n_split=n_split)
        pool_ext = jnp.concatenate([jnp.zeros((dbatch, 1, pw), F32), state_pool[l],
                                    a_s.reshape(dbatch, tdec, pw)], axis=1)
        xs = _merge_sample(xs, pool_ext, icnt_s, u_s, vn_s.astype(BF), yb_s, lw, tdec=tdec)
        prefix = jnp.pad(state_ffn[l], ((0, 0), (SUBLANES - (CONV_W - 1), 0), (0, 0))).reshape(rows_s, 2 * d_ff)
        xs, tail_s = _ffn(xs, prefix, lw, tm=rows_s, seq=tdec, n_seq=dbatch)
        outs[2].append(ckvn_s.reshape(dbatch, tdec, kl))
        outs[3].append(kpe_s.reshape(dbatch, tdec, rp))
        outs[5].append(pool_ext[:, -POOL_KEEP:])
        outs[7].append(tail_s)
        outs[8].append(vn_s.reshape(dbatch, tdec, gw))
    stacked = [jnp.stack(o) for o in outs]
    new_ckv_p, new_kpe_p, new_ckv_s, new_kpe_s, pool_p, pool_s, ffn_p, ffn_s, gv_s = stacked
    return (xp.reshape(batch, seq, dm), xs.reshape(dbatch, tdec, dm), new_ckv_p, new_kpe_p, new_ckv_s, new_kpe_s,
            pool_p, pool_s, ffn_p, ffn_s, gv_s)
```

```python
import functools

import jax
import jax.numpy as jnp
import numpy as np
from jax import lax
from jax.experimental import pallas as pl
from jax.experimental.pallas import tpu as pltpu

BF = jnp.bfloat16
F32 = jnp.float32

POOL_WINDOWS = (2, 4, 8, 16)
POOL_KEEP = max(POOL_WINDOWS) - 1
N_HEADS = 8
ROPE_THETA = 10000.0
RMS_EPS = 1e-6
LN_EPS = 1e-5
CONV_W = 3
SUBLANES = 8
LANES = 128
FF_CHUNK = 256
NEG_BIG = -1e30
SCORE_LOOKAHEAD = 8
DECODE_SLOTS = 4
DECODE_AHEAD = 3
VMEM_LIMIT = 56 * 1024 * 1024


def _params(*sem):
    return pltpu.CompilerParams(dimension_semantics=sem, vmem_limit_bytes=VMEM_LIMIT)


class _LayerView:
    def __init__(self, stacked, layer):
        self.stacked, self.layer = stacked, layer

    @property
    def shape(self):
        return self.stacked.shape[1:]


def _const_spec(x):
    if isinstance(x, _LayerView):
        nd, layer = len(x.shape), x.layer
        return pl.BlockSpec((None,) + tuple(x.shape), lambda *_: (layer,) + (0,) * nd,
                            pipeline_mode=pl.Buffered(1))
    shape = tuple(getattr(x, "shape", x))
    nd = len(shape)
    return pl.BlockSpec(shape, lambda *_: (0,) * nd, pipeline_mode=pl.Buffered(1))


def _operands(args):
    return [a.stacked if isinstance(a, _LayerView) else a for a in args]


def _rmsnorm(x, g):
    return x * lax.rsqrt(jnp.mean(x * x, axis=-1, keepdims=True) + RMS_EPS) * g


def _layernorm(x, g, b):
    mu = jnp.mean(x, axis=-1, keepdims=True)
    xc = x - mu
    var = jnp.mean(xc * xc, axis=-1, keepdims=True)
    return xc * lax.rsqrt(var + LN_EPS) * g + b


def _dot(a, b):
    return jnp.dot(a, b, preferred_element_type=F32)


def _dot_nt(a, b):
    return lax.dot_general(a, b, (((1,), (1,)), ((), ())), preferred_element_type=F32)


def _shift_rows(y3, prev3, k):
    t = lax.broadcasted_iota(jnp.int32, y3.shape, 1)
    return jnp.where(t >= k, pltpu.roll(y3, k, 1), pltpu.roll(prev3, k, 1))


def _prev_groups(y3, first):
    if y3.shape[0] == 1:
        return first
    return jnp.concatenate([first, y3[:-1]], axis=0)


def _in_proj_body(x_ref, w_ref, qg_ref, kvg_ref, vg_ref, vb_ref, cos_ref, sin_ref,
                  a_ref, cq_ref, cqt_ref, ckv_ref, kpe_ref, kv_ref, vt_ref, u_ref, vn_ref, *, dims):
    pw, ql, kl, rp, gw = dims
    tm = x_ref.shape[0]
    n_part = 2 if tm % (2 * LANES) == 0 else 1
    part = tm // n_part
    zs = [_dot(x_ref[i * part:(i + 1) * part, :].astype(BF), w_ref[...]) for i in range(n_part)]
    for i, z in enumerate(zs):
        rows = slice(i * part, (i + 1) * part)
        o = 0
        a_ref[rows, :] = z[:, o:o + pw]
        o += pw
        cqn = _rmsnorm(z[:, o:o + ql], qg_ref[...])
        cq_ref[rows, :] = cqn.astype(BF)
        cqt_ref[0, :, rows] = cqn.T.astype(BF)
        o += ql
        ckvn = _rmsnorm(z[:, o:o + kl], kvg_ref[...])
        ckv_ref[rows, :] = ckvn
        o += kl
        u_ref[rows, :] = z[:, o:o + gw].astype(BF)
        o += gw
        vn_ref[rows, :] = _layernorm(z[:, o:o + gw], vg_ref[...], vb_ref[...]).astype(vn_ref.dtype)
        o += gw
        kr = z[:, o:o + LANES]
        kpe = kr * cos_ref[rows, :] + pltpu.roll(kr, LANES - rp, 1) * sin_ref[rows, :]
        kpe_ref[rows, :] = kpe[:, :rp]
        tiled = kpe
        for j in range(1, LANES // rp):
            tiled = tiled + pltpu.roll(kpe, j * rp, 1)
        tb = tiled.astype(BF)
        kv_ref[rows, 0:kl] = ckvn.astype(BF)
        vt_ref[0, :, rows] = ckvn.T.astype(BF)
        for j in range(N_HEADS * rp // LANES):
            kv_ref[rows, kl + j * LANES:kl + (j + 1) * LANES] = tb


def _in_proj(x2d, lw, cos_k, sin_k, *, tm, vn_dtype):
    rows, dm = x2d.shape
    dims = lw["dims"]
    pw, ql, kl, rp, gw = dims
    ntab = cos_k.shape[0] // tm
    weights = [lw["w_in_main"], lw["q_norm_g"], lw["kv_norm_g"], lw["v_norm_g"], lw["v_norm_b"]]
    row_spec = lambda w: pl.BlockSpec((tm, w), lambda i: (i, 0))
    tab_spec = pl.BlockSpec((tm, LANES), lambda i: (i % ntab, 0))
    return pl.pallas_call(
        functools.partial(_in_proj_body, dims=dims),
        grid=(rows // tm,),
        in_specs=[row_spec(dm)] + [_const_spec(w) for w in weights] + [tab_spec, tab_spec],
        out_specs=[row_spec(pw), row_spec(ql), pl.BlockSpec((1, ql, tm), lambda i: (i, 0, 0)), row_spec(kl),
                   row_spec(rp), row_spec(kl + N_HEADS * rp),
                   pl.BlockSpec((1, kl, tm), lambda i: (i, 0, 0)), row_spec(gw), row_spec(gw)],
        out_shape=[jax.ShapeDtypeStruct((rows, pw), F32), jax.ShapeDtypeStruct((rows, ql), BF),
                   jax.ShapeDtypeStruct((rows // tm, ql, tm), BF),
                   jax.ShapeDtypeStruct((rows, kl), F32), jax.ShapeDtypeStruct((rows, rp), F32),
                   jax.ShapeDtypeStruct((rows, kl + N_HEADS * rp), BF), jax.ShapeDtypeStruct((rows // tm, kl, tm), BF),
                   jax.ShapeDtypeStruct((rows, gw), BF), jax.ShapeDtypeStruct((rows, gw), vn_dtype)],
        compiler_params=_params("arbitrary"),
        name="in_proj",
    )(x2d, *_operands(weights), cos_k, sin_k)


def _q_heads(cq, w_ref, wuk_ref, cos, sin, nope, rp, scale):
    hn = N_HEADS * nope
    hr = N_HEADS * rp
    q = _dot(cq, w_ref[...])
    qpe = (q[:, hn:hn + hr] * cos + q[:, hn + hr:hn + 2 * hr] * sin) * scale
    qlat = [_dot(q[:, (h // 2) * LANES:(h // 2 + 1) * LANES].astype(BF), wuk_ref[h]) * scale
            for h in range(N_HEADS)]
    return qlat, qpe


def _q_body(cq_ref, w_ref, wuk_ref, cos_ref, sin_ref, qlat_ref, qpe_ref, *, nope, rp, scale):
    qlat, qpe = _q_heads(cq_ref[...], w_ref, wuk_ref, cos_ref[...], sin_ref[...], nope, rp, scale)
    qpe_ref[...] = qpe
    for h in range(N_HEADS):
        qlat_ref[h] = qlat[h]


def _q_proj(cq, lw, cos_q, sin_q):
    rows, ql = cq.shape
    pw, _, kl, rp, gw = lw["dims"]
    hr = N_HEADS * rp
    args = [cq, lw["w_q"], lw["w_uk_pad"], cos_q, sin_q]
    return pl.pallas_call(
        functools.partial(_q_body, nope=lw["nope"], rp=rp, scale=lw["scale"]),
        grid=(1,),
        in_specs=[_const_spec(a) for a in args],
        out_specs=[_const_spec((N_HEADS, rows, kl)), _const_spec((rows, hr))],
        out_shape=[jax.ShapeDtypeStruct((N_HEADS, rows, kl), F32), jax.ShapeDtypeStruct((rows, hr), F32)],
        compiler_params=_params("arbitrary"),
        name="q_proj",
    )(*_operands(args))


def _heads_out(o, wuv_ref, rows_per_head, full_m):
    outs = []
    ob = o.astype(BF)
    for j in range(N_HEADS // 2):
        acc = None
        for h in (2 * j, 2 * j + 1):
            sl = slice(h * rows_per_head, (h + 1) * rows_per_head)
            if full_m:
                y = _dot(ob, wuv_ref[h])[sl]
            else:
                y = _dot(ob[sl], wuv_ref[h])
            acc = y if acc is None else acc + y
        outs.append(acc)
    return outs


def _q_heads_t(cqt, wqt_ref, wukt_ref, cos_t, sin_t, nope, rp, scale):
    hn = N_HEADS * nope
    hr = N_HEADS * rp
    qt = _dot(wqt_ref[...], cqt)
    qpe_t = (qt[hn:hn + hr] * cos_t + qt[hn + hr:hn + 2 * hr] * sin_t) * scale
    qlat_t = [_dot(wukt_ref[h], qt[(h // 2) * LANES:(h // 2 + 1) * LANES].astype(BF)) * scale
              for h in range(N_HEADS)]
    return qlat_t, qpe_t


def _attn_body(cqt_ref, cos_ref, sin_ref, wqt_ref, wukt_ref, kv_ref, vt_ref, wuvt_ref, o_ref,
               q_ref, m_ref, l_ref, acc_ref, *, tq, kl, nope, rp, scale):
    qi = pl.program_id(1)
    m_ref[...] = jnp.full(m_ref.shape, NEG_BIG, F32)
    l_ref[...] = jnp.zeros(l_ref.shape, F32)
    acc_ref[...] = jnp.zeros(acc_ref.shape, F32)
    qlat_t, qpe_t = _q_heads_t(cqt_ref[0], wqt_ref, wukt_ref, cos_ref[...], sin_ref[...], nope, rp, scale)
    head_of_row = lax.broadcasted_iota(jnp.int32, qpe_t.shape, 0) // rp
    for h in range(N_HEADS):
        q_ref[h, 0:kl, :] = qlat_t[h].astype(BF)
        q_ref[h, kl:, :] = jnp.where(head_of_row == h, qpe_t, 0.0).astype(BF)

    def step(masked, ki):
        half = tq // 2
        parts = [(0, half, half), (half, tq, tq)] if masked and half % LANES == 0 else [(0, tq, tq)]

        def scores(h):
            out = []
            for q0, q1, nk in parts:
                s = _dot(kv_ref[ki, 0:nk, :], q_ref[h, :, q0:q1])
                if masked:
                    kpos = lax.broadcasted_iota(jnp.int32, s.shape, 0)
                    qpos = lax.broadcasted_iota(jnp.int32, s.shape, 1) + q0
                    s = jnp.where(kpos <= qpos, s, NEG_BIG)
                out.append(s)
            return out

        pending = [scores(h) for h in range(SCORE_LOOKAHEAD)]
        for h in range(N_HEADS):
            s_parts = pending.pop(0)
            if h + SCORE_LOOKAHEAD < N_HEADS:
                pending.append(scores(h + SCORE_LOOKAHEAD))
            for (q0, q1, nk), s in zip(parts, s_parts):
                m_prev = m_ref[h, :, q0:q1]
                m_new = jnp.maximum(m_prev, jnp.max(s, axis=0, keepdims=True))
                alpha = jnp.exp(m_prev - m_new)
                p = jnp.exp(s - m_new)
                l_ref[h, :, q0:q1] = alpha * l_ref[h, :, q0:q1] + jnp.sum(p, axis=0, keepdims=True)
                acc_ref[h, :, q0:q1] = alpha * acc_ref[h, :, q0:q1] + _dot(vt_ref[ki, :, 0:nk], p.astype(BF))
                m_ref[h, :, q0:q1] = m_new

    def unmasked(ki, carry):
        step(False, ki)
        return carry

    step(True, qi)
    lax.fori_loop(0, qi, unmasked, 0)
    outs = []
    for h in range(N_HEADS):
        o = (acc_ref[h] * (1.0 / l_ref[h])).astype(BF)
        outs.append(_dot(wuvt_ref[h], o))
    o_ref[...] = jnp.concatenate(outs, axis=0).T.astype(o_ref.dtype)


def _prompt_attention(cqt, kv, vt, cos_qt, sin_qt, lw, *, batch, seq, tq):
    _, ql, kl, rp, _ = lw["dims"]
    width = kv.shape[-1]
    hr = N_HEADS * rp
    nq = seq // tq
    n_h, vh, _ = lw["w_uv_t"].shape
    assert vt.shape == (batch * nq, kl, tq) and cqt.shape == (batch * nq, ql, tq)
    tab_spec = pl.BlockSpec((hr, tq), lambda b, i: (0, i))
    return pl.pallas_call(
        functools.partial(_attn_body, tq=tq, kl=kl, nope=lw["nope"], rp=rp, scale=lw["scale"]),
        grid=(batch, nq),
        in_specs=[pl.BlockSpec((1, ql, tq), lambda b, i: (b * nq + i, 0, 0)), tab_spec, tab_spec,
                  _const_spec(lw["w_q_t"]), _const_spec(lw["w_uk_pad_t"]),
                  pl.BlockSpec((nq, tq, width), lambda b, i: (b, 0, 0)),
                  pl.BlockSpec((nq, kl, tq), lambda b, i: (b, 0, 0)),
                  _const_spec(lw["w_uv_t"])],
        out_specs=pl.BlockSpec((tq, n_h * vh), lambda b, i: (b * nq + i, 0)),
        out_shape=jax.ShapeDtypeStruct((batch * seq, n_h * vh), BF),
        scratch_shapes=[pltpu.VMEM((N_HEADS, width, tq), BF), pltpu.VMEM((N_HEADS, 1, tq), F32),
                        pltpu.VMEM((N_HEADS, 1, tq), F32), pltpu.VMEM((N_HEADS, kl, tq), F32)],
        compiler_params=_params("arbitrary", "arbitrary"),
        name="prompt_attention",
    )(*_operands([cqt, cos_qt, sin_qt, lw["w_q_t"], lw["w_uk_pad_t"], kv.reshape(batch * nq, tq, width), vt,
                  lw["w_uv_t"]]))


def _decode_body(pt_ref, qlat_ref, qpe_ref, ckvn_ref, kpen_ref, wuv_ref, ckv_hbm, kpe_hbm, o_ref,
                 kbuf, pbuf, sem_k, sem_p, *, layer, n_grp, n_groups, n_split, tdec, rp, kl):
    b = pl.program_id(0)
    rows = N_HEADS * tdec

    def page_copies(sample, group, slot, table=True):
        copies = []
        for j in range(n_grp):
            pid = pt_ref[sample, group * n_grp + j] if table else 0
            copies.append(pltpu.make_async_copy(ckv_hbm.at[layer, pid], kbuf.at[slot, j], sem_k.at[slot]))
            copies.append(pltpu.make_async_copy(kpe_hbm.at[layer, pid], pbuf.at[slot, j], sem_p.at[slot]))
        return copies

    def start_ahead(g):
        nxt = g + DECODE_AHEAD
        slot = nxt % DECODE_SLOTS
        if nxt < n_groups:
            for c in page_copies(b, nxt, slot):
                c.start()
        else:
            @pl.when(b + 1 < pl.num_programs(0))
            def _():
                for c in page_copies(b + 1, nxt - n_groups, slot):
                    c.start()

    @pl.when(b == 0)
    def _():
        for g in range(DECODE_AHEAD):
            for c in page_copies(0, g, g % DECODE_SLOTS):
                c.start()

    q = qlat_ref[...].reshape(rows, kl).astype(BF)
    qpe = qpe_ref[...]
    qp = jnp.concatenate([qpe[:, h * rp:(h + 1) * rp] for h in range(N_HEADS)], axis=0).astype(BF)

    def update(state, s, keys):
        m_prev, l_prev, acc = state
        m_new = jnp.maximum(m_prev, jnp.max(s, axis=-1, keepdims=True))
        alpha = jnp.exp(m_prev - m_new)
        p = jnp.exp(s - m_new)
        l_new = alpha * l_prev + jnp.sum(p, axis=-1, keepdims=True)
        acc = alpha * acc
        off = 0
        pb = p.astype(BF)
        for k in keys:
            acc = acc + _dot(pb[:, off:off + k.shape[0]], k)
            off += k.shape[0]
        return m_new, l_new, acc

    states = [(jnp.full((rows, 1), NEG_BIG, F32), jnp.zeros((rows, 1), F32), jnp.zeros((rows, kl), F32))
              for _ in range(n_split)]
    per = n_grp // n_split
    for g in range(n_groups):
        slot = g % DECODE_SLOTS
        start_ahead(g)
        for c in page_copies(b, g, slot, table=False):
            c.wait()
        keys = [kbuf[slot, j].astype(BF) for j in range(n_grp)]
        scores = [_dot_nt(q, k) + _dot(qp, pbuf[slot, j].astype(BF)) for j, k in enumerate(keys)]
        for i in range(n_split):
            states[i] = update(states[i], jnp.concatenate(scores[i * per:(i + 1) * per], axis=1),
                               keys[i * per:(i + 1) * per])

    pad = 2 * tdec
    kn = jnp.concatenate([ckvn_ref[...], jnp.zeros((pad - tdec, kl), F32)], axis=0).astype(BF)
    kpn = jnp.concatenate([kpen_ref[...], jnp.zeros((pad - tdec, rp), F32)], axis=0).astype(BF)
    sn = _dot_nt(q, kn) + _dot_nt(qp, kpn)
    qpos = lax.broadcasted_iota(jnp.int32, (N_HEADS, tdec, pad), 1).reshape(rows, pad)
    kpos = lax.broadcasted_iota(jnp.int32, (rows, pad), 1)
    states[0] = update(states[0], jnp.where(kpos <= qpos, sn, NEG_BIG), [kn])
    m_all = states[0][0]
    for m_i, _, _ in states[1:]:
        m_all = jnp.maximum(m_all, m_i)
    l_all = jnp.zeros_like(m_all)
    o = jnp.zeros((rows, kl), F32)
    for m_i, l_i, acc_i in states:
        w = jnp.exp(m_i - m_all)
        l_all = l_all + w * l_i
        o = o + w * acc_i
    o = o * (1.0 / l_all)
    for j, y in enumerate(_heads_out(o, wuv_ref, tdec, full_m=True)):
        o_ref[:, j * LANES:(j + 1) * LANES] = y.astype(o_ref.dtype)


def _sample_attention(qlat, qpe, ckvn, kpen, cache_ckv, cache_kpe_t, page_table, layer, lw, *, dbatch, tdec, n_grp,
                      n_split):
    _, _, kl, rp, _ = lw["dims"]
    n_pages = page_table.shape[1]
    ps = cache_ckv.shape[2]
    hv = lw["w_uv_pad"].shape[2] * N_HEADS // 2
    hr = N_HEADS * rp
    rows = N_HEADS * tdec

    n_groups = n_pages // n_grp
    assert n_groups % DECODE_SLOTS == 0
    in_specs = [pl.BlockSpec((N_HEADS, tdec, kl), lambda b, pt: (0, b, 0)),
                pl.BlockSpec((tdec, hr), lambda b, pt: (b, 0)),
                pl.BlockSpec((tdec, kl), lambda b, pt: (b, 0)),
                pl.BlockSpec((tdec, rp), lambda b, pt: (b, 0)),
                _const_spec(lw["w_uv_pad"]),
                pl.BlockSpec(memory_space=pl.ANY), pl.BlockSpec(memory_space=pl.ANY)]
    grid_spec = pltpu.PrefetchScalarGridSpec(
        num_scalar_prefetch=1, grid=(dbatch,), in_specs=in_specs,
        out_specs=pl.BlockSpec((tdec, hv), lambda b, pt: (b, 0)),
        scratch_shapes=[pltpu.VMEM((DECODE_SLOTS, n_grp, ps, kl), F32), pltpu.VMEM((DECODE_SLOTS, n_grp, rp, ps), F32),
                        pltpu.SemaphoreType.DMA((DECODE_SLOTS,)), pltpu.SemaphoreType.DMA((DECODE_SLOTS,))])
    return pl.pallas_call(
        functools.partial(_decode_body, layer=layer, n_grp=n_grp, n_groups=n_groups, n_split=n_split, tdec=tdec,
                          rp=rp, kl=kl),
        grid_spec=grid_spec,
        out_shape=jax.ShapeDtypeStruct((dbatch * tdec, hv), F32),
        compiler_params=_params("arbitrary"),
        name="sample_attention",
    )(*_operands([page_table, qlat, qpe, ckvn, kpen, lw["w_uv_pad"], cache_ckv, cache_kpe_t]))


def _window_select(sums, shape):
    gw = shape[-1] // len(POOL_WINDOWS)
    grp = lax.broadcasted_iota(jnp.int32, shape, len(shape) - 1) // gw
    out = sums[-1]
    for gi in range(len(POOL_WINDOWS) - 2, -1, -1):
        out = jnp.where(grp == gi, sums[gi], out)
    return out


def _spatial_matmul(vn, ws_ref, bs_ref):
    n_g, clen, _ = ws_ref.shape
    gc = vn.shape[1] // n_g
    grp = lax.broadcasted_iota(jnp.int32, (clen, vn.shape[1]), 1) // gc
    parts = []
    for c in range(vn.shape[0] // clen):
        vc = vn[c * clen:(c + 1) * clen]
        s = _dot(ws_ref[n_g - 1], vc)
        for g in range(n_g - 2, -1, -1):
            s = jnp.where(grp == g, _dot(ws_ref[g], vc), s)
        parts.append(s + bs_ref[...])
    return parts[0] if len(parts) == 1 else jnp.concatenate(parts, axis=0)


def _merge_tail(x, d, u, s, yb_pre, wg_ref, pbd_ref, psc_ref, wpo_ref, wgo_ref, wmo_ref,
                wo_ref, lng_ref, lnb_ref, alpha):
    dm = x.shape[1]
    xb = x.astype(BF)
    ya = _dot(d.astype(BF), pbd_ref[...]) * psc_ref[...]
    ya = _dot(ya.astype(BF), wpo_ref[...])
    m = jax.nn.sigmoid(_dot(xb, wg_ref[:, 0:dm])) * ya
    yb = _dot(yb_pre.astype(BF), wmo_ref[...])
    m = m + jax.nn.sigmoid(_dot(xb, wg_ref[:, dm:2 * dm])) * yb
    yc = _dot((u.astype(F32) * s).astype(BF), wgo_ref[...])
    m = m + jax.nn.sigmoid(_dot(xb, wg_ref[:, 2 * dm:3 * dm])) * yc
    y = alpha * x + _dot(m.astype(BF), wo_ref[...])
    return _layernorm(y, lng_ref[...], lnb_ref[...])


def _merge_prompt_body(x_ref, a_ref, aprev_ref, icnt_ref, u_ref, vn_ref, yb_ref, *rest, tiles_per_seq, n_parts,
                       alpha):
    ws_ref, bs_ref = rest[:2]
    w_refs, o_ref = rest[2:-1], rest[-1]
    i = pl.program_id(0)
    a = a_ref[...]
    tm, pw = a.shape
    hist = jnp.where(i % tiles_per_seq == 0, 0.0, aprev_ref[...])
    n_hist = hist.shape[0] // SUBLANES
    ext = jnp.concatenate([hist, a], axis=0).reshape(tm // SUBLANES + n_hist, SUBLANES, pw)
    zero = jnp.zeros((1, SUBLANES, pw), F32)
    s2 = ext + _shift_rows(ext, _prev_groups(ext, zero), 1)
    s4 = s2 + _shift_rows(s2, _prev_groups(s2, zero), 2)
    s8 = s4 + _shift_rows(s4, _prev_groups(s4, zero), 4)
    s16 = s8 + _prev_groups(s8, zero)
    sel = _window_select([s[n_hist:] for s in (s2, s4, s8, s16)], (tm // SUBLANES, SUBLANES, pw))
    d = sel.reshape(tm, pw) * icnt_ref[...] - a
    part = tm // n_parts
    for j in range(n_parts):
        rows = slice(j * part, (j + 1) * part)
        o_ref[rows, :] = _merge_tail(x_ref[rows, :], d[rows], u_ref[rows, :],
                                     _spatial_matmul(vn_ref[rows, :], ws_ref, bs_ref), yb_ref[rows, :],
                                     *w_refs, alpha)


def _merge_sample_body(x_ref, ext_ref, icnt_ref, u_ref, vn_ref, yb_ref, coef_ref, bs_ref, *rest, tdec, alpha):
    w_refs, o_ref = rest[:-1], rest[-1]
    nb, ext_len, pw = ext_ref.shape
    acc = None
    sums = []
    for j in range(max(POOL_WINDOWS)):
        cur = ext_ref[:, ext_len - tdec - j:ext_len - j, :]
        acc = cur if acc is None else acc + cur
        if j + 1 in POOL_WINDOWS:
            sums.append(acc)
    tok = ext_ref[:, ext_len - tdec:ext_len, :]
    sel = _window_select(sums, (nb, tdec, pw))
    d = sel.reshape(nb * tdec, pw) * icnt_ref[...] - tok.reshape(nb * tdec, pw)
    vn3 = vn_ref[...].reshape(nb, tdec, vn_ref.shape[1])
    s = coef_ref[0] * vn3 + bs_ref[...]
    for k in range(1, tdec):
        s = s + coef_ref[k] * pltpu.roll(vn3, k, 1)
    o_ref[...] = _merge_tail(x_ref[...], d, u_ref[...], s.reshape(nb * tdec, vn_ref.shape[1]), yb_ref[...],
                             *w_refs, alpha)


def _merge_weights(lw):
    return [lw["w_gates"], lw["pool_bd"], lw["pool_scale"], lw["w_pool_out"],
            lw["w_gmlp_out"], lw["w_mla_out"], lw["w_out"], lw["ln1_g"], lw["ln1_b"]]


def _merge_prompt(x2d, a_in, icnt, u, vn, yb_pre, lw, *, tm, seq, n_parts):
    rows, dm = x2d.shape
    pw, _, _, _, gw = lw["dims"]
    tiles_per_seq = seq // tm
    hist_rows = 2 * SUBLANES
    hist_per_tile = tm // hist_rows
    weights = [lw["ws"], lw["bs"]] + _merge_weights(lw)
    row_spec = lambda w: pl.BlockSpec((tm, w), lambda i: (i, 0))
    in_specs = [row_spec(dm), row_spec(pw),
                pl.BlockSpec((hist_rows, pw), lambda i: (jnp.maximum(i * hist_per_tile - 1, 0), 0)),
                pl.BlockSpec((tm, pw), lambda i: (i % tiles_per_seq, 0)),
                row_spec(gw), row_spec(gw), row_spec(yb_pre.shape[1])]
    in_specs += [_const_spec(w) for w in weights]
    return pl.pallas_call(
        functools.partial(_merge_prompt_body, tiles_per_seq=tiles_per_seq, n_parts=n_parts, alpha=lw["alpha"]),
        grid=(rows // tm,), in_specs=in_specs, out_specs=row_spec(dm),
        out_shape=jax.ShapeDtypeStruct((rows, dm), F32),
        compiler_params=_params("arbitrary"),
        name="merge_prompt",
    )(x2d, a_in, a_in, icnt, u, vn, yb_pre, *_operands(weights))


def _merge_sample(x2d, ext, icnt, u, vn, yb_pre, lw, *, tdec):
    rows, dm = x2d.shape
    weights = [lw["ws_coef"], lw["bs_dec"]] + _merge_weights(lw)
    args = [x2d, ext, icnt, u, vn, yb_pre] + weights
    return pl.pallas_call(
        functools.partial(_merge_sample_body, tdec=tdec, alpha=lw["alpha"]),
        grid=(1,), in_specs=[_const_spec(a) for a in args], out_specs=_const_spec((rows, dm)),
        out_shape=jax.ShapeDtypeStruct((rows, dm), F32),
        compiler_params=_params("arbitrary"),
        name="merge_sample",
    )(*_operands(args))


def _conv_gate(cur, back1, back2, w, b):
    half = cur.shape[-1] // 2
    conv = (b + w[0:1] * back2 + w[1:2] * back1 + w[2:3] * cur).reshape(cur.shape[0] * SUBLANES, 2 * half)
    return jax.nn.silu(conv[:, :half]) * conv[:, half:]


def _shift_rows_ext(ext, k):
    r = pltpu.roll(ext, k, 1)
    t = lax.broadcasted_iota(jnp.int32, r[1:].shape, 1)
    return jnp.where(t >= k, r[1:], r[:-1])


def _ff_cols(ref_or_val, c, d_ff):
    lo = c * FF_CHUNK
    return jnp.concatenate([ref_or_val[:, lo:lo + FF_CHUNK], ref_or_val[:, d_ff + lo:d_ff + lo + FF_CHUNK]],
                           axis=-1)


def _ff_store_tail(tail, c, d_ff, idx, val):
    lo = c * FF_CHUNK
    tail[idx + (slice(lo, lo + FF_CHUNK),)] = val[..., :FF_CHUNK]
    tail[idx + (slice(d_ff + lo, d_ff + lo + FF_CHUNK),)] = val[..., FF_CHUNK:]


def _ffn_sample_body(x_ref, prefix_ref, wup_ref, cw_ref, cb_ref, wdn_ref, lng_ref, lnb_ref, o_ref, tail_ref,
                     h_ref, *, alpha):
    d_ff = wdn_ref.shape[0]
    cw2 = 2 * FF_CHUNK
    tm = x_ref.shape[0]
    grp = tm // SUBLANES
    x = x_ref[...]
    xb = x.astype(BF)
    for c in range(d_ff // FF_CHUNK):
        a3 = _dot(xb, _ff_cols(wup_ref, c, d_ff)).reshape(grp, SUBLANES, cw2)
        prev = _ff_cols(prefix_ref, c, d_ff).reshape(grp, SUBLANES, cw2)
        h = _conv_gate(a3, _shift_rows(a3, prev, 1), _shift_rows(a3, prev, 2), _ff_cols(cw_ref, c, d_ff),
                       _ff_cols(cb_ref, c, d_ff))
        h_ref[:, c * FF_CHUNK:(c + 1) * FF_CHUNK] = h.astype(BF)
        _ff_store_tail(tail_ref, c, d_ff, (slice(None), slice(None)), a3[:, SUBLANES - (CONV_W - 1):, :])
    o_ref[...] = _layernorm(alpha * x + _dot(h_ref[...], wdn_ref[...]), lng_ref[...], lnb_ref[...])


def _ffn_prompt_body(x_ref, wup_ref, cw_ref, cb_ref, wdn_ref, lng_ref, lnb_ref, o_ref, tail_ref,
                     h_ref, carry_ref, *, tiles_per_seq, n_parts, alpha):
    d_ff = wdn_ref.shape[0]
    n_chunks = d_ff // FF_CHUNK
    cw2 = 2 * FF_CHUNK
    part = x_ref.shape[0] // n_parts

    @pl.when(pl.program_id(0) % tiles_per_seq == 0)
    def _():
        carry_ref[...] = jnp.zeros(carry_ref.shape, F32)

    for j in range(n_parts):
        rows = slice(j * part, (j + 1) * part)
        x = x_ref[rows, :]
        xb = x.astype(BF)

        def up(c):
            return _dot(xb, _ff_cols(wup_ref, c, d_ff))

        a_next = up(0)
        for c in range(n_chunks):
            a = a_next
            if c + 1 < n_chunks:
                a_next = up(c + 1)
            ext = jnp.concatenate([carry_ref[c], a], axis=0).reshape(part // SUBLANES + 1, SUBLANES, cw2)
            h = _conv_gate(ext[1:], _shift_rows_ext(ext, 1), _shift_rows_ext(ext, 2), _ff_cols(cw_ref, c, d_ff),
                           _ff_cols(cb_ref, c, d_ff))
            h_ref[j, :, c * FF_CHUNK:(c + 1) * FF_CHUNK] = h.astype(BF)
            last = a[part - SUBLANES:]
            carry_ref[c] = last
            if j == n_parts - 1:
                _ff_store_tail(tail_ref, c, d_ff, (0, slice(None)), last[SUBLANES - (CONV_W - 1):, :])
        o_ref[rows, :] = _layernorm(alpha * x + _dot(h_ref[j], wdn_ref[...]), lng_ref[...], lnb_ref[...])


def _ffn(x2d, prefix, lw, *, tm, seq, n_seq, n_parts=1):
    rows, dm = x2d.shape
    wup, cw, cb, wdn = lw["w_up"], lw["conv_w"], lw["conv_b"], lw["w_down"]
    d_ff = wdn.shape[0]
    n_chunks = d_ff // FF_CHUNK
    cw2 = 2 * FF_CHUNK
    has_prefix = prefix is not None
    tiles_per_seq = max(seq // tm, 1)
    keep = CONV_W - 1
    in_specs = [pl.BlockSpec((tm, dm), lambda i: (i, 0))]
    args = [x2d]
    if has_prefix:
        in_specs.append(_const_spec(prefix.shape))
        args.append(prefix)
        tail_shape = (rows // SUBLANES, keep, 2 * d_ff)
        tail_spec = _const_spec(tail_shape)
    else:
        tail_shape = (n_seq, keep, 2 * d_ff)
        tail_spec = pl.BlockSpec((1, keep, 2 * d_ff), lambda i: (i // tiles_per_seq, 0, 0))
    weights = [wup, cw, cb, wdn, lw["ln2_g"], lw["ln2_b"]]
    in_specs += [_const_spec(w) for w in weights]
    if has_prefix:
        body = functools.partial(_ffn_sample_body, alpha=lw["alpha"])
        scratch = [pltpu.VMEM((tm, d_ff), BF)]
    else:
        body = functools.partial(_ffn_prompt_body, tiles_per_seq=tiles_per_seq, n_parts=n_parts, alpha=lw["alpha"])
        scratch = [pltpu.VMEM((n_parts, tm // n_parts, d_ff), BF), pltpu.VMEM((n_chunks, SUBLANES, cw2), F32)]
    return pl.pallas_call(
        body, grid=(rows // tm,), in_specs=in_specs,
        out_specs=[pl.BlockSpec((tm, dm), lambda i: (i, 0)), tail_spec],
        out_shape=[jax.ShapeDtypeStruct((rows, dm), F32), jax.ShapeDtypeStruct(tail_shape, F32)],
        scratch_shapes=scratch,
        compiler_params=_params("arbitrary"),
        name="ffn_sample" if has_prefix else "ffn_prompt",
    )(*args, *_operands(weights))


def _rope_tables(pos, half):
    inv = ROPE_THETA ** (-jnp.arange(half, dtype=F32) / half)
    ang = pos.astype(F32)[:, None] * inv[None, :]
    cos, sin = jnp.cos(ang), jnp.sin(ang)
    return jnp.concatenate([cos, cos], axis=-1), jnp.concatenate([sin, sin], axis=-1)


def _rotate_half_cols(w):
    half = w.shape[-1] // 2
    return jnp.concatenate([-w[..., half:], w[..., :half]], axis=-1)


def _paired_halves(x, axis):
    even = jnp.zeros_like(x)
    return jnp.where((jnp.arange(x.shape[1]) % 2 == 0).reshape((1, -1) + (1,) * (x.ndim - 2)),
                     jnp.concatenate([x, even], axis=axis), jnp.concatenate([even, x], axis=axis))


def _stacked_weights(p, sizes, tdec, dbatch):
    pw, ql, kl, rp, gw, dm, nope, vh, d_ff = sizes
    depth = p["w_in"].shape[0]
    w_in = p["w_in"]
    o = 0
    cols = {}
    for name, width in (("a", pw), ("cq", ql), ("ckv", kl), ("kr", rp), ("u", gw), ("v", gw), ("g", 3 * dm)):
        cols[name] = w_in[:, :, o:o + width]
        o += width
    kr_block = jnp.concatenate([cols["kr"], _rotate_half_cols(cols["kr"]),
                                jnp.zeros((depth, dm, LANES - 2 * rp), F32)], axis=2)
    sw = {}
    sw["w_in_main"] = jnp.concatenate([cols["a"], cols["cq"], cols["ckv"], cols["u"], cols["v"], kr_block],
                                      axis=2).astype(BF)
    sw["w_gates"] = cols["g"].astype(BF)
    for k in ("q_norm_g", "kv_norm_g", "v_norm_g", "v_norm_b", "pool_scale", "ln1_g", "ln1_b", "ln2_g", "ln2_b",
              "conv_b"):
        sw[k] = p[k].reshape(depth, 1, -1)
    w_uq = p["w_uq"].reshape(depth, ql, N_HEADS, nope + rp)
    q_nope = w_uq[..., :nope].reshape(depth, ql, N_HEADS * nope)
    q_pe = w_uq[..., nope:]
    sw["w_q"] = jnp.concatenate([q_nope, q_pe.reshape(depth, ql, N_HEADS * rp),
                                 _rotate_half_cols(q_pe).reshape(depth, ql, N_HEADS * rp)], axis=2).astype(BF)
    sw["w_q_t"] = jnp.swapaxes(sw["w_q"], 1, 2)
    sw["w_uk_pad"] = _paired_halves(jnp.transpose(p["w_uk"], (0, 2, 3, 1)), axis=2).astype(BF)
    sw["w_uk_pad_t"] = jnp.swapaxes(sw["w_uk_pad"], 2, 3)
    sw["w_uv_pad"] = _paired_halves(jnp.transpose(p["w_uv"], (0, 2, 1, 3)), axis=3).astype(BF)
    sw["w_uv_t"] = jnp.transpose(p["w_uv"], (0, 2, 3, 1)).astype(BF)
    pool_w = p["pool_w"]
    n_pg, pgw = pool_w.shape[1], pool_w.shape[2]
    sw["pool_bd"] = jnp.einsum("gh,dgij->dgihj", jnp.eye(n_pg, dtype=F32),
                               pool_w).reshape(depth, n_pg * pgw, n_pg * pgw).astype(BF)
    for k in ("w_pool_out", "w_mla_out", "w_gmlp_out", "w_out", "w_up", "w_down"):
        sw[k] = p[k].astype(BF)
    sw["conv_w"] = p["conv_w"]
    w_s = p["w_spatial"]
    b_s = p["b_spatial"]
    n_gg = w_s.shape[1]
    gc = gw // n_gg
    sw["ws"] = jnp.tril(w_s).astype(BF)
    sw["bs"] = jnp.repeat(jnp.swapaxes(b_s, 1, 2), gc, axis=2)
    ws_dec = w_s[:, :, :tdec, :tdec]
    diags = [jnp.pad(jnp.diagonal(ws_dec, offset=-k, axis1=2, axis2=3), ((0, 0), (0, 0), (k, 0)))
             for k in range(tdec)]
    sw["ws_coef"] = jnp.repeat(jnp.swapaxes(jnp.stack(diags, axis=1), 2, 3), gc, axis=3)
    sw["bs_dec"] = jnp.repeat(jnp.swapaxes(b_s[:, :, :tdec], 1, 2), gc, axis=2)
    return sw


def _layer_weights(l, sw, sizes, depth):
    pw, ql, kl, rp, gw, dm, nope, vh, d_ff = sizes
    lw = {k: _LayerView(v, l) for k, v in sw.items()}
    lw.update({"dims": (pw, ql, kl, rp, gw), "nope": nope, "scale": float((nope + rp) ** -0.5),
               "alpha": float((2.0 * depth) ** 0.25)})
    return lw


def _inv_count(pos, pw):
    gwidth = pw // len(POOL_WINDOWS)
    win = jnp.repeat(jnp.asarray(POOL_WINDOWS, jnp.int32), gwidth)
    cnt = jnp.minimum(pos[:, None] + 1, win[None, :]).astype(F32)
    return 1.0 / cnt


def kernel(x_prompt, x_sample, cache_ckv, cache_kpe, state_pool, state_ffn, page_table, w_in, pool_w, pool_scale,
           w_pool_out, q_norm_g, w_uq, kv_norm_g, w_uk, w_uv, w_mla_out, v_norm_g, v_norm_b, w_spatial, b_spatial,
           w_gmlp_out, w_out, ln1_g, ln1_b, w_up, conv_w, conv_b, w_down, ln2_g, ln2_b):
    p = dict(w_in=w_in, pool_w=pool_w, pool_scale=pool_scale, w_pool_out=w_pool_out, q_norm_g=q_norm_g, w_uq=w_uq,
             kv_norm_g=kv_norm_g, w_uk=w_uk, w_uv=w_uv, w_mla_out=w_mla_out, v_norm_g=v_norm_g, v_norm_b=v_norm_b,
             w_spatial=w_spatial, b_spatial=b_spatial, w_gmlp_out=w_gmlp_out, w_out=w_out, ln1_g=ln1_g,
             ln1_b=ln1_b, w_up=w_up, conv_w=conv_w, conv_b=conv_b, w_down=w_down, ln2_g=ln2_g, ln2_b=ln2_b)
    batch, seq, dm = x_prompt.shape
    dbatch, tdec, _ = x_sample.shape
    depth = w_in.shape[0]
    ps = cache_ckv.shape[2]
    n_pages = page_table.shape[1]
    past = n_pages * ps
    kl, rp = cache_ckv.shape[3], cache_kpe.shape[3]
    pw = state_pool.shape[3]
    ql = q_norm_g.shape[1]
    gw = v_norm_g.shape[1]
    nope, vh = w_uk.shape[3], w_uv.shape[3]
    d_ff = w_down.shape[1]
    sizes = (pw, ql, kl, rp, gw, dm, nope, vh, d_ff)
    assert tdec == SUBLANES and pw == 2 * LANES and kl % LANES == 0 and N_HEADS * rp == 2 * LANES
    assert nope * 2 == LANES and vh * 2 == LANES and d_ff % FF_CHUNK == 0

    tm = min(512, seq)
    tq = min(512, seq)
    tm_big = min(2 * tm, seq)
    rows_s = dbatch * tdec
    n_grp = next(g for g in (16, 8, 4, 2, 1) if n_pages % (DECODE_SLOTS * g) == 0)
    n_split = min(2, n_grp)
    assert seq % tm == 0 and tq == tm
    cache_kpe_t = jnp.swapaxes(cache_kpe, 2, 3)

    pos_p = jnp.arange(seq, dtype=jnp.int32)
    pos_s = past + jnp.arange(tdec, dtype=jnp.int32)

    def tables(pos, reps):
        c, s = _rope_tables(pos, rp // 2)
        zeros = jnp.zeros((pos.shape[0], LANES - rp), F32)
        ck, sk = jnp.concatenate([c, zeros], 1), jnp.concatenate([s, zeros], 1)
        cq, sq = jnp.tile(c, (1, N_HEADS)), jnp.tile(s, (1, N_HEADS))
        ic = _inv_count(pos, pw)
        return [jnp.tile(t, (reps, 1)) for t in (ck, sk, cq, sq, ic)]

    cos_kp, sin_kp, cos_qp, sin_qp, icnt_p = tables(pos_p, 1)
    cos_qpt, sin_qpt = cos_qp.T, sin_qp.T
    cos_ks, sin_ks, cos_qs, sin_qs, icnt_s = tables(pos_s, dbatch)

    xp = x_prompt.reshape(batch * seq, dm)
    xs = x_sample.reshape(rows_s, dm)
    outs = [[] for _ in range(9)]
    sw = _stacked_weights(p, sizes, tdec, dbatch)
    for l in range(depth):
        lw = _layer_weights(l, sw, sizes, depth)
        a_in, _, cqt, ckvn, kpe, kv, vt, u, vn = _in_proj(xp, lw, cos_kp, sin_kp, tm=tm, vn_dtype=BF)
        yb_pre = _prompt_attention(cqt, kv, vt, cos_qpt, sin_qpt, lw, batch=batch, seq=seq, tq=tq)
        xp = _merge_prompt(xp, a_in, icnt_p, u, vn, yb_pre, lw, tm=tm_big, seq=seq, n_parts=tm_big // tm)
        xp, tail_p = _ffn(xp, None, lw, tm=tm_big, seq=seq, n_seq=batch, n_parts=tm_big // tm)
        outs[0].append(ckvn.reshape(batch * seq // ps, ps, kl))
        outs[1].append(kpe.reshape(batch * seq // ps, ps, rp))
        outs[4].append(a_in.reshape(batch, seq, pw)[:, seq - POOL_KEEP:])
        outs[6].append(tail_p)
        a_s, cqn_s, _, ckvn_s, kpe_s, _, _, u_s, vn_s = _in_proj(xs, lw, cos_ks, sin_ks, tm=rows_s, vn_dtype=F32)
        qlat_s, qpe_s = _q_proj(cqn_s, lw, cos_qs, sin_qs)
        yb_s = _sample_attention(qlat_s, qpe_s, ckvn_s, kpe_s, cache_ckv, cache_kpe_t, page_table, l, lw,
                                 dbatch=dbatch, tdec=tdec, n_grp=n_grp, n_split=n_split)
        pool_ext = jnp.concatenate([jnp.zeros((dbatch, 1, pw), F32), state_pool[l],
                                    a_s.reshape(dbatch, tdec, pw)], axis=1)
        xs = _merge_sample(xs, pool_ext, icnt_s, u_s, vn_s, yb_s, lw, tdec=tdec)
        prefix = jnp.pad(state_ffn[l], ((0, 0), (SUBLANES - (CONV_W - 1), 0), (0, 0))).reshape(rows_s, 2 * d_ff)
        xs, tail_s = _ffn(xs, prefix, lw, tm=rows_s, seq=tdec, n_seq=dbatch)
        outs[2].append(ckvn_s.reshape(dbatch, tdec, kl))
        outs[3].append(kpe_s.reshape(dbatch, tdec, rp))
        outs[5].append(pool_ext[:, -POOL_KEEP:])
        outs[7].append(tail_s)
        outs[8].append(vn_s.reshape(dbatch, tdec, gw))
    stacked = [jnp.stack(o) for o in outs]
    new_ckv_p, new_kpe_p, new_ckv_s, new_kpe_s, pool_p, pool_s, ffn_p, ffn_s, gv_s = stacked
    return (xp.reshape(batch, seq, dm), xs.reshape(dbatch, tdec, dm), new_ckv_p, new_kpe_p, new_ckv_s, new_kpe_s,
            pool_p, pool_s, ffn_p, ffn_s, gv_s)
```

```python
import functools

import jax
import jax.numpy as jnp
import numpy as np
from jax import lax
from jax.experimental import pallas as pl
from jax.experimental.pallas import tpu as pltpu

BF = jnp.bfloat16
F32 = jnp.float32

POOL_WINDOWS = (2, 4, 8, 16)
POOL_KEEP = max(POOL_WINDOWS) - 1
N_HEADS = 8
ROPE_THETA = 10000.0
RMS_EPS = 1e-6
LN_EPS = 1e-5
CONV_W = 3
SUBLANES = 8
LANES = 128
FF_CHUNK = 256
NEG_BIG = -1e30
SCORE_LOOKAHEAD = 8
DECODE_SLOTS = 4
DECODE_AHEAD = 3
VMEM_LIMIT = 56 * 1024 * 1024


def _params(*sem):
    return pltpu.CompilerParams(dimension_semantics=sem, vmem_limit_bytes=VMEM_LIMIT)


class _LayerView:
    def __init__(self, stacked, layer):
        self.stacked, self.layer = stacked, layer

    @property
    def shape(self):
        return self.stacked.shape[1:]


def _const_spec(x):
    if isinstance(x, _LayerView):
        nd, layer = len(x.shape), x.layer
        return pl.BlockSpec((None,) + tuple(x.shape), lambda *_: (layer,) + (0,) * nd,
                            pipeline_mode=pl.Buffered(1))
    shape = tuple(getattr(x, "shape", x))
    nd = len(shape)
    return pl.BlockSpec(shape, lambda *_: (0,) * nd, pipeline_mode=pl.Buffered(1))


def _operands(args):
    return [a.stacked if isinstance(a, _LayerView) else a for a in args]


def _rmsnorm(x, g):
    return x * lax.rsqrt(jnp.mean(x * x, axis=-1, keepdims=True) + RMS_EPS) * g


def _layernorm(x, g, b):
    mu = jnp.mean(x, axis=-1, keepdims=True)
    xc = x - mu
    var = jnp.mean(xc * xc, axis=-1, keepdims=True)
    return xc * lax.rsqrt(var + LN_EPS) * g + b


def _dot(a, b):
    return jnp.dot(a, b, preferred_element_type=F32)


def _dot_nt(a, b):
    return lax.dot_general(a, b, (((1,), (1,)), ((), ())), preferred_element_type=F32)


def _shift_rows(y3, prev3, k):
    t = lax.broadcasted_iota(jnp.int32, y3.shape, 1)
    return jnp.where(t >= k, pltpu.roll(y3, k, 1), pltpu.roll(prev3, k, 1))


def _prev_groups(y3, first):
    if y3.shape[0] == 1:
        return first
    return jnp.concatenate([first, y3[:-1]], axis=0)


def _in_proj_body(x_ref, w_ref, wkv_ref, qg_ref, kvg_ref, vg_ref, vb_ref, cos_ref, sin_ref,
                  a_ref, cq_ref, cqt_ref, ckv_ref, kpe_ref, kh_ref, vt_ref, u_ref, vn_ref, *, dims):
    pw, ql, kl, rp, gw = dims
    hn = wkv_ref.shape[1] // 2
    tm = x_ref.shape[0]
    n_part = 2 if tm % (2 * LANES) == 0 else 1
    part = tm // n_part
    zs = [_dot(x_ref[i * part:(i + 1) * part, :].astype(BF), w_ref[...]) for i in range(n_part)]
    for i, z in enumerate(zs):
        rows = slice(i * part, (i + 1) * part)
        o = 0
        a_ref[rows, :] = z[:, o:o + pw]
        o += pw
        cqn = _rmsnorm(z[:, o:o + ql], qg_ref[...])
        cq_ref[rows, :] = cqn.astype(BF)
        cqt_ref[0, :, rows] = cqn.T.astype(BF)
        o += ql
        ckvn = _rmsnorm(z[:, o:o + kl], kvg_ref[...])
        ckv_ref[rows, :] = ckvn
        o += kl
        u_ref[rows, :] = z[:, o:o + gw].astype(BF)
        o += gw
        vn_ref[rows, :] = _layernorm(z[:, o:o + gw], vg_ref[...], vb_ref[...]).astype(vn_ref.dtype)
        o += gw
        kr = z[:, o:o + LANES]
        kpe = kr * cos_ref[rows, :] + pltpu.roll(kr, LANES - rp, 1) * sin_ref[rows, :]
        kpe_ref[rows, :] = kpe[:, :rp]
        kvh = _dot(ckvn.astype(BF), wkv_ref[...])
        kpe_b = kpe.astype(BF)
        for j in range(hn // LANES):
            kh_ref[j, rows, 0:LANES] = kvh[:, j * LANES:(j + 1) * LANES].astype(BF)
            kh_ref[j, rows, LANES:2 * LANES] = kpe_b
        vt_ref[0, :, rows] = kvh[:, hn:].T.astype(BF)


def _in_proj(x2d, lw, cos_k, sin_k, *, tm, vn_dtype):
    rows, dm = x2d.shape
    dims = lw["dims"]
    pw, ql, kl, rp, gw = dims
    ntab = cos_k.shape[0] // tm
    weights = [lw["w_in_main"], lw["w_kv_heads"], lw["q_norm_g"], lw["kv_norm_g"], lw["v_norm_g"], lw["v_norm_b"]]
    hn = lw["w_kv_heads"].shape[1] // 2
    row_spec = lambda w: pl.BlockSpec((tm, w), lambda i: (i, 0))
    tab_spec = pl.BlockSpec((tm, LANES), lambda i: (i % ntab, 0))
    return pl.pallas_call(
        functools.partial(_in_proj_body, dims=dims),
        grid=(rows // tm,),
        in_specs=[row_spec(dm)] + [_const_spec(w) for w in weights] + [tab_spec, tab_spec],
        out_specs=[row_spec(pw), row_spec(ql), pl.BlockSpec((1, ql, tm), lambda i: (i, 0, 0)), row_spec(kl),
                   row_spec(rp), pl.BlockSpec((hn // LANES, tm, 2 * LANES), lambda i: (0, i, 0)),
                   pl.BlockSpec((1, hn, tm), lambda i: (i, 0, 0)), row_spec(gw), row_spec(gw)],
        out_shape=[jax.ShapeDtypeStruct((rows, pw), F32), jax.ShapeDtypeStruct((rows, ql), BF),
                   jax.ShapeDtypeStruct((rows // tm, ql, tm), BF),
                   jax.ShapeDtypeStruct((rows, kl), F32), jax.ShapeDtypeStruct((rows, rp), F32),
                   jax.ShapeDtypeStruct((hn // LANES, rows, 2 * LANES), BF),
                   jax.ShapeDtypeStruct((rows // tm, hn, tm), BF),
                   jax.ShapeDtypeStruct((rows, gw), BF), jax.ShapeDtypeStruct((rows, gw), vn_dtype)],
        compiler_params=_params("arbitrary"),
        name="in_proj",
    )(x2d, *_operands(weights), cos_k, sin_k)


def _q_heads(cq, w_ref, wuk_ref, cos, sin, nope, rp, scale):
    hn = N_HEADS * nope
    hr = N_HEADS * rp
    q = _dot(cq, w_ref[...])
    qpe = (q[:, hn:hn + hr] * cos + q[:, hn + hr:hn + 2 * hr] * sin) * scale
    qlat = [_dot(q[:, (h // 2) * LANES:(h // 2 + 1) * LANES].astype(BF), wuk_ref[h]) * scale
            for h in range(N_HEADS)]
    return qlat, qpe


def _q_body(cq_ref, w_ref, wuk_ref, cos_ref, sin_ref, qlat_ref, qpe_ref, *, nope, rp, scale):
    qlat, qpe = _q_heads(cq_ref[...], w_ref, wuk_ref, cos_ref[...], sin_ref[...], nope, rp, scale)
    qpe_ref[...] = qpe
    for h in range(N_HEADS):
        qlat_ref[h] = qlat[h]


def _q_proj(cq, lw, cos_q, sin_q):
    rows, ql = cq.shape
    pw, _, kl, rp, gw = lw["dims"]
    hr = N_HEADS * rp
    args = [cq, lw["w_q"], lw["w_uk_pad"], cos_q, sin_q]
    return pl.pallas_call(
        functools.partial(_q_body, nope=lw["nope"], rp=rp, scale=lw["scale"]),
        grid=(1,),
        in_specs=[_const_spec(a) for a in args],
        out_specs=[_const_spec((N_HEADS, rows, kl)), _const_spec((rows, hr))],
        out_shape=[jax.ShapeDtypeStruct((N_HEADS, rows, kl), F32), jax.ShapeDtypeStruct((rows, hr), F32)],
        compiler_params=_params("arbitrary"),
        name="q_proj",
    )(*_operands(args))


def _heads_out(o, wuv_ref, rows_per_head, full_m):
    outs = []
    ob = o.astype(BF)
    for j in range(N_HEADS // 2):
        acc = None
        for h in (2 * j, 2 * j + 1):
            sl = slice(h * rows_per_head, (h + 1) * rows_per_head)
            if full_m:
                y = _dot(ob, wuv_ref[h])[sl]
            else:
                y = _dot(ob[sl], wuv_ref[h])
            acc = y if acc is None else acc + y
        outs.append(acc)
    return outs


def _attn_body(cqt_ref, cos_ref, sin_ref, wqt_ref, kh_ref, vt_ref, o_ref,
               q_ref, m_ref, l_ref, acc_ref, *, tq, nope, rp, vh, scale):
    qi = pl.program_id(1)
    hn = N_HEADS * nope
    hr = N_HEADS * rp
    m_ref[...] = jnp.full(m_ref.shape, NEG_BIG, F32)
    l_ref[...] = jnp.zeros(l_ref.shape, F32)
    acc_ref[...] = jnp.zeros(acc_ref.shape, F32)
    qt = _dot(wqt_ref[...], cqt_ref[0])
    qpe_t = ((qt[hn:hn + hr] * cos_ref[...] + qt[hn + hr:hn + 2 * hr] * sin_ref[...]) * scale).astype(BF)
    q_ref[:, LANES + rp:, :] = jnp.zeros((N_HEADS, q_ref.shape[1] - LANES - rp, tq), BF)
    half_of_row = lax.broadcasted_iota(jnp.int32, (LANES, tq), 0) // nope
    for h in range(N_HEADS):
        pair = qt[(h // 2) * LANES:(h // 2 + 1) * LANES] * scale
        q_ref[h, 0:LANES, :] = jnp.where(half_of_row == h % 2, pair, 0.0).astype(BF)
        q_ref[h, LANES:LANES + rp, :] = qpe_t[h * rp:(h + 1) * rp]

    def step(masked, ki):
        half = tq // 2
        parts = [(0, half, half), (half, tq, tq)] if masked and half % LANES == 0 else [(0, tq, tq)]

        def scores(h):
            out = []
            for q0, q1, nk in parts:
                s = _dot(kh_ref[h // 2, ki, 0:nk, :], q_ref[h, :, q0:q1])
                if masked:
                    kpos = lax.broadcasted_iota(jnp.int32, s.shape, 0)
                    qpos = lax.broadcasted_iota(jnp.int32, s.shape, 1) + q0
                    s = jnp.where(kpos <= qpos, s, NEG_BIG)
                out.append(s)
            return out

        pending = [scores(h) for h in range(SCORE_LOOKAHEAD)]
        for h in range(N_HEADS):
            s_parts = pending.pop(0)
            if h + SCORE_LOOKAHEAD < N_HEADS:
                pending.append(scores(h + SCORE_LOOKAHEAD))
            for (q0, q1, nk), s in zip(parts, s_parts):
                m_prev = m_ref[h, :, q0:q1]
                m_new = jnp.maximum(m_prev, jnp.max(s, axis=0, keepdims=True))
                alpha = jnp.exp(m_prev - m_new)
                p = jnp.exp(s - m_new)
                l_ref[h, :, q0:q1] = alpha * l_ref[h, :, q0:q1] + jnp.sum(p, axis=0, keepdims=True)
                acc_ref[h, :, q0:q1] = (alpha * acc_ref[h, :, q0:q1]
                                        + _dot(vt_ref[ki, h * vh:(h + 1) * vh, 0:nk], p.astype(BF)))
                m_ref[h, :, q0:q1] = m_new

    def unmasked(ki, carry):
        step(False, ki)
        return carry

    step(True, qi)
    lax.fori_loop(0, qi, unmasked, 0)
    out_t = jnp.concatenate([acc_ref[h] * (1.0 / l_ref[h]) for h in range(N_HEADS)], axis=0)
    o_ref[...] = out_t.T.astype(o_ref.dtype)


def _prompt_attention(cqt, kh, vt, cos_qt, sin_qt, lw, *, batch, seq, tq):
    _, ql, kl, rp, _ = lw["dims"]
    n_pair, _, width = kh.shape
    hr = N_HEADS * rp
    nq = seq // tq
    hv = vt.shape[1]
    vh = hv // N_HEADS
    assert vt.shape == (batch * nq, hv, tq) and cqt.shape == (batch * nq, ql, tq)
    tab_spec = pl.BlockSpec((hr, tq), lambda b, i: (0, i))
    return pl.pallas_call(
        functools.partial(_attn_body, tq=tq, nope=lw["nope"], rp=rp, vh=vh, scale=lw["scale"]),
        grid=(batch, nq),
        in_specs=[pl.BlockSpec((1, ql, tq), lambda b, i: (b * nq + i, 0, 0)), tab_spec, tab_spec,
                  _const_spec(lw["w_q_t"]),
                  pl.BlockSpec((n_pair, nq, tq, width), lambda b, i: (0, b, 0, 0)),
                  pl.BlockSpec((nq, hv, tq), lambda b, i: (b, 0, 0))],
        out_specs=pl.BlockSpec((tq, hv), lambda b, i: (b * nq + i, 0)),
        out_shape=jax.ShapeDtypeStruct((batch * seq, hv), BF),
        scratch_shapes=[pltpu.VMEM((N_HEADS, width, tq), BF), pltpu.VMEM((N_HEADS, 1, tq), F32),
                        pltpu.VMEM((N_HEADS, 1, tq), F32), pltpu.VMEM((N_HEADS, vh, tq), F32)],
        compiler_params=_params("arbitrary", "arbitrary"),
        name="prompt_attention",
    )(*_operands([cqt, cos_qt, sin_qt, lw["w_q_t"], kh.reshape(n_pair, batch * nq, tq, width), vt]))


def _decode_body(pt_ref, qlat_ref, qpe_ref, ckvn_ref, kpen_ref, wuv_ref, ckv_hbm, kpe_hbm, o_ref,
                 kbuf, pbuf, sem_k, sem_p, *, layer, n_grp, n_groups, n_split, tdec, rp, kl):
    b = pl.program_id(0)
    rows = N_HEADS * tdec

    def page_copies(sample, group, slot, table=True):
        copies = []
        for j in range(n_grp):
            pid = pt_ref[sample, group * n_grp + j] if table else 0
            copies.append(pltpu.make_async_copy(ckv_hbm.at[layer, pid], kbuf.at[slot, j], sem_k.at[slot]))
            copies.append(pltpu.make_async_copy(kpe_hbm.at[layer, pid], pbuf.at[slot, j], sem_p.at[slot]))
        return copies

    def start_ahead(g):
        nxt = g + DECODE_AHEAD
        slot = nxt % DECODE_SLOTS
        if nxt < n_groups:
            for c in page_copies(b, nxt, slot):
                c.start()
        else:
            @pl.when(b + 1 < pl.num_programs(0))
            def _():
                for c in page_copies(b + 1, nxt - n_groups, slot):
                    c.start()

    @pl.when(b == 0)
    def _():
        for g in range(DECODE_AHEAD):
            for c in page_copies(0, g, g % DECODE_SLOTS):
                c.start()

    q = qlat_ref[...].reshape(rows, kl).astype(BF)
    qpe = qpe_ref[...]
    qp = jnp.concatenate([qpe[:, h * rp:(h + 1) * rp] for h in range(N_HEADS)], axis=0).astype(BF)

    def update(state, s, keys):
        m_prev, l_prev, acc = state
        m_new = jnp.maximum(m_prev, jnp.max(s, axis=-1, keepdims=True))
        alpha = jnp.exp(m_prev - m_new)
        p = jnp.exp(s - m_new)
        l_new = alpha * l_prev + jnp.sum(p, axis=-1, keepdims=True)
        acc = alpha * acc
        off = 0
        pb = p.astype(BF)
        for k in keys:
            acc = acc + _dot(pb[:, off:off + k.shape[0]], k)
            off += k.shape[0]
        return m_new, l_new, acc

    states = [(jnp.full((rows, 1), NEG_BIG, F32), jnp.zeros((rows, 1), F32), jnp.zeros((rows, kl), F32))
              for _ in range(n_split)]
    per = n_grp // n_split
    for g in range(n_groups):
        slot = g % DECODE_SLOTS
        start_ahead(g)
        for c in page_copies(b, g, slot, table=False):
            c.wait()
        keys = [kbuf[slot, j].astype(BF) for j in range(n_grp)]
        scores = [_dot_nt(q, k) + _dot(qp, pbuf[slot, j].astype(BF)) for j, k in enumerate(keys)]
        for i in range(n_split):
            states[i] = update(states[i], jnp.concatenate(scores[i * per:(i + 1) * per], axis=1),
                               keys[i * per:(i + 1) * per])

    pad = 2 * tdec
    kn = jnp.concatenate([ckvn_ref[...], jnp.zeros((pad - tdec, kl), F32)], axis=0).astype(BF)
    kpn = jnp.concatenate([kpen_ref[...], jnp.zeros((pad - tdec, rp), F32)], axis=0).astype(BF)
    sn = _dot_nt(q, kn) + _dot_nt(qp, kpn)
    qpos = lax.broadcasted_iota(jnp.int32, (N_HEADS, tdec, pad), 1).reshape(rows, pad)
    kpos = lax.broadcasted_iota(jnp.int32, (rows, pad), 1)
    states[0] = update(states[0], jnp.where(kpos <= qpos, sn, NEG_BIG), [kn])
    m_all = states[0][0]
    for m_i, _, _ in states[1:]:
        m_all = jnp.maximum(m_all, m_i)
    l_all = jnp.zeros_like(m_all)
    o = jnp.zeros((rows, kl), F32)
    for m_i, l_i, acc_i in states:
        w = jnp.exp(m_i - m_all)
        l_all = l_all + w * l_i
        o = o + w * acc_i
    o = o * (1.0 / l_all)
    for j, y in enumerate(_heads_out(o, wuv_ref, tdec, full_m=True)):
        o_ref[:, j * LANES:(j + 1) * LANES] = y.astype(o_ref.dtype)


def _sample_attention(qlat, qpe, ckvn, kpen, cache_ckv, cache_kpe_t, page_table, layer, lw, *, dbatch, tdec, n_grp,
                      n_split):
    _, _, kl, rp, _ = lw["dims"]
    n_pages = page_table.shape[1]
    ps = cache_ckv.shape[2]
    hv = lw["w_uv_pad"].shape[2] * N_HEADS // 2
    hr = N_HEADS * rp
    rows = N_HEADS * tdec

    n_groups = n_pages // n_grp
    assert n_groups % DECODE_SLOTS == 0
    in_specs = [pl.BlockSpec((N_HEADS, tdec, kl), lambda b, pt: (0, b, 0)),
                pl.BlockSpec((tdec, hr), lambda b, pt: (b, 0)),
                pl.BlockSpec((tdec, kl), lambda b, pt: (b, 0)),
                pl.BlockSpec((tdec, rp), lambda b, pt: (b, 0)),
                _const_spec(lw["w_uv_pad"]),
                pl.BlockSpec(memory_space=pl.ANY), pl.BlockSpec(memory_space=pl.ANY)]
    grid_spec = pltpu.PrefetchScalarGridSpec(
        num_scalar_prefetch=1, grid=(dbatch,), in_specs=in_specs,
        out_specs=pl.BlockSpec((tdec, hv), lambda b, pt: (b, 0)),
        scratch_shapes=[pltpu.VMEM((DECODE_SLOTS, n_grp, ps, kl), F32), pltpu.VMEM((DECODE_SLOTS, n_grp, rp, ps), F32),
                        pltpu.SemaphoreType.DMA((DECODE_SLOTS,)), pltpu.SemaphoreType.DMA((DECODE_SLOTS,))])
    return pl.pallas_call(
        functools.partial(_decode_body, layer=layer, n_grp=n_grp, n_groups=n_groups, n_split=n_split, tdec=tdec,
                          rp=rp, kl=kl),
        grid_spec=grid_spec,
        out_shape=jax.ShapeDtypeStruct((dbatch * tdec, hv), F32),
        compiler_params=_params("arbitrary"),
        name="sample_attention",
    )(*_operands([page_table, qlat, qpe, ckvn, kpen, lw["w_uv_pad"], cache_ckv, cache_kpe_t]))


def _window_select(sums, shape):
    gw = shape[-1] // len(POOL_WINDOWS)
    grp = lax.broadcasted_iota(jnp.int32, shape, len(shape) - 1) // gw
    out = sums[-1]
    for gi in range(len(POOL_WINDOWS) - 2, -1, -1):
        out = jnp.where(grp == gi, sums[gi], out)
    return out


def _spatial_matmul(vn, ws_ref, bs_ref):
    n_g, clen, _ = ws_ref.shape
    gc = vn.shape[1] // n_g
    grp = lax.broadcasted_iota(jnp.int32, (clen, vn.shape[1]), 1) // gc
    parts = []
    for c in range(vn.shape[0] // clen):
        vc = vn[c * clen:(c + 1) * clen]
        s = _dot(ws_ref[n_g - 1], vc)
        for g in range(n_g - 2, -1, -1):
            s = jnp.where(grp == g, _dot(ws_ref[g], vc), s)
        parts.append(s + bs_ref[...])
    return parts[0] if len(parts) == 1 else jnp.concatenate(parts, axis=0)


def _merge_tail(x, d, u, s, yb_pre, wg_ref, pbd_ref, psc_ref, wpo_ref, wgo_ref, wmo_ref,
                wo_ref, lng_ref, lnb_ref, alpha):
    dm = x.shape[1]
    xb = x.astype(BF)
    ya = _dot(d.astype(BF), pbd_ref[...]) * psc_ref[...]
    ya = _dot(ya.astype(BF), wpo_ref[...])
    m = jax.nn.sigmoid(_dot(xb, wg_ref[:, 0:dm])) * ya
    yb = _dot(yb_pre.astype(BF), wmo_ref[...])
    m = m + jax.nn.sigmoid(_dot(xb, wg_ref[:, dm:2 * dm])) * yb
    yc = _dot((u.astype(F32) * s).astype(BF), wgo_ref[...])
    m = m + jax.nn.sigmoid(_dot(xb, wg_ref[:, 2 * dm:3 * dm])) * yc
    y = alpha * x + _dot(m.astype(BF), wo_ref[...])
    return _layernorm(y, lng_ref[...], lnb_ref[...])


def _merge_prompt_body(x_ref, a_ref, aprev_ref, icnt_ref, u_ref, vn_ref, yb_ref, *rest, tiles_per_seq, n_parts,
                       alpha):
    ws_ref, bs_ref = rest[:2]
    w_refs, o_ref = rest[2:-1], rest[-1]
    i = pl.program_id(0)
    a = a_ref[...]
    tm, pw = a.shape
    hist = jnp.where(i % tiles_per_seq == 0, 0.0, aprev_ref[...])
    n_hist = hist.shape[0] // SUBLANES
    ext = jnp.concatenate([hist, a], axis=0).reshape(tm // SUBLANES + n_hist, SUBLANES, pw)
    zero = jnp.zeros((1, SUBLANES, pw), F32)
    s2 = ext + _shift_rows(ext, _prev_groups(ext, zero), 1)
    s4 = s2 + _shift_rows(s2, _prev_groups(s2, zero), 2)
    s8 = s4 + _shift_rows(s4, _prev_groups(s4, zero), 4)
    s16 = s8 + _prev_groups(s8, zero)
    sel = _window_select([s[n_hist:] for s in (s2, s4, s8, s16)], (tm // SUBLANES, SUBLANES, pw))
    d = sel.reshape(tm, pw) * icnt_ref[...] - a
    part = tm // n_parts
    for j in range(n_parts):
        rows = slice(j * part, (j + 1) * part)
        o_ref[rows, :] = _merge_tail(x_ref[rows, :], d[rows], u_ref[rows, :],
                                     _spatial_matmul(vn_ref[rows, :], ws_ref, bs_ref), yb_ref[rows, :],
                                     *w_refs, alpha)


def _merge_sample_body(x_ref, ext_ref, icnt_ref, u_ref, vn_ref, yb_ref, coef_ref, bs_ref, *rest, tdec, alpha):
    w_refs, o_ref = rest[:-1], rest[-1]
    nb, ext_len, pw = ext_ref.shape
    acc = None
    sums = []
    for j in range(max(POOL_WINDOWS)):
        cur = ext_ref[:, ext_len - tdec - j:ext_len - j, :]
        acc = cur if acc is None else acc + cur
        if j + 1 in POOL_WINDOWS:
            sums.append(acc)
    tok = ext_ref[:, ext_len - tdec:ext_len, :]
    sel = _window_select(sums, (nb, tdec, pw))
    d = sel.reshape(nb * tdec, pw) * icnt_ref[...] - tok.reshape(nb * tdec, pw)
    vn3 = vn_ref[...].reshape(nb, tdec, vn_ref.shape[1])
    s = coef_ref[0] * vn3 + bs_ref[...]
    for k in range(1, tdec):
        s = s + coef_ref[k] * pltpu.roll(vn3, k, 1)
    o_ref[...] = _merge_tail(x_ref[...], d, u_ref[...], s.reshape(nb * tdec, vn_ref.shape[1]), yb_ref[...],
                             *w_refs, alpha)


def _merge_weights(lw):
    return [lw["w_gates"], lw["pool_bd"], lw["pool_scale"], lw["w_pool_out"],
            lw["w_gmlp_out"], lw["w_mla_out"], lw["w_out"], lw["ln1_g"], lw["ln1_b"]]


def _merge_prompt(x2d, a_in, icnt, u, vn, yb_pre, lw, *, tm, seq, n_parts):
    rows, dm = x2d.shape
    pw, _, _, _, gw = lw["dims"]
    tiles_per_seq = seq // tm
    hist_rows = 2 * SUBLANES
    hist_per_tile = tm // hist_rows
    weights = [lw["ws"], lw["bs"]] + _merge_weights(lw)
    row_spec = lambda w: pl.BlockSpec((tm, w), lambda i: (i, 0))
    in_specs = [row_spec(dm), row_spec(pw),
                pl.BlockSpec((hist_rows, pw), lambda i: (jnp.maximum(i * hist_per_tile - 1, 0), 0)),
                pl.BlockSpec((tm, pw), lambda i: (i % tiles_per_seq, 0)),
                row_spec(gw), row_spec(gw), row_spec(yb_pre.shape[1])]
    in_specs += [_const_spec(w) for w in weights]
    return pl.pallas_call(
        functools.partial(_merge_prompt_body, tiles_per_seq=tiles_per_seq, n_parts=n_parts, alpha=lw["alpha"]),
        grid=(rows // tm,), in_specs=in_specs, out_specs=row_spec(dm),
        out_shape=jax.ShapeDtypeStruct((rows, dm), F32),
        compiler_params=_params("arbitrary"),
        name="merge_prompt",
    )(x2d, a_in, a_in, icnt, u, vn, yb_pre, *_operands(weights))


def _merge_sample(x2d, ext, icnt, u, vn, yb_pre, lw, *, tdec):
    rows, dm = x2d.shape
    weights = [lw["ws_coef"], lw["bs_dec"]] + _merge_weights(lw)
    args = [x2d, ext, icnt, u, vn, yb_pre] + weights
    return pl.pallas_call(
        functools.partial(_merge_sample_body, tdec=tdec, alpha=lw["alpha"]),
        grid=(1,), in_specs=[_const_spec(a) for a in args], out_specs=_const_spec((rows, dm)),
        out_shape=jax.ShapeDtypeStruct((rows, dm), F32),
        compiler_params=_params("arbitrary"),
        name="merge_sample",
    )(*_operands(args))


def _conv_gate(cur, back1, back2, w, b):
    half = cur.shape[-1] // 2
    conv = (b + w[0:1] * back2 + w[1:2] * back1 + w[2:3] * cur).reshape(cur.shape[0] * SUBLANES, 2 * half)
    return jax.nn.silu(conv[:, :half]) * conv[:, half:]


def _shift_rows_ext(ext, k):
    r = pltpu.roll(ext, k, 1)
    t = lax.broadcasted_iota(jnp.int32, r[1:].shape, 1)
    return jnp.where(t >= k, r[1:], r[:-1])


def _ff_cols(ref_or_val, c, d_ff):
    lo = c * FF_CHUNK
    return jnp.concatenate([ref_or_val[:, lo:lo + FF_CHUNK], ref_or_val[:, d_ff + lo:d_ff + lo + FF_CHUNK]],
                           axis=-1)


def _ff_store_tail(tail, c, d_ff, idx, val):
    lo = c * FF_CHUNK
    tail[idx + (slice(lo, lo + FF_CHUNK),)] = val[..., :FF_CHUNK]
    tail[idx + (slice(d_ff + lo, d_ff + lo + FF_CHUNK),)] = val[..., FF_CHUNK:]


def _ffn_sample_body(x_ref, prefix_ref, wup_ref, cw_ref, cb_ref, wdn_ref, lng_ref, lnb_ref, o_ref, tail_ref,
                     h_ref, *, alpha):
    d_ff = wdn_ref.shape[0]
    cw2 = 2 * FF_CHUNK
    tm = x_ref.shape[0]
    grp = tm // SUBLANES
    x = x_ref[...]
    xb = x.astype(BF)
    for c in range(d_ff // FF_CHUNK):
        a3 = _dot(xb, _ff_cols(wup_ref, c, d_ff)).reshape(grp, SUBLANES, cw2)
        prev = _ff_cols(prefix_ref, c, d_ff).reshape(grp, SUBLANES, cw2)
        h = _conv_gate(a3, _shift_rows(a3, prev, 1), _shift_rows(a3, prev, 2), _ff_cols(cw_ref, c, d_ff),
                       _ff_cols(cb_ref, c, d_ff))
        h_ref[:, c * FF_CHUNK:(c + 1) * FF_CHUNK] = h.astype(BF)
        _ff_store_tail(tail_ref, c, d_ff, (slice(None), slice(None)), a3[:, SUBLANES - (CONV_W - 1):, :])
    o_ref[...] = _layernorm(alpha * x + _dot(h_ref[...], wdn_ref[...]), lng_ref[...], lnb_ref[...])


def _ffn_prompt_body(x_ref, wup_ref, cw_ref, cb_ref, wdn_ref, lng_ref, lnb_ref, o_ref, tail_ref,
                     h_ref, carry_ref, *, tiles_per_seq, n_parts, alpha):
    d_ff = wdn_ref.shape[0]
    n_chunks = d_ff // FF_CHUNK
    cw2 = 2 * FF_CHUNK
    part = x_ref.shape[0] // n_parts

    @pl.when(pl.program_id(0) % tiles_per_seq == 0)
    def _():
        carry_ref[...] = jnp.zeros(carry_ref.shape, F32)

    for j in range(n_parts):
        rows = slice(j * part, (j + 1) * part)
        x = x_ref[rows, :]
        xb = x.astype(BF)

        def up(c):
            return _dot(xb, _ff_cols(wup_ref, c, d_ff))

        a_next = up(0)
        for c in range(n_chunks):
            a = a_next
            if c + 1 < n_chunks:
                a_next = up(c + 1)
            ext = jnp.concatenate([carry_ref[c], a], axis=0).reshape(part // SUBLANES + 1, SUBLANES, cw2)
            h = _conv_gate(ext[1:], _shift_rows_ext(ext, 1), _shift_rows_ext(ext, 2), _ff_cols(cw_ref, c, d_ff),
                           _ff_cols(cb_ref, c, d_ff))
            h_ref[j, :, c * FF_CHUNK:(c + 1) * FF_CHUNK] = h.astype(BF)
            last = a[part - SUBLANES:]
            carry_ref[c] = last
            if j == n_parts - 1:
                _ff_store_tail(tail_ref, c, d_ff, (0, slice(None)), last[SUBLANES - (CONV_W - 1):, :])
        o_ref[rows, :] = _layernorm(alpha * x + _dot(h_ref[j], wdn_ref[...]), lng_ref[...], lnb_ref[...])


def _ffn(x2d, prefix, lw, *, tm, seq, n_seq, n_parts=1):
    rows, dm = x2d.shape
    wup, cw, cb, wdn = lw["w_up"], lw["conv_w"], lw["conv_b"], lw["w_down"]
    d_ff = wdn.shape[0]
    n_chunks = d_ff // FF_CHUNK
    cw2 = 2 * FF_CHUNK
    has_prefix = prefix is not None
    tiles_per_seq = max(seq // tm, 1)
    keep = CONV_W - 1
    in_specs = [pl.BlockSpec((tm, dm), lambda i: (i, 0))]
    args = [x2d]
    if has_prefix:
        in_specs.append(_const_spec(prefix.shape))
        args.append(prefix)
        tail_shape = (rows // SUBLANES, keep, 2 * d_ff)
        tail_spec = _const_spec(tail_shape)
    else:
        tail_shape = (n_seq, keep, 2 * d_ff)
        tail_spec = pl.BlockSpec((1, keep, 2 * d_ff), lambda i: (i // tiles_per_seq, 0, 0))
    weights = [wup, cw, cb, wdn, lw["ln2_g"], lw["ln2_b"]]
    in_specs += [_const_spec(w) for w in weights]
    if has_prefix:
        body = functools.partial(_ffn_sample_body, alpha=lw["alpha"])
        scratch = [pltpu.VMEM((tm, d_ff), BF)]
    else:
        body = functools.partial(_ffn_prompt_body, tiles_per_seq=tiles_per_seq, n_parts=n_parts, alpha=lw["alpha"])
        scratch = [pltpu.VMEM((n_parts, tm // n_parts, d_ff), BF), pltpu.VMEM((n_chunks, SUBLANES, cw2), F32)]
    return pl.pallas_call(
        body, grid=(rows // tm,), in_specs=in_specs,
        out_specs=[pl.BlockSpec((tm, dm), lambda i: (i, 0)), tail_spec],
        out_shape=[jax.ShapeDtypeStruct((rows, dm), F32), jax.ShapeDtypeStruct(tail_shape, F32)],
        scratch_shapes=scratch,
        compiler_params=_params("arbitrary"),
        name="ffn_sample" if has_prefix else "ffn_prompt",
    )(*args, *_operands(weights))


def _rope_tables(pos, half):
    inv = ROPE_THETA ** (-jnp.arange(half, dtype=F32) / half)
    ang = pos.astype(F32)[:, None] * inv[None, :]
    cos, sin = jnp.cos(ang), jnp.sin(ang)
    return jnp.concatenate([cos, cos], axis=-1), jnp.concatenate([sin, sin], axis=-1)


def _rotate_half_cols(w):
    half = w.shape[-1] // 2
    return jnp.concatenate([-w[..., half:], w[..., :half]], axis=-1)


def _paired_halves(x, axis):
    even = jnp.zeros_like(x)
    return jnp.where((jnp.arange(x.shape[1]) % 2 == 0).reshape((1, -1) + (1,) * (x.ndim - 2)),
                     jnp.concatenate([x, even], axis=axis), jnp.concatenate([even, x], axis=axis))


def _stacked_weights(p, sizes, tdec, dbatch):
    pw, ql, kl, rp, gw, dm, nope, vh, d_ff = sizes
    depth = p["w_in"].shape[0]
    w_in = p["w_in"]
    o = 0
    cols = {}
    for name, width in (("a", pw), ("cq", ql), ("ckv", kl), ("kr", rp), ("u", gw), ("v", gw), ("g", 3 * dm)):
        cols[name] = w_in[:, :, o:o + width]
        o += width
    kr_block = jnp.concatenate([cols["kr"], _rotate_half_cols(cols["kr"]),
                                jnp.zeros((depth, dm, LANES - 2 * rp), F32)], axis=2)
    sw = {}
    sw["w_in_main"] = jnp.concatenate([cols["a"], cols["cq"], cols["ckv"], cols["u"], cols["v"], kr_block],
                                      axis=2).astype(BF)
    sw["w_gates"] = cols["g"].astype(BF)
    for k in ("q_norm_g", "kv_norm_g", "v_norm_g", "v_norm_b", "pool_scale", "ln1_g", "ln1_b", "ln2_g", "ln2_b",
              "conv_b"):
        sw[k] = p[k].reshape(depth, 1, -1)
    w_uq = p["w_uq"].reshape(depth, ql, N_HEADS, nope + rp)
    q_nope = w_uq[..., :nope].reshape(depth, ql, N_HEADS * nope)
    q_pe = w_uq[..., nope:]
    sw["w_q"] = jnp.concatenate([q_nope, q_pe.reshape(depth, ql, N_HEADS * rp),
                                 _rotate_half_cols(q_pe).reshape(depth, ql, N_HEADS * rp)], axis=2).astype(BF)
    sw["w_q_t"] = jnp.swapaxes(sw["w_q"], 1, 2)
    sw["w_kv_heads"] = jnp.concatenate([p["w_uk"].reshape(depth, kl, N_HEADS * nope),
                                        p["w_uv"].reshape(depth, kl, N_HEADS * vh)], axis=2).astype(BF)
    sw["w_uk_pad"] = _paired_halves(jnp.transpose(p["w_uk"], (0, 2, 3, 1)), axis=2).astype(BF)
    sw["w_uv_pad"] = _paired_halves(jnp.transpose(p["w_uv"], (0, 2, 1, 3)), axis=3).astype(BF)
    pool_w = p["pool_w"]
    n_pg, pgw = pool_w.shape[1], pool_w.shape[2]
    sw["pool_bd"] = jnp.einsum("gh,dgij->dgihj", jnp.eye(n_pg, dtype=F32),
                               pool_w).reshape(depth, n_pg * pgw, n_pg * pgw).astype(BF)
    for k in ("w_pool_out", "w_mla_out", "w_gmlp_out", "w_out", "w_up", "w_down"):
        sw[k] = p[k].astype(BF)
    sw["conv_w"] = p["conv_w"]
    w_s = p["w_spatial"]
    b_s = p["b_spatial"]
    n_gg = w_s.shape[1]
    gc = gw // n_gg
    sw["ws"] = jnp.tril(w_s).astype(BF)
    sw["bs"] = jnp.repeat(jnp.swapaxes(b_s, 1, 2), gc, axis=2)
    ws_dec = w_s[:, :, :tdec, :tdec]
    diags = [jnp.pad(jnp.diagonal(ws_dec, offset=-k, axis1=2, axis2=3), ((0, 0), (0, 0), (k, 0)))
             for k in range(tdec)]
    sw["ws_coef"] = jnp.repeat(jnp.swapaxes(jnp.stack(diags, axis=1), 2, 3), gc, axis=3)
    sw["bs_dec"] = jnp.repeat(jnp.swapaxes(b_s[:, :, :tdec], 1, 2), gc, axis=2)
    return sw


def _layer_weights(l, sw, sizes, depth):
    pw, ql, kl, rp, gw, dm, nope, vh, d_ff = sizes
    lw = {k: _LayerView(v, l) for k, v in sw.items()}
    lw.update({"dims": (pw, ql, kl, rp, gw), "nope": nope, "scale": float((nope + rp) ** -0.5),
               "alpha": float((2.0 * depth) ** 0.25)})
    return lw


def _inv_count(pos, pw):
    gwidth = pw // len(POOL_WINDOWS)
    win = jnp.repeat(jnp.asarray(POOL_WINDOWS, jnp.int32), gwidth)
    cnt = jnp.minimum(pos[:, None] + 1, win[None, :]).astype(F32)
    return 1.0 / cnt


def kernel(x_prompt, x_sample, cache_ckv, cache_kpe, state_pool, state_ffn, page_table, w_in, pool_w, pool_scale,
           w_pool_out, q_norm_g, w_uq, kv_norm_g, w_uk, w_uv, w_mla_out, v_norm_g, v_norm_b, w_spatial, b_spatial,
           w_gmlp_out, w_out, ln1_g, ln1_b, w_up, conv_w, conv_b, w_down, ln2_g, ln2_b):
    p = dict(w_in=w_in, pool_w=pool_w, pool_scale=pool_scale, w_pool_out=w_pool_out, q_norm_g=q_norm_g, w_uq=w_uq,
             kv_norm_g=kv_norm_g, w_uk=w_uk, w_uv=w_uv, w_mla_out=w_mla_out, v_norm_g=v_norm_g, v_norm_b=v_norm_b,
             w_spatial=w_spatial, b_spatial=b_spatial, w_gmlp_out=w_gmlp_out, w_out=w_out, ln1_g=ln1_g,
             ln1_b=ln1_b, w_up=w_up, conv_w=conv_w, conv_b=conv_b, w_down=w_down, ln2_g=ln2_g, ln2_b=ln2_b)
    batch, seq, dm = x_prompt.shape
    dbatch, tdec, _ = x_sample.shape
    depth = w_in.shape[0]
    ps = cache_ckv.shape[2]
    n_pages = page_table.shape[1]
    past = n_pages * ps
    kl, rp = cache_ckv.shape[3], cache_kpe.shape[3]
    pw = state_pool.shape[3]
    ql = q_norm_g.shape[1]
    gw = v_norm_g.shape[1]
    nope, vh = w_uk.shape[3], w_uv.shape[3]
    d_ff = w_down.shape[1]
    sizes = (pw, ql, kl, rp, gw, dm, nope, vh, d_ff)
    assert tdec == SUBLANES and pw == 2 * LANES and kl % LANES == 0 and N_HEADS * rp == 2 * LANES
    assert nope * 2 == LANES and vh * 2 == LANES and d_ff % FF_CHUNK == 0

    tm = min(512, seq)
    tq = min(512, seq)
    tm_big = min(2 * tm, seq)
    rows_s = dbatch * tdec
    n_grp = next(g for g in (16, 8, 4, 2, 1) if n_pages % (DECODE_SLOTS * g) == 0)
    n_split = min(2, n_grp)
    assert seq % tm == 0 and tq == tm
    cache_kpe_t = jnp.swapaxes(cache_kpe, 2, 3)

    pos_p = jnp.arange(seq, dtype=jnp.int32)
    pos_s = past + jnp.arange(tdec, dtype=jnp.int32)

    def tables(pos, reps):
        c, s = _rope_tables(pos, rp // 2)
        zeros = jnp.zeros((pos.shape[0], LANES - rp), F32)
        ck, sk = jnp.concatenate([c, zeros], 1), jnp.concatenate([s, zeros], 1)
        cq, sq = jnp.tile(c, (1, N_HEADS)), jnp.tile(s, (1, N_HEADS))
        ic = _inv_count(pos, pw)
        return [jnp.tile(t, (reps, 1)) for t in (ck, sk, cq, sq, ic)]

    cos_kp, sin_kp, cos_qp, sin_qp, icnt_p = tables(pos_p, 1)
    cos_qpt, sin_qpt = cos_qp.T, sin_qp.T
    cos_ks, sin_ks, cos_qs, sin_qs, icnt_s = tables(pos_s, dbatch)

    xp = x_prompt.reshape(batch * seq, dm)
    xs = x_sample.reshape(rows_s, dm)
    outs = [[] for _ in range(9)]
    sw = _stacked_weights(p, sizes, tdec, dbatch)
    for l in range(depth):
        lw = _layer_weights(l, sw, sizes, depth)
        a_in, _, cqt, ckvn, kpe, kh, vt, u, vn = _in_proj(xp, lw, cos_kp, sin_kp, tm=tm, vn_dtype=BF)
        yb_pre = _prompt_attention(cqt, kh, vt, cos_qpt, sin_qpt, lw, batch=batch, seq=seq, tq=tq)
        xp = _merge_prompt(xp, a_in, icnt_p, u, vn, yb_pre, lw, tm=tm_big, seq=seq, n_parts=tm_big // tm)
        xp, tail_p = _ffn(xp, None, lw, tm=tm_big, seq=seq, n_seq=batch, n_parts=tm_big // tm)
        outs[0].append(ckvn.reshape(batch * seq // ps, ps, kl))
        outs[1].append(kpe.reshape(batch * seq // ps, ps, rp))
        outs[4].append(a_in.reshape(batch, seq, pw)[:, seq - POOL_KEEP:])
        outs[6].append(tail_p)
        a_s, cqn_s, _, ckvn_s, kpe_s, _, _, u_s, vn_s = _in_proj(xs, lw, cos_ks, sin_ks, tm=rows_s, vn_dtype=F32)
        qlat_s, qpe_s = _q_proj(cqn_s, lw, cos_qs, sin_qs)
        yb_s = _sample_attention(qlat_s, qpe_s, ckvn_s, kpe_s, cache_ckv, cache_kpe_t, page_table, l, lw,
                                 dbatch=dbatch, tdec=tdec, n_grp=n_grp, n_split=n_split)
        pool_ext = jnp.concatenate([jnp.zeros((dbatch, 1, pw), F32), state_pool[l],
                                    a_s.reshape(dbatch, tdec, pw)], axis=1)
        xs = _merge_sample(xs, pool_ext, icnt_s, u_s, vn_s, yb_s, lw, tdec=tdec)
        prefix = jnp.pad(state_ffn[l], ((0, 0), (SUBLANES - (CONV_W - 1), 0), (0, 0))).reshape(rows_s, 2 * d_ff)
        xs, tail_s = _ffn(xs, prefix, lw, tm=rows_s, seq=tdec, n_seq=dbatch)
        outs[2].append(ckvn_s.reshape(dbatch, tdec, kl))
        outs[3].append(kpe_s.reshape(dbatch, tdec, rp))
        outs[5].append(pool_ext[:, -POOL_KEEP:])
        outs[7].append(tail_s)
        outs[8].append(vn_s.reshape(dbatch, tdec, gw))
    stacked = [jnp.stack(o) for o in outs]
    new_ckv_p, new_kpe_p, new_ckv_s, new_kpe_s, pool_p, pool_s, ffn_p, ffn_s, gv_s = stacked
    return (xp.reshape(batch, seq, dm), xs.reshape(dbatch, tdec, dm), new_ckv_p, new_kpe_p, new_ckv_s, new_kpe_s,
            pool_p, pool_s, ffn_p, ffn_s, gv_s)
```

```python
import functools

import jax
import jax.numpy as jnp
import numpy as np
from jax import lax
from jax.experimental import pallas as pl
from jax.experimental.pallas import tpu as pltpu

BF = jnp.bfloat16
F32 = jnp.float32

POOL_WINDOWS = (2, 4, 8, 16)
POOL_KEEP = max(POOL_WINDOWS) - 1
N_HEADS = 8
ROPE_THETA = 10000.0
RMS_EPS = 1e-6
LN_EPS = 1e-5
CONV_W = 3
SUBLANES = 8
LANES = 128
FF_CHUNK = 256
NEG_BIG = -1e30
SCORE_LOOKAHEAD = 3
DECODE_SLOTS = 4
DECODE_AHEAD = 3
VMEM_LIMIT = 56 * 1024 * 1024


def _params(*sem):
    return pltpu.CompilerParams(dimension_semantics=sem, vmem_limit_bytes=VMEM_LIMIT)


class _LayerView:
    def __init__(self, stacked, layer):
        self.stacked, self.layer = stacked, layer

    @property
    def shape(self):
        return self.stacked.shape[1:]


def _const_spec(x):
    if isinstance(x, _LayerView):
        nd, layer = len(x.shape), x.layer
        return pl.BlockSpec((None,) + tuple(x.shape), lambda *_: (layer,) + (0,) * nd,
                            pipeline_mode=pl.Buffered(1))
    shape = tuple(getattr(x, "shape", x))
    nd = len(shape)
    return pl.BlockSpec(shape, lambda *_: (0,) * nd, pipeline_mode=pl.Buffered(1))


def _operands(args):
    return [a.stacked if isinstance(a, _LayerView) else a for a in args]


def _rmsnorm(x, g):
    return x * lax.rsqrt(jnp.mean(x * x, axis=-1, keepdims=True) + RMS_EPS) * g


def _layernorm(x, g, b):
    mu = jnp.mean(x, axis=-1, keepdims=True)
    xc = x - mu
    var = jnp.mean(xc * xc, axis=-1, keepdims=True)
    return xc * lax.rsqrt(var + LN_EPS) * g + b


def _dot(a, b):
    return jnp.dot(a, b, preferred_element_type=F32)


def _dot_nt(a, b):
    return lax.dot_general(a, b, (((1,), (1,)), ((), ())), preferred_element_type=F32)


def _shift_rows(y3, prev3, k):
    t = lax.broadcasted_iota(jnp.int32, y3.shape, 1)
    return jnp.where(t >= k, pltpu.roll(y3, k, 1), pltpu.roll(prev3, k, 1))


def _prev_groups(y3, first):
    if y3.shape[0] == 1:
        return first
    return jnp.concatenate([first, y3[:-1]], axis=0)


def _in_proj_body(x_ref, w_ref, wkv_ref, qg_ref, kvg_ref, vg_ref, vb_ref, cos_ref, sin_ref,
                  a_ref, cq_ref, cqt_ref, ckv_ref, kpe_ref, kh_ref, vt_ref, u_ref, vn_ref, *, dims):
    pw, ql, kl, rp, gw = dims
    hn = wkv_ref.shape[1] // 2
    tm = x_ref.shape[0]
    n_part = 2 if tm % (2 * LANES) == 0 else 1
    part = tm // n_part
    zs = [_dot(x_ref[i * part:(i + 1) * part, :].astype(BF), w_ref[...]) for i in range(n_part)]
    for i, z in enumerate(zs):
        rows = slice(i * part, (i + 1) * part)
        o = 0
        a_ref[rows, :] = z[:, o:o + pw]
        o += pw
        cqn = _rmsnorm(z[:, o:o + ql], qg_ref[...])
        cq_ref[rows, :] = cqn.astype(BF)
        cqt_ref[0, :, rows] = cqn.T.astype(BF)
        o += ql
        ckvn = _rmsnorm(z[:, o:o + kl], kvg_ref[...])
        ckv_ref[rows, :] = ckvn
        o += kl
        u_ref[rows, :] = z[:, o:o + gw].astype(BF)
        o += gw
        vn_ref[rows, :] = _layernorm(z[:, o:o + gw], vg_ref[...], vb_ref[...]).astype(vn_ref.dtype)
        o += gw
        kr = z[:, o:o + LANES]
        kpe = kr * cos_ref[rows, :] + pltpu.roll(kr, LANES - rp, 1) * sin_ref[rows, :]
        kpe_ref[rows, :] = kpe[:, :rp]
        kvh = _dot(ckvn.astype(BF), wkv_ref[...])
        kpe_b = kpe.astype(BF)
        for j in range(hn // LANES):
            kh_ref[j, rows, 0:LANES] = kvh[:, j * LANES:(j + 1) * LANES].astype(BF)
            kh_ref[j, rows, LANES:2 * LANES] = kpe_b
        vt_ref[0, :, rows] = kvh[:, hn:].T.astype(BF)


def _in_proj(x2d, lw, cos_k, sin_k, *, tm, vn_dtype):
    rows, dm = x2d.shape
    dims = lw["dims"]
    pw, ql, kl, rp, gw = dims
    ntab = cos_k.shape[0] // tm
    weights = [lw["w_in_main"], lw["w_kv_heads"], lw["q_norm_g"], lw["kv_norm_g"], lw["v_norm_g"], lw["v_norm_b"]]
    hn = lw["w_kv_heads"].shape[1] // 2
    row_spec = lambda w: pl.BlockSpec((tm, w), lambda i: (i, 0))
    tab_spec = pl.BlockSpec((tm, LANES), lambda i: (i % ntab, 0))
    return pl.pallas_call(
        functools.partial(_in_proj_body, dims=dims),
        grid=(rows // tm,),
        in_specs=[row_spec(dm)] + [_const_spec(w) for w in weights] + [tab_spec, tab_spec],
        out_specs=[row_spec(pw), row_spec(ql), pl.BlockSpec((1, ql, tm), lambda i: (i, 0, 0)), row_spec(kl),
                   row_spec(rp), pl.BlockSpec((hn // LANES, tm, 2 * LANES), lambda i: (0, i, 0)),
                   pl.BlockSpec((1, hn, tm), lambda i: (i, 0, 0)), row_spec(gw), row_spec(gw)],
        out_shape=[jax.ShapeDtypeStruct((rows, pw), F32), jax.ShapeDtypeStruct((rows, ql), BF),
                   jax.ShapeDtypeStruct((rows // tm, ql, tm), BF),
                   jax.ShapeDtypeStruct((rows, kl), F32), jax.ShapeDtypeStruct((rows, rp), F32),
                   jax.ShapeDtypeStruct((hn // LANES, rows, 2 * LANES), BF),
                   jax.ShapeDtypeStruct((rows // tm, hn, tm), BF),
                   jax.ShapeDtypeStruct((rows, gw), BF), jax.ShapeDtypeStruct((rows, gw), vn_dtype)],
        compiler_params=_params("arbitrary"),
        name="in_proj",
    )(x2d, *_operands(weights), cos_k, sin_k)


def _q_heads(cq, w_ref, wuk_ref, cos, sin, nope, rp, scale):
    hn = N_HEADS * nope
    hr = N_HEADS * rp
    q = _dot(cq, w_ref[...])
    qpe = (q[:, hn:hn + hr] * cos + q[:, hn + hr:hn + 2 * hr] * sin) * scale
    qlat = [_dot(q[:, (h // 2) * LANES:(h // 2 + 1) * LANES].astype(BF), wuk_ref[h]) * scale
            for h in range(N_HEADS)]
    return qlat, qpe


def _q_body(cq_ref, w_ref, wuk_ref, cos_ref, sin_ref, qlat_ref, qpe_ref, *, nope, rp, scale):
    qlat, qpe = _q_heads(cq_ref[...], w_ref, wuk_ref, cos_ref[...], sin_ref[...], nope, rp, scale)
    qpe_ref[...] = qpe
    for h in range(N_HEADS):
        qlat_ref[h] = qlat[h]


def _q_proj(cq, lw, cos_q, sin_q):
    rows, ql = cq.shape
    pw, _, kl, rp, gw = lw["dims"]
    hr = N_HEADS * rp
    args = [cq, lw["w_q"], lw["w_uk_pad"], cos_q, sin_q]
    return pl.pallas_call(
        functools.partial(_q_body, nope=lw["nope"], rp=rp, scale=lw["scale"]),
        grid=(1,),
        in_specs=[_const_spec(a) for a in args],
        out_specs=[_const_spec((N_HEADS, rows, kl)), _const_spec((rows, hr))],
        out_shape=[jax.ShapeDtypeStruct((N_HEADS, rows, kl), F32), jax.ShapeDtypeStruct((rows, hr), F32)],
        compiler_params=_params("arbitrary"),
        name="q_proj",
    )(*_operands(args))


def _heads_out(o, wuv_ref, rows_per_head, full_m):
    outs = []
    ob = o.astype(BF)
    for j in range(N_HEADS // 2):
        acc = None
        for h in (2 * j, 2 * j + 1):
            sl = slice(h * rows_per_head, (h + 1) * rows_per_head)
            if full_m:
                y = _dot(ob, wuv_ref[h])[sl]
            else:
                y = _dot(ob[sl], wuv_ref[h])
            acc = y if acc is None else acc + y
        outs.append(acc)
    return outs


def _attn_body(cqt_ref, cos_ref, sin_ref, wqt_ref, kh_ref, vt_ref, o_ref,
               q_ref, m_ref, l_ref, acc_ref, *, tq, nope, rp, vh, scale):
    qi = pl.program_id(1)
    hn = N_HEADS * nope
    hr = N_HEADS * rp
    m_ref[...] = jnp.full(m_ref.shape, NEG_BIG, F32)
    l_ref[...] = jnp.zeros(l_ref.shape, F32)
    acc_ref[...] = jnp.zeros(acc_ref.shape, F32)
    qt = _dot(wqt_ref[...], cqt_ref[0])
    qpe_t = ((qt[hn:hn + hr] * cos_ref[...] + qt[hn + hr:hn + 2 * hr] * sin_ref[...]) * scale).astype(BF)
    q_ref[:, LANES + rp:, :] = jnp.zeros((N_HEADS, q_ref.shape[1] - LANES - rp, tq), BF)
    half_of_row = lax.broadcasted_iota(jnp.int32, (LANES, tq), 0) // nope
    for h in range(N_HEADS):
        pair = qt[(h // 2) * LANES:(h // 2 + 1) * LANES] * scale
        q_ref[h, 0:LANES, :] = jnp.where(half_of_row == h % 2, pair, 0.0).astype(BF)
        q_ref[h, LANES:LANES + rp, :] = qpe_t[h * rp:(h + 1) * rp]

    def step(masked, ki):
        half = tq // 2
        parts = [(0, half, half), (half, tq, tq)] if masked and half % LANES == 0 else [(0, tq, tq)]

        def scores(h):
            out = []
            for q0, q1, nk in parts:
                s = _dot(kh_ref[h // 2, ki, 0:nk, :], q_ref[h, :, q0:q1])
                if masked:
                    kpos = lax.broadcasted_iota(jnp.int32, s.shape, 0)
                    qpos = lax.broadcasted_iota(jnp.int32, s.shape, 1) + q0
                    s = jnp.where(kpos <= qpos, s, NEG_BIG)
                out.append(s)
            return out

        pending = [scores(h) for h in range(SCORE_LOOKAHEAD)]
        for h in range(N_HEADS):
            s_parts = pending.pop(0)
            if h + SCORE_LOOKAHEAD < N_HEADS:
                pending.append(scores(h + SCORE_LOOKAHEAD))
            for (q0, q1, nk), s in zip(parts, s_parts):
                m_prev = m_ref[h, :, q0:q1]
                m_new = jnp.maximum(m_prev, jnp.max(s, axis=0, keepdims=True))
                alpha = jnp.exp(m_prev - m_new)
                p = jnp.exp(s - m_new)
                l_ref[h, :, q0:q1] = alpha * l_ref[h, :, q0:q1] + jnp.sum(p, axis=0, keepdims=True)
                acc_ref[h, :, q0:q1] = (alpha * acc_ref[h, :, q0:q1]
                                        + _dot(vt_ref[ki, h * vh:(h + 1) * vh, 0:nk], p.astype(BF)))
                m_ref[h, :, q0:q1] = m_new

    def unmasked(ki, carry):
        step(False, ki)
        return carry

    step(True, qi)
    lax.fori_loop(0, qi, unmasked, 0)
    out_t = jnp.concatenate([acc_ref[h] * (1.0 / l_ref[h]) for h in range(N_HEADS)], axis=0)
    o_ref[...] = out_t.T.astype(o_ref.dtype)


def _prompt_attention(cqt, kh, vt, cos_qt, sin_qt, lw, *, batch, seq, tq):
    _, ql, kl, rp, _ = lw["dims"]
    n_pair, _, width = kh.shape
    hr = N_HEADS * rp
    nq = seq // tq
    hv = vt.shape[1]
    vh = hv // N_HEADS
    assert vt.shape == (batch * nq, hv, tq) and cqt.shape == (batch * nq, ql, tq)
    tab_spec = pl.BlockSpec((hr, tq), lambda b, i: (0, i))
    return pl.pallas_call(
        functools.partial(_attn_body, tq=tq, nope=lw["nope"], rp=rp, vh=vh, scale=lw["scale"]),
        grid=(batch, nq),
        in_specs=[pl.BlockSpec((1, ql, tq), lambda b, i: (b * nq + i, 0, 0)), tab_spec, tab_spec,
                  _const_spec(lw["w_q_t"]),
                  pl.BlockSpec((n_pair, nq, tq, width), lambda b, i: (0, b, 0, 0)),
                  pl.BlockSpec((nq, hv, tq), lambda b, i: (b, 0, 0))],
        out_specs=pl.BlockSpec((tq, hv), lambda b, i: (b * nq + i, 0)),
        out_shape=jax.ShapeDtypeStruct((batch * seq, hv), BF),
        scratch_shapes=[pltpu.VMEM((N_HEADS, width, tq), BF), pltpu.VMEM((N_HEADS, 1, tq), F32),
                        pltpu.VMEM((N_HEADS, 1, tq), F32), pltpu.VMEM((N_HEADS, vh, tq), F32)],
        compiler_params=_params("arbitrary", "arbitrary"),
        name="prompt_attention",
    )(*_operands([cqt, cos_qt, sin_qt, lw["w_q_t"], kh.reshape(n_pair, batch * nq, tq, width), vt]))


def _decode_body(pt_ref, qlat_ref, qpe_ref, ckvn_ref, kpen_ref, wuv_ref, ckv_hbm, kpe_hbm, o_ref,
                 kbuf, pbuf, sem_k, sem_p, *, layer, n_grp, n_groups, n_split, tdec, rp, kl):
    b = pl.program_id(0)
    rows = N_HEADS * tdec

    def page_copies(sample, group, slot, table=True):
        copies = []
        for j in range(n_grp):
            pid = pt_ref[sample, group * n_grp + j] if table else 0
            copies.append(pltpu.make_async_copy(ckv_hbm.at[layer, pid], kbuf.at[slot, j], sem_k.at[slot]))
            copies.append(pltpu.make_async_copy(kpe_hbm.at[layer, pid], pbuf.at[slot, j], sem_p.at[slot]))
        return copies

    def start_ahead(g):
        nxt = g + DECODE_AHEAD
        slot = nxt % DECODE_SLOTS
        if nxt < n_groups:
            for c in page_copies(b, nxt, slot):
                c.start()
        else:
            @pl.when(b + 1 < pl.num_programs(0))
            def _():
                for c in page_copies(b + 1, nxt - n_groups, slot):
                    c.start()

    @pl.when(b == 0)
    def _():
        for g in range(DECODE_AHEAD):
            for c in page_copies(0, g, g % DECODE_SLOTS):
                c.start()

    q = qlat_ref[...].reshape(rows, kl).astype(BF)
    qpe = qpe_ref[...]
    qp = jnp.concatenate([qpe[:, h * rp:(h + 1) * rp] for h in range(N_HEADS)], axis=0).astype(BF)

    def update(state, s, keys):
        m_prev, l_prev, acc = state
        m_new = jnp.maximum(m_prev, jnp.max(s, axis=-1, keepdims=True))
        alpha = jnp.exp(m_prev - m_new)
        p = jnp.exp(s - m_new)
        l_new = alpha * l_prev + jnp.sum(p, axis=-1, keepdims=True)
        acc = alpha * acc
        off = 0
        pb = p.astype(BF)
        for k in keys:
            acc = acc + _dot(pb[:, off:off + k.shape[0]], k)
            off += k.shape[0]
        return m_new, l_new, acc

    states = [(jnp.full((rows, 1), NEG_BIG, F32), jnp.zeros((rows, 1), F32), jnp.zeros((rows, kl), F32))
              for _ in range(n_split)]
    per = n_grp // n_split
    for g in range(n_groups):
        slot = g % DECODE_SLOTS
        start_ahead(g)
        for c in page_copies(b, g, slot, table=False):
            c.wait()
        pages = [kbuf[slot, j] for j in range(n_grp)]
        keys = [k.astype(BF) for k in pages]
        scores = [_dot(q, k.T.astype(BF)) + _dot(qp, pbuf[slot, j].astype(BF)) for j, k in enumerate(pages)]
        for i in range(n_split):
            states[i] = update(states[i], jnp.concatenate(scores[i * per:(i + 1) * per], axis=1),
                               keys[i * per:(i + 1) * per])

    pad = 2 * tdec
    kn = jnp.concatenate([ckvn_ref[...], jnp.zeros((pad - tdec, kl), F32)], axis=0).astype(BF)
    kpn = jnp.concatenate([kpen_ref[...], jnp.zeros((pad - tdec, rp), F32)], axis=0).astype(BF)
    sn = _dot_nt(q, kn) + _dot_nt(qp, kpn)
    qpos = lax.broadcasted_iota(jnp.int32, (N_HEADS, tdec, pad), 1).reshape(rows, pad)
    kpos = lax.broadcasted_iota(jnp.int32, (rows, pad), 1)
    states[0] = update(states[0], jnp.where(kpos <= qpos, sn, NEG_BIG), [kn])
    m_all = states[0][0]
    for m_i, _, _ in states[1:]:
        m_all = jnp.maximum(m_all, m_i)
    l_all = jnp.zeros_like(m_all)
    o = jnp.zeros((rows, kl), F32)
    for m_i, l_i, acc_i in states:
        w = jnp.exp(m_i - m_all)
        l_all = l_all + w * l_i
        o = o + w * acc_i
    o = o * (1.0 / l_all)
    for j, y in enumerate(_heads_out(o, wuv_ref, tdec, full_m=True)):
        o_ref[:, j * LANES:(j + 1) * LANES] = y.astype(o_ref.dtype)


def _sample_attention(qlat, qpe, ckvn, kpen, cache_ckv, cache_kpe_t, page_table, layer, lw, *, dbatch, tdec, n_grp,
                      n_split):
    _, _, kl, rp, _ = lw["dims"]
    n_pages = page_table.shape[1]
    ps = cache_ckv.shape[2]
    hv = lw["w_uv_pad"].shape[2] * N_HEADS // 2
    hr = N_HEADS * rp
    rows = N_HEADS * tdec

    n_groups = n_pages // n_grp
    assert n_groups % DECODE_SLOTS == 0
    in_specs = [pl.BlockSpec((N_HEADS, tdec, kl), lambda b, pt: (0, b, 0)),
                pl.BlockSpec((tdec, hr), lambda b, pt: (b, 0)),
                pl.BlockSpec((tdec, kl), lambda b, pt: (b, 0)),
                pl.BlockSpec((tdec, rp), lambda b, pt: (b, 0)),
                _const_spec(lw["w_uv_pad"]),
                pl.BlockSpec(memory_space=pl.ANY), pl.BlockSpec(memory_space=pl.ANY)]
    grid_spec = pltpu.PrefetchScalarGridSpec(
        num_scalar_prefetch=1, grid=(dbatch,), in_specs=in_specs,
        out_specs=pl.BlockSpec((tdec, hv), lambda b, pt: (b, 0)),
        scratch_shapes=[pltpu.VMEM((DECODE_SLOTS, n_grp, ps, kl), F32), pltpu.VMEM((DECODE_SLOTS, n_grp, rp, ps), F32),
                        pltpu.SemaphoreType.DMA((DECODE_SLOTS,)), pltpu.SemaphoreType.DMA((DECODE_SLOTS,))])
    return pl.pallas_call(
        functools.partial(_decode_body, layer=layer, n_grp=n_grp, n_groups=n_groups, n_split=n_split, tdec=tdec,
                          rp=rp, kl=kl),
        grid_spec=grid_spec,
        out_shape=jax.ShapeDtypeStruct((dbatch * tdec, hv), F32),
        compiler_params=_params("arbitrary"),
        name="sample_attention",
    )(*_operands([page_table, qlat, qpe, ckvn, kpen, lw["w_uv_pad"], cache_ckv, cache_kpe_t]))


def _window_select(sums, shape):
    gw = shape[-1] // len(POOL_WINDOWS)
    grp = lax.broadcasted_iota(jnp.int32, shape, len(shape) - 1) // gw
    out = sums[-1]
    for gi in range(len(POOL_WINDOWS) - 2, -1, -1):
        out = jnp.where(grp == gi, sums[gi], out)
    return out


def _spatial_matmul(vn, ws_ref, bs_ref):
    n_g, clen, _ = ws_ref.shape
    gc = vn.shape[1] // n_g
    grp = lax.broadcasted_iota(jnp.int32, (clen, vn.shape[1]), 1) // gc
    parts = []
    for c in range(vn.shape[0] // clen):
        vc = vn[c * clen:(c + 1) * clen]
        s = _dot(ws_ref[n_g - 1], vc)
        for g in range(n_g - 2, -1, -1):
            s = jnp.where(grp == g, _dot(ws_ref[g], vc), s)
        parts.append(s + bs_ref[...])
    return parts[0] if len(parts) == 1 else jnp.concatenate(parts, axis=0)


def _merge_tail(x, d, u, s, yb_pre, wg_ref, pbd_ref, psc_ref, wpo_ref, wgo_ref, wmo_ref,
                wo_ref, lng_ref, lnb_ref, alpha):
    dm = x.shape[1]
    xb = x.astype(BF)
    ya = _dot(d.astype(BF), pbd_ref[...]) * psc_ref[...]
    ya = _dot(ya.astype(BF), wpo_ref[...])
    m = jax.nn.sigmoid(_dot(xb, wg_ref[:, 0:dm])) * ya
    yb = _dot(yb_pre.astype(BF), wmo_ref[...])
    m = m + jax.nn.sigmoid(_dot(xb, wg_ref[:, dm:2 * dm])) * yb
    yc = _dot((u.astype(F32) * s).astype(BF), wgo_ref[...])
    m = m + jax.nn.sigmoid(_dot(xb, wg_ref[:, 2 * dm:3 * dm])) * yc
    y = alpha * x + _dot(m.astype(BF), wo_ref[...])
    return _layernorm(y, lng_ref[...], lnb_ref[...])


def _merge_prompt_body(x_ref, a_ref, aprev_ref, icnt_ref, u_ref, vn_ref, yb_ref, *rest, tiles_per_seq, n_parts,
                       alpha):
    ws_ref, bs_ref = rest[:2]
    w_refs, o_ref = rest[2:-1], rest[-1]
    i = pl.program_id(0)
    a = a_ref[...]
    tm, pw = a.shape
    hist = jnp.where(i % tiles_per_seq == 0, 0.0, aprev_ref[...])
    n_hist = hist.shape[0] // SUBLANES
    ext = jnp.concatenate([hist, a], axis=0).reshape(tm // SUBLANES + n_hist, SUBLANES, pw)
    zero = jnp.zeros((1, SUBLANES, pw), F32)
    s2 = ext + _shift_rows(ext, _prev_groups(ext, zero), 1)
    s4 = s2 + _shift_rows(s2, _prev_groups(s2, zero), 2)
    s8 = s4 + _shift_rows(s4, _prev_groups(s4, zero), 4)
    s16 = s8 + _prev_groups(s8, zero)
    sel = _window_select([s[n_hist:] for s in (s2, s4, s8, s16)], (tm // SUBLANES, SUBLANES, pw))
    d = sel.reshape(tm, pw) * icnt_ref[...] - a
    part = tm // n_parts
    for j in range(n_parts):
        rows = slice(j * part, (j + 1) * part)
        o_ref[rows, :] = _merge_tail(x_ref[rows, :], d[rows], u_ref[rows, :],
                                     _spatial_matmul(vn_ref[rows, :], ws_ref, bs_ref), yb_ref[rows, :],
                                     *w_refs, alpha)


def _merge_sample_body(x_ref, ext_ref, icnt_ref, u_ref, vn_ref, yb_ref, coef_ref, bs_ref, *rest, tdec, alpha):
    w_refs, o_ref = rest[:-1], rest[-1]
    nb, ext_len, pw = ext_ref.shape
    acc = None
    sums = []
    for j in range(max(POOL_WINDOWS)):
        cur = ext_ref[:, ext_len - tdec - j:ext_len - j, :]
        acc = cur if acc is None else acc + cur
        if j + 1 in POOL_WINDOWS:
            sums.append(acc)
    tok = ext_ref[:, ext_len - tdec:ext_len, :]
    sel = _window_select(sums, (nb, tdec, pw))
    d = sel.reshape(nb * tdec, pw) * icnt_ref[...] - tok.reshape(nb * tdec, pw)
    vn3 = vn_ref[...].reshape(nb, tdec, vn_ref.shape[1])
    s = coef_ref[0] * vn3 + bs_ref[...]
    for k in range(1, tdec):
        s = s + coef_ref[k] * pltpu.roll(vn3, k, 1)
    o_ref[...] = _merge_tail(x_ref[...], d, u_ref[...], s.reshape(nb * tdec, vn_ref.shape[1]), yb_ref[...],
                             *w_refs, alpha)


def _merge_weights(lw):
    return [lw["w_gates"], lw["pool_bd"], lw["pool_scale"], lw["w_pool_out"],
            lw["w_gmlp_out"], lw["w_mla_out"], lw["w_out"], lw["ln1_g"], lw["ln1_b"]]


def _merge_prompt(x2d, a_in, icnt, u, vn, yb_pre, lw, *, tm, seq, n_parts):
    rows, dm = x2d.shape
    pw, _, _, _, gw = lw["dims"]
    tiles_per_seq = seq // tm
    hist_rows = 2 * SUBLANES
    hist_per_tile = tm // hist_rows
    weights = [lw["ws"], lw["bs"]] + _merge_weights(lw)
    row_spec = lambda w: pl.BlockSpec((tm, w), lambda i: (i, 0))
    in_specs = [row_spec(dm), row_spec(pw),
                pl.BlockSpec((hist_rows, pw), lambda i: (jnp.maximum(i * hist_per_tile - 1, 0), 0)),
                pl.BlockSpec((tm, pw), lambda i: (i % tiles_per_seq, 0)),
                row_spec(gw), row_spec(gw), row_spec(yb_pre.shape[1])]
    in_specs += [_const_spec(w) for w in weights]
    return pl.pallas_call(
        functools.partial(_merge_prompt_body, tiles_per_seq=tiles_per_seq, n_parts=n_parts, alpha=lw["alpha"]),
        grid=(rows // tm,), in_specs=in_specs, out_specs=row_spec(dm),
        out_shape=jax.ShapeDtypeStruct((rows, dm), F32),
        compiler_params=_params("arbitrary"),
        name="merge_prompt",
    )(x2d, a_in, a_in, icnt, u, vn, yb_pre, *_operands(weights))


def _merge_sample(x2d, ext, icnt, u, vn, yb_pre, lw, *, tdec):
    rows, dm = x2d.shape
    weights = [lw["ws_coef"], lw["bs_dec"]] + _merge_weights(lw)
    args = [x2d, ext, icnt, u, vn, yb_pre] + weights
    return pl.pallas_call(
        functools.partial(_merge_sample_body, tdec=tdec, alpha=lw["alpha"]),
        grid=(1,), in_specs=[_const_spec(a) for a in args], out_specs=_const_spec((rows, dm)),
        out_shape=jax.ShapeDtypeStruct((rows, dm), F32),
        compiler_params=_params("arbitrary"),
        name="merge_sample",
    )(*_operands(args))


def _conv_gate(cur, back1, back2, w, b):
    half = cur.shape[-1] // 2
    conv = (b + w[0:1] * back2 + w[1:2] * back1 + w[2:3] * cur).reshape(cur.shape[0] * SUBLANES, 2 * half)
    return jax.nn.silu(conv[:, :half]) * conv[:, half:]


def _shift_rows_ext(ext, k):
    r = pltpu.roll(ext, k, 1)
    t = lax.broadcasted_iota(jnp.int32, r[1:].shape, 1)
    return jnp.where(t >= k, r[1:], r[:-1])


def _ff_cols(ref_or_val, c, d_ff):
    lo = c * FF_CHUNK
    return jnp.concatenate([ref_or_val[:, lo:lo + FF_CHUNK], ref_or_val[:, d_ff + lo:d_ff + lo + FF_CHUNK]],
                           axis=-1)


def _ff_store_tail(tail, c, d_ff, idx, val):
    lo = c * FF_CHUNK
    tail[idx + (slice(lo, lo + FF_CHUNK),)] = val[..., :FF_CHUNK]
    tail[idx + (slice(d_ff + lo, d_ff + lo + FF_CHUNK),)] = val[..., FF_CHUNK:]


def _ffn_sample_body(x_ref, prefix_ref, wup_ref, cw_ref, cb_ref, wdn_ref, lng_ref, lnb_ref, o_ref, tail_ref,
                     h_ref, *, alpha):
    d_ff = wdn_ref.shape[0]
    cw2 = 2 * FF_CHUNK
    tm = x_ref.shape[0]
    grp = tm // SUBLANES
    x = x_ref[...]
    xb = x.astype(BF)
    for c in range(d_ff // FF_CHUNK):
        a3 = _dot(xb, _ff_cols(wup_ref, c, d_ff)).reshape(grp, SUBLANES, cw2)
        prev = _ff_cols(prefix_ref, c, d_ff).reshape(grp, SUBLANES, cw2)
        h = _conv_gate(a3, _shift_rows(a3, prev, 1), _shift_rows(a3, prev, 2), _ff_cols(cw_ref, c, d_ff),
                       _ff_cols(cb_ref, c, d_ff))
        h_ref[:, c * FF_CHUNK:(c + 1) * FF_CHUNK] = h.astype(BF)
        _ff_store_tail(tail_ref, c, d_ff, (slice(None), slice(None)), a3[:, SUBLANES - (CONV_W - 1):, :])
    o_ref[...] = _layernorm(alpha * x + _dot(h_ref[...], wdn_ref[...]), lng_ref[...], lnb_ref[...])


def _ffn_prompt_body(x_ref, wup_ref, cw_ref, cb_ref, wdn_ref, lng_ref, lnb_ref, o_ref, tail_ref,
                     h_ref, carry_ref, *, tiles_per_seq, n_parts, alpha):
    d_ff = wdn_ref.shape[0]
    n_chunks = d_ff // FF_CHUNK
    cw2 = 2 * FF_CHUNK
    part = x_ref.shape[0] // n_parts

    @pl.when(pl.program_id(0) % tiles_per_seq == 0)
    def _():
        carry_ref[...] = jnp.zeros(carry_ref.shape, F32)

    for j in range(n_parts):
        rows = slice(j * part, (j + 1) * part)
        x = x_ref[rows, :]
        xb = x.astype(BF)

        def up(c):
            return _dot(xb, _ff_cols(wup_ref, c, d_ff))

        a_next = up(0)
        for c in range(n_chunks):
            a = a_next
            if c + 1 < n_chunks:
                a_next = up(c + 1)
            ext = jnp.concatenate([carry_ref[c], a], axis=0).reshape(part // SUBLANES + 1, SUBLANES, cw2)
            h = _conv_gate(ext[1:], _shift_rows_ext(ext, 1), _shift_rows_ext(ext, 2), _ff_cols(cw_ref, c, d_ff),
                           _ff_cols(cb_ref, c, d_ff))
            h_ref[j, :, c * FF_CHUNK:(c + 1) * FF_CHUNK] = h.astype(BF)
            last = a[part - SUBLANES:]
            carry_ref[c] = last
            if j == n_parts - 1:
                _ff_store_tail(tail_ref, c, d_ff, (0, slice(None)), last[SUBLANES - (CONV_W - 1):, :])
        o_ref[rows, :] = _layernorm(alpha * x + _dot(h_ref[j], wdn_ref[...]), lng_ref[...], lnb_ref[...])


def _ffn(x2d, prefix, lw, *, tm, seq, n_seq, n_parts=1):
    rows, dm = x2d.shape
    wup, cw, cb, wdn = lw["w_up"], lw["conv_w"], lw["conv_b"], lw["w_down"]
    d_ff = wdn.shape[0]
    n_chunks = d_ff // FF_CHUNK
    cw2 = 2 * FF_CHUNK
    has_prefix = prefix is not None
    tiles_per_seq = max(seq // tm, 1)
    keep = CONV_W - 1
    in_specs = [pl.BlockSpec((tm, dm), lambda i: (i, 0))]
    args = [x2d]
    if has_prefix:
        in_specs.append(_const_spec(prefix.shape))
        args.append(prefix)
        tail_shape = (rows // SUBLANES, keep, 2 * d_ff)
        tail_spec = _const_spec(tail_shape)
    else:
        tail_shape = (n_seq, keep, 2 * d_ff)
        tail_spec = pl.BlockSpec((1, keep, 2 * d_ff), lambda i: (i // tiles_per_seq, 0, 0))
    weights = [wup, cw, cb, wdn, lw["ln2_g"], lw["ln2_b"]]
    in_specs += [_const_spec(w) for w in weights]
    if has_prefix:
        body = functools.partial(_ffn_sample_body, alpha=lw["alpha"])
        scratch = [pltpu.VMEM((tm, d_ff), BF)]
    else:
        body = functools.partial(_ffn_prompt_body, tiles_per_seq=tiles_per_seq, n_parts=n_parts, alpha=lw["alpha"])
        scratch = [pltpu.VMEM((n_parts, tm // n_parts, d_ff), BF), pltpu.VMEM((n_chunks, SUBLANES, cw2), F32)]
    return pl.pallas_call(
        body, grid=(rows // tm,), in_specs=in_specs,
        out_specs=[pl.BlockSpec((tm, dm), lambda i: (i, 0)), tail_spec],
        out_shape=[jax.ShapeDtypeStruct((rows, dm), F32), jax.ShapeDtypeStruct(tail_shape, F32)],
        scratch_shapes=scratch,
        compiler_params=_params("arbitrary"),
        name="ffn_sample" if has_prefix else "ffn_prompt",
    )(*args, *_operands(weights))


def _rope_tables(pos, half):
    inv = ROPE_THETA ** (-jnp.arange(half, dtype=F32) / half)
    ang = pos.astype(F32)[:, None] * inv[None, :]
    cos, sin = jnp.cos(ang), jnp.sin(ang)
    return jnp.concatenate([cos, cos], axis=-1), jnp.concatenate([sin, sin], axis=-1)


def _rotate_half_cols(w):
    half = w.shape[-1] // 2
    return jnp.concatenate([-w[..., half:], w[..., :half]], axis=-1)


def _paired_halves(x, axis):
    even = jnp.zeros_like(x)
    return jnp.where((jnp.arange(x.shape[1]) % 2 == 0).reshape((1, -1) + (1,) * (x.ndim - 2)),
                     jnp.concatenate([x, even], axis=axis), jnp.concatenate([even, x], axis=axis))


def _stacked_weights(p, sizes, tdec, dbatch):
    pw, ql, kl, rp, gw, dm, nope, vh, d_ff = sizes
    depth = p["w_in"].shape[0]
    w_in = p["w_in"]
    o = 0
    cols = {}
    for name, width in (("a", pw), ("cq", ql), ("ckv", kl), ("kr", rp), ("u", gw), ("v", gw), ("g", 3 * dm)):
        cols[name] = w_in[:, :, o:o + width]
        o += width
    kr_block = jnp.concatenate([cols["kr"], _rotate_half_cols(cols["kr"]),
                                jnp.zeros((depth, dm, LANES - 2 * rp), F32)], axis=2)
    sw = {}
    sw["w_in_main"] = jnp.concatenate([cols["a"], cols["cq"], cols["ckv"], cols["u"], cols["v"], kr_block],
                                      axis=2).astype(BF)
    sw["w_gates"] = cols["g"].astype(BF)
    for k in ("q_norm_g", "kv_norm_g", "v_norm_g", "v_norm_b", "pool_scale", "ln1_g", "ln1_b", "ln2_g", "ln2_b",
              "conv_b"):
        sw[k] = p[k].reshape(depth, 1, -1)
    w_uq = p["w_uq"].reshape(depth, ql, N_HEADS, nope + rp)
    q_nope = w_uq[..., :nope].reshape(depth, ql, N_HEADS * nope)
    q_pe = w_uq[..., nope:]
    sw["w_q"] = jnp.concatenate([q_nope, q_pe.reshape(depth, ql, N_HEADS * rp),
                                 _rotate_half_cols(q_pe).reshape(depth, ql, N_HEADS * rp)], axis=2).astype(BF)
    sw["w_q_t"] = jnp.swapaxes(sw["w_q"], 1, 2)
    sw["w_kv_heads"] = jnp.concatenate([p["w_uk"].reshape(depth, kl, N_HEADS * nope),
                                        p["w_uv"].reshape(depth, kl, N_HEADS * vh)], axis=2).astype(BF)
    sw["w_uk_pad"] = _paired_halves(jnp.transpose(p["w_uk"], (0, 2, 3, 1)), axis=2).astype(BF)
    sw["w_uv_pad"] = _paired_halves(jnp.transpose(p["w_uv"], (0, 2, 1, 3)), axis=3).astype(BF)
    pool_w = p["pool_w"]
    n_pg, pgw = pool_w.shape[1], pool_w.shape[2]
    sw["pool_bd"] = jnp.einsum("gh,dgij->dgihj", jnp.eye(n_pg, dtype=F32),
                               pool_w).reshape(depth, n_pg * pgw, n_pg * pgw).astype(BF)
    for k in ("w_pool_out", "w_mla_out", "w_gmlp_out", "w_out", "w_up", "w_down"):
        sw[k] = p[k].astype(BF)
    sw["conv_w"] = p["conv_w"]
    w_s = p["w_spatial"]
    b_s = p["b_spatial"]
    n_gg = w_s.shape[1]
    gc = gw // n_gg
    sw["ws"] = jnp.tril(w_s).astype(BF)
    sw["bs"] = jnp.repeat(jnp.swapaxes(b_s, 1, 2), gc, axis=2)
    ws_dec = w_s[:, :, :tdec, :tdec]
    diags = [jnp.pad(jnp.diagonal(ws_dec, offset=-k, axis1=2, axis2=3), ((0, 0), (0, 0), (k, 0)))
             for k in range(tdec)]
    sw["ws_coef"] = jnp.repeat(jnp.swapaxes(jnp.stack(diags, axis=1), 2, 3), gc, axis=3)
    sw["bs_dec"] = jnp.repeat(jnp.swapaxes(b_s[:, :, :tdec], 1, 2), gc, axis=2)
    return sw


def _layer_weights(l, sw, sizes, depth):
    pw, ql, kl, rp, gw, dm, nope, vh, d_ff = sizes
    lw = {k: _LayerView(v, l) for k, v in sw.items()}
    lw.update({"dims": (pw, ql, kl, rp, gw), "nope": nope, "scale": float((nope + rp) ** -0.5),
               "alpha": float((2.0 * depth) ** 0.25)})
    return lw


def _inv_count(pos, pw):
    gwidth = pw // len(POOL_WINDOWS)
    win = jnp.repeat(jnp.asarray(POOL_WINDOWS, jnp.int32), gwidth)
    cnt = jnp.minimum(pos[:, None] + 1, win[None, :]).astype(F32)
    return 1.0 / cnt


def kernel(x_prompt, x_sample, cache_ckv, cache_kpe, state_pool, state_ffn, page_table, w_in, pool_w, pool_scale,
           w_pool_out, q_norm_g, w_uq, kv_norm_g, w_uk, w_uv, w_mla_out, v_norm_g, v_norm_b, w_spatial, b_spatial,
           w_gmlp_out, w_out, ln1_g, ln1_b, w_up, conv_w, conv_b, w_down, ln2_g, ln2_b):
    p = dict(w_in=w_in, pool_w=pool_w, pool_scale=pool_scale, w_pool_out=w_pool_out, q_norm_g=q_norm_g, w_uq=w_uq,
             kv_norm_g=kv_norm_g, w_uk=w_uk, w_uv=w_uv, w_mla_out=w_mla_out, v_norm_g=v_norm_g, v_norm_b=v_norm_b,
             w_spatial=w_spatial, b_spatial=b_spatial, w_gmlp_out=w_gmlp_out, w_out=w_out, ln1_g=ln1_g,
             ln1_b=ln1_b, w_up=w_up, conv_w=conv_w, conv_b=conv_b, w_down=w_down, ln2_g=ln2_g, ln2_b=ln2_b)
    batch, seq, dm = x_prompt.shape
    dbatch, tdec, _ = x_sample.shape
    depth = w_in.shape[0]
    ps = cache_ckv.shape[2]
    n_pages = page_table.shape[1]
    past = n_pages * ps
    kl, rp = cache_ckv.shape[3], cache_kpe.shape[3]
    pw = state_pool.shape[3]
    ql = q_norm_g.shape[1]
    gw = v_norm_g.shape[1]
    nope, vh = w_uk.shape[3], w_uv.shape[3]
    d_ff = w_down.shape[1]
    sizes = (pw, ql, kl, rp, gw, dm, nope, vh, d_ff)
    assert tdec == SUBLANES and pw == 2 * LANES and kl % LANES == 0 and N_HEADS * rp == 2 * LANES
    assert nope * 2 == LANES and vh * 2 == LANES and d_ff % FF_CHUNK == 0

    tm = min(512, seq)
    tq = min(512, seq)
    tm_big = min(2 * tm, seq)
    rows_s = dbatch * tdec
    n_grp = next(g for g in (16, 8, 4, 2, 1) if n_pages % (DECODE_SLOTS * g) == 0)
    n_split = min(2, n_grp)
    assert seq % tm == 0 and tq == tm
    cache_kpe_t = jnp.swapaxes(cache_kpe, 2, 3)

    pos_p = jnp.arange(seq, dtype=jnp.int32)
    pos_s = past + jnp.arange(tdec, dtype=jnp.int32)

    def tables(pos, reps):
        c, s = _rope_tables(pos, rp // 2)
        zeros = jnp.zeros((pos.shape[0], LANES - rp), F32)
        ck, sk = jnp.concatenate([c, zeros], 1), jnp.concatenate([s, zeros], 1)
        cq, sq = jnp.tile(c, (1, N_HEADS)), jnp.tile(s, (1, N_HEADS))
        ic = _inv_count(pos, pw)
        return [jnp.tile(t, (reps, 1)) for t in (ck, sk, cq, sq, ic)]

    cos_kp, sin_kp, cos_qp, sin_qp, icnt_p = tables(pos_p, 1)
    cos_qpt, sin_qpt = cos_qp.T, sin_qp.T
    cos_ks, sin_ks, cos_qs, sin_qs, icnt_s = tables(pos_s, dbatch)

    xp = x_prompt.reshape(batch * seq, dm)
    xs = x_sample.reshape(rows_s, dm)
    outs = [[] for _ in range(9)]
    sw = _stacked_weights(p, sizes, tdec, dbatch)
    for l in range(depth):
        lw = _layer_weights(l, sw, sizes, depth)
        a_in, _, cqt, ckvn, kpe, kh, vt, u, vn = _in_proj(xp, lw, cos_kp, sin_kp, tm=tm, vn_dtype=BF)
        yb_pre = _prompt_attention(cqt, kh, vt, cos_qpt, sin_qpt, lw, batch=batch, seq=seq, tq=tq)
        xp = _merge_prompt(xp, a_in, icnt_p, u, vn, yb_pre, lw, tm=tm, seq=seq, n_parts=1)
        xp, tail_p = _ffn(xp, None, lw, tm=tm_big, seq=seq, n_seq=batch, n_parts=tm_big // tm)
        outs[0].append(ckvn.reshape(batch * seq // ps, ps, kl))
        outs[1].append(kpe.reshape(batch * seq // ps, ps, rp))
        outs[4].append(a_in.reshape(batch, seq, pw)[:, seq - POOL_KEEP:])
        outs[6].append(tail_p)
        a_s, cqn_s, _, ckvn_s, kpe_s, _, _, u_s, vn_s = _in_proj(xs, lw, cos_ks, sin_ks, tm=rows_s, vn_dtype=F32)
        qlat_s, qpe_s = _q_proj(cqn_s, lw, cos_qs, sin_qs)
        yb_s = _sample_attention(qlat_s, qpe_s, ckvn_s, kpe_s, cache_ckv, cache_kpe_t, page_table, l, lw,
                                 dbatch=dbatch, tdec=tdec, n_grp=n_grp, n_split=n_split)
        pool_ext = jnp.concatenate([jnp.zeros((dbatch, 1, pw), F32), state_pool[l],
                                    a_s.reshape(dbatch, tdec, pw)], axis=1)
        xs = _merge_sample(xs, pool_ext, icnt_s, u_s, vn_s, yb_s, lw, tdec=tdec)
        prefix = jnp.pad(state_ffn[l], ((0, 0), (SUBLANES - (CONV_W - 1), 0), (0, 0))).reshape(rows_s, 2 * d_ff)
        xs, tail_s = _ffn(xs, prefix, lw, tm=rows_s, seq=tdec, n_seq=dbatch)
        outs[2].append(ckvn_s.reshape(dbatch, tdec, kl))
        outs[3].append(kpe_s.reshape(dbatch, tdec, rp))
        outs[5].append(pool_ext[:, -POOL_KEEP:])
        outs[7].append(tail_s)
        outs[8].append(vn_s.reshape(dbatch, tdec, gw))
    stacked = [jnp.stack(o) for o in outs]
    new_ckv_p, new_kpe_p, new_ckv_s, new_kpe_s, pool_p, pool_s, ffn_p, ffn_s, gv_s = stacked
    return (xp.reshape(batch, seq, dm), xs.reshape(dbatch, tdec, dm), new_ckv_p, new_kpe_p, new_ckv_s, new_kpe_s,
            pool_p, pool_s, ffn_p, ffn_s, gv_s)
```

```python
import functools

import jax
import jax.numpy as jnp
import numpy as np
from jax import lax
from jax.experimental import pallas as pl
from jax.experimental.pallas import tpu as pltpu

BF = jnp.bfloat16
F32 = jnp.float32

POOL_WINDOWS = (2, 4, 8, 16)
POOL_KEEP = max(POOL_WINDOWS) - 1
N_HEADS = 8
ROPE_THETA = 10000.0
RMS_EPS = 1e-6
LN_EPS = 1e-5
CONV_W = 3
SUBLANES = 8
LANES = 128
FF_CHUNK = 256
NEG_BIG = -1e30
LOG2_E = 1.4426950408889634
SCORE_LOOKAHEAD = 8
DECODE_SLOTS = 4
DECODE_AHEAD = 3
VMEM_LIMIT = 56 * 1024 * 1024


def _params(*sem):
    return pltpu.CompilerParams(dimension_semantics=sem, vmem_limit_bytes=VMEM_LIMIT)


class _LayerView:
    def __init__(self, stacked, layer):
        self.stacked, self.layer = stacked, layer

    @property
    def shape(self):
        return self.stacked.shape[1:]


def _const_spec(x):
    if isinstance(x, _LayerView):
        nd, layer = len(x.shape), x.layer
        return pl.BlockSpec((None,) + tuple(x.shape), lambda *_: (layer,) + (0,) * nd,
                            pipeline_mode=pl.Buffered(1))
    shape = tuple(getattr(x, "shape", x))
    nd = len(shape)
    return pl.BlockSpec(shape, lambda *_: (0,) * nd, pipeline_mode=pl.Buffered(1))


def _operands(args):
    return [a.stacked if isinstance(a, _LayerView) else a for a in args]


def _rmsnorm(x, g):
    return x * lax.rsqrt(jnp.mean(x * x, axis=-1, keepdims=True) + RMS_EPS) * g


def _layernorm(x, g, b):
    mu = jnp.mean(x, axis=-1, keepdims=True)
    xc = x - mu
    var = jnp.mean(xc * xc, axis=-1, keepdims=True)
    return xc * lax.rsqrt(var + LN_EPS) * g + b


def _dot(a, b):
    return jnp.dot(a, b, preferred_element_type=F32)


def _dot_nt(a, b):
    return lax.dot_general(a, b, (((1,), (1,)), ((), ())), preferred_element_type=F32)


def _shift_rows(y3, prev3, k):
    t = lax.broadcasted_iota(jnp.int32, y3.shape, 1)
    return jnp.where(t >= k, pltpu.roll(y3, k, 1), pltpu.roll(prev3, k, 1))


def _prev_groups(y3, first):
    if y3.shape[0] == 1:
        return first
    return jnp.concatenate([first, y3[:-1]], axis=0)


def _in_proj_body(x_ref, w_ref, wkv_ref, qg_ref, kvg_ref, vg_ref, vb_ref, cos_ref, sin_ref,
                  a_ref, cq_ref, cqt_ref, ckv_ref, kpe_ref, kh_ref, vt_ref, u_ref, vn_ref, *, dims):
    pw, ql, kl, rp, gw = dims
    hn = wkv_ref.shape[1] // 2
    tm = x_ref.shape[0]
    n_part = 2 if tm % (2 * LANES) == 0 else 1
    part = tm // n_part
    zs = [_dot(x_ref[i * part:(i + 1) * part, :].astype(BF), w_ref[...]) for i in range(n_part)]
    for i, z in enumerate(zs):
        rows = slice(i * part, (i + 1) * part)
        o = 0
        a_ref[rows, :] = z[:, o:o + pw]
        o += pw
        cqn = _rmsnorm(z[:, o:o + ql], qg_ref[...])
        cq_ref[rows, :] = cqn.astype(BF)
        cqt_ref[0, :, rows] = cqn.T.astype(BF)
        o += ql
        ckvn = _rmsnorm(z[:, o:o + kl], kvg_ref[...])
        ckv_ref[rows, :] = ckvn
        o += kl
        u_ref[rows, :] = z[:, o:o + gw].astype(BF)
        o += gw
        vn_ref[rows, :] = _layernorm(z[:, o:o + gw], vg_ref[...], vb_ref[...]).astype(vn_ref.dtype)
        o += gw
        kr = z[:, o:o + LANES]
        kpe = kr * cos_ref[rows, :] + pltpu.roll(kr, LANES - rp, 1) * sin_ref[rows, :]
        kpe_ref[rows, :] = kpe[:, :rp]
        kvh = _dot(ckvn.astype(BF), wkv_ref[...])
        kpe_b = kpe.astype(BF)
        for j in range(hn // LANES):
            kh_ref[j, rows, 0:LANES] = kvh[:, j * LANES:(j + 1) * LANES].astype(BF)
            kh_ref[j, rows, LANES:2 * LANES] = kpe_b
        vt_ref[0, :, rows] = kvh[:, hn:].T.astype(BF)


def _in_proj(x2d, lw, cos_k, sin_k, *, tm, vn_dtype):
    rows, dm = x2d.shape
    dims = lw["dims"]
    pw, ql, kl, rp, gw = dims
    ntab = cos_k.shape[0] // tm
    weights = [lw["w_in_main"], lw["w_kv_heads"], lw["q_norm_g"], lw["kv_norm_g"], lw["v_norm_g"], lw["v_norm_b"]]
    hn = lw["w_kv_heads"].shape[1] // 2
    row_spec = lambda w: pl.BlockSpec((tm, w), lambda i: (i, 0))
    tab_spec = pl.BlockSpec((tm, LANES), lambda i: (i % ntab, 0))
    return pl.pallas_call(
        functools.partial(_in_proj_body, dims=dims),
        grid=(rows // tm,),
        in_specs=[row_spec(dm)] + [_const_spec(w) for w in weights] + [tab_spec, tab_spec],
        out_specs=[row_spec(pw), row_spec(ql), pl.BlockSpec((1, ql, tm), lambda i: (i, 0, 0)), row_spec(kl),
                   row_spec(rp), pl.BlockSpec((hn // LANES, tm, 2 * LANES), lambda i: (0, i, 0)),
                   pl.BlockSpec((1, hn, tm), lambda i: (i, 0, 0)), row_spec(gw), row_spec(gw)],
        out_shape=[jax.ShapeDtypeStruct((rows, pw), F32), jax.ShapeDtypeStruct((rows, ql), BF),
                   jax.ShapeDtypeStruct((rows // tm, ql, tm), BF),
                   jax.ShapeDtypeStruct((rows, kl), F32), jax.ShapeDtypeStruct((rows, rp), F32),
                   jax.ShapeDtypeStruct((hn // LANES, rows, 2 * LANES), BF),
                   jax.ShapeDtypeStruct((rows // tm, hn, tm), BF),
                   jax.ShapeDtypeStruct((rows, gw), BF), jax.ShapeDtypeStruct((rows, gw), vn_dtype)],
        compiler_params=_params("arbitrary"),
        name="in_proj",
    )(x2d, *_operands(weights), cos_k, sin_k)


def _q_heads(cq, w_ref, wuk_ref, cos, sin, nope, rp, scale):
    hn = N_HEADS * nope
    hr = N_HEADS * rp
    q = _dot(cq, w_ref[...])
    qpe = (q[:, hn:hn + hr] * cos + q[:, hn + hr:hn + 2 * hr] * sin) * scale
    qlat = [_dot(q[:, (h // 2) * LANES:(h // 2 + 1) * LANES].astype(BF), wuk_ref[h]) * scale
            for h in range(N_HEADS)]
    return qlat, qpe


def _q_body(cq_ref, w_ref, wuk_ref, cos_ref, sin_ref, qlat_ref, qpe_ref, *, nope, rp, scale):
    qlat, qpe = _q_heads(cq_ref[...], w_ref, wuk_ref, cos_ref[...], sin_ref[...], nope, rp, scale)
    qpe_ref[...] = qpe
    for h in range(N_HEADS):
        qlat_ref[h] = qlat[h]


def _q_proj(cq, lw, cos_q, sin_q):
    rows, ql = cq.shape
    pw, _, kl, rp, gw = lw["dims"]
    hr = N_HEADS * rp
    args = [cq, lw["w_q"], lw["w_uk_pad"], cos_q, sin_q]
    return pl.pallas_call(
        functools.partial(_q_body, nope=lw["nope"], rp=rp, scale=lw["scale"]),
        grid=(1,),
        in_specs=[_const_spec(a) for a in args],
        out_specs=[_const_spec((N_HEADS, rows, kl)), _const_spec((rows, hr))],
        out_shape=[jax.ShapeDtypeStruct((N_HEADS, rows, kl), F32), jax.ShapeDtypeStruct((rows, hr), F32)],
        compiler_params=_params("arbitrary"),
        name="q_proj",
    )(*_operands(args))


def _heads_out(o, wuv_ref, rows_per_head, full_m):
    outs = []
    ob = o.astype(BF)
    for j in range(N_HEADS // 2):
        acc = None
        for h in (2 * j, 2 * j + 1):
            sl = slice(h * rows_per_head, (h + 1) * rows_per_head)
            if full_m:
                y = _dot(ob, wuv_ref[h])[sl]
            else:
                y = _dot(ob[sl], wuv_ref[h])
            acc = y if acc is None else acc + y
        outs.append(acc)
    return outs


def _attn_body(cqt_ref, cos_ref, sin_ref, wqt_ref, kh_ref, vt_ref, o_ref,
               q_ref, m_ref, l_ref, acc_ref, *, tq, nope, rp, vh, scale):
    qi = pl.program_id(1)
    hn = N_HEADS * nope
    hr = N_HEADS * rp
    scale = scale * LOG2_E
    m_ref[...] = jnp.full(m_ref.shape, NEG_BIG, F32)
    l_ref[...] = jnp.zeros(l_ref.shape, F32)
    acc_ref[...] = jnp.zeros(acc_ref.shape, F32)
    qt = _dot(wqt_ref[...], cqt_ref[0])
    qpe_t = ((qt[hn:hn + hr] * cos_ref[...] + qt[hn + hr:hn + 2 * hr] * sin_ref[...]) * scale).astype(BF)
    q_ref[:, LANES + rp:, :] = jnp.zeros((N_HEADS, q_ref.shape[1] - LANES - rp, tq), BF)
    half_of_row = lax.broadcasted_iota(jnp.int32, (LANES, tq), 0) // nope
    for h in range(N_HEADS):
        pair = qt[(h // 2) * LANES:(h // 2 + 1) * LANES] * scale
        q_ref[h, 0:LANES, :] = jnp.where(half_of_row == h % 2, pair, 0.0).astype(BF)
        q_ref[h, LANES:LANES + rp, :] = qpe_t[h * rp:(h + 1) * rp]

    def step(masked, ki):
        half = tq // 2
        parts = [(0, half, half), (half, tq, tq)] if masked and half % LANES == 0 else [(0, tq, tq)]

        def scores(h):
            out = []
            for q0, q1, nk in parts:
                s = _dot(kh_ref[h // 2, ki, 0:nk, :], q_ref[h, :, q0:q1])
                if masked:
                    kpos = lax.broadcasted_iota(jnp.int32, s.shape, 0)
                    qpos = lax.broadcasted_iota(jnp.int32, s.shape, 1) + q0
                    s = jnp.where(kpos <= qpos, s, NEG_BIG)
                out.append(s)
            return out

        pending = [scores(h) for h in range(SCORE_LOOKAHEAD)]
        for h in range(N_HEADS):
            s_parts = pending.pop(0)
            if h + SCORE_LOOKAHEAD < N_HEADS:
                pending.append(scores(h + SCORE_LOOKAHEAD))
            for (q0, q1, nk), s in zip(parts, s_parts):
                m_prev = m_ref[h, :, q0:q1]
                m_new = jnp.maximum(m_prev, jnp.max(s, axis=0, keepdims=True))
                alpha = jnp.exp2(m_prev - m_new)
                p = jnp.exp2(s - m_new)
                l_ref[h, :, q0:q1] = alpha * l_ref[h, :, q0:q1] + jnp.sum(p, axis=0, keepdims=True)
                acc_ref[h, :, q0:q1] = (alpha * acc_ref[h, :, q0:q1]
                                        + _dot(vt_ref[ki, h * vh:(h + 1) * vh, 0:nk], p.astype(BF)))
                m_ref[h, :, q0:q1] = m_new

    def unmasked(ki, carry):
        step(False, ki)
        return carry

    step(True, qi)
    lax.fori_loop(0, qi, unmasked, 0)
    out_t = jnp.concatenate([acc_ref[h] * (1.0 / l_ref[h]) for h in range(N_HEADS)], axis=0)
    o_ref[...] = out_t.T.astype(o_ref.dtype)


def _prompt_attention(cqt, kh, vt, cos_qt, sin_qt, lw, *, batch, seq, tq):
    _, ql, kl, rp, _ = lw["dims"]
    n_pair, _, width = kh.shape
    hr = N_HEADS * rp
    nq = seq // tq
    hv = vt.shape[1]
    vh = hv // N_HEADS
    assert vt.shape == (batch * nq, hv, tq) and cqt.shape == (batch * nq, ql, tq)
    tab_spec = pl.BlockSpec((hr, tq), lambda b, i: (0, i))
    return pl.pallas_call(
        functools.partial(_attn_body, tq=tq, nope=lw["nope"], rp=rp, vh=vh, scale=lw["scale"]),
        grid=(batch, nq),
        in_specs=[pl.BlockSpec((1, ql, tq), lambda b, i: (b * nq + i, 0, 0)), tab_spec, tab_spec,
                  _const_spec(lw["w_q_t"]),
                  pl.BlockSpec((n_pair, nq, tq, width), lambda b, i: (0, b, 0, 0)),
                  pl.BlockSpec((nq, hv, tq), lambda b, i: (b, 0, 0))],
        out_specs=pl.BlockSpec((tq, hv), lambda b, i: (b * nq + i, 0)),
        out_shape=jax.ShapeDtypeStruct((batch * seq, hv), BF),
        scratch_shapes=[pltpu.VMEM((N_HEADS, width, tq), BF), pltpu.VMEM((N_HEADS, 1, tq), F32),
                        pltpu.VMEM((N_HEADS, 1, tq), F32), pltpu.VMEM((N_HEADS, vh, tq), F32)],
        compiler_params=_params("arbitrary", "arbitrary"),
        name="prompt_attention",
    )(*_operands([cqt, cos_qt, sin_qt, lw["w_q_t"], kh.reshape(n_pair, batch * nq, tq, width), vt]))


def _decode_body(pt_ref, qlat_ref, qpe_ref, ckvn_ref, kpen_ref, wuv_ref, ckv_hbm, kpe_hbm, o_ref,
                 kbuf, pbuf, sem_k, sem_p, *, layer, n_grp, n_groups, n_split, tdec, rp, kl):
    b = pl.program_id(0)
    rows = N_HEADS * tdec

    def page_copies(sample, group, slot, table=True):
        copies = []
        for j in range(n_grp):
            pid = pt_ref[sample, group * n_grp + j] if table else 0
            copies.append(pltpu.make_async_copy(ckv_hbm.at[layer, pid], kbuf.at[slot, j], sem_k.at[slot]))
            copies.append(pltpu.make_async_copy(kpe_hbm.at[layer, pid], pbuf.at[slot, j], sem_p.at[slot]))
        return copies

    def start_ahead(g):
        nxt = g + DECODE_AHEAD
        slot = nxt % DECODE_SLOTS
        if nxt < n_groups:
            for c in page_copies(b, nxt, slot):
                c.start()
        else:
            @pl.when(b + 1 < pl.num_programs(0))
            def _():
                for c in page_copies(b + 1, nxt - n_groups, slot):
                    c.start()

    @pl.when(b == 0)
    def _():
        for g in range(DECODE_AHEAD):
            for c in page_copies(0, g, g % DECODE_SLOTS):
                c.start()

    q = qlat_ref[...].reshape(rows, kl).astype(BF)
    qpe = qpe_ref[...]
    qp = jnp.concatenate([qpe[:, h * rp:(h + 1) * rp] for h in range(N_HEADS)], axis=0).astype(BF)

    def update(state, s, keys):
        m_prev, l_prev, acc = state
        m_new = jnp.maximum(m_prev, jnp.max(s, axis=-1, keepdims=True))
        alpha = jnp.exp(m_prev - m_new)
        p = jnp.exp(s - m_new)
        l_new = alpha * l_prev + jnp.sum(p, axis=-1, keepdims=True)
        acc = alpha * acc
        off = 0
        pb = p.astype(BF)
        for k in keys:
            acc = acc + _dot(pb[:, off:off + k.shape[0]], k)
            off += k.shape[0]
        return m_new, l_new, acc

    states = [(jnp.full((rows, 1), NEG_BIG, F32), jnp.zeros((rows, 1), F32), jnp.zeros((rows, kl), F32))
              for _ in range(n_split)]
    per = n_grp // n_split
    for g in range(n_groups):
        slot = g % DECODE_SLOTS
        start_ahead(g)
        for c in page_copies(b, g, slot, table=False):
            c.wait()
        keys = [kbuf[slot, j].astype(BF) for j in range(n_grp)]
        scores = [_dot_nt(q, k) + _dot(qp, pbuf[slot, j].astype(BF)) for j, k in enumerate(keys)]
        for i in range(n_split):
            states[i] = update(states[i], jnp.concatenate(scores[i * per:(i + 1) * per], axis=1),
                               keys[i * per:(i + 1) * per])

    pad = 2 * tdec
    kn = jnp.concatenate([ckvn_ref[...], jnp.zeros((pad - tdec, kl), F32)], axis=0).astype(BF)
    kpn = jnp.concatenate([kpen_ref[...], jnp.zeros((pad - tdec, rp), F32)], axis=0).astype(BF)
    sn = _dot_nt(q, kn) + _dot_nt(qp, kpn)
    qpos = lax.broadcasted_iota(jnp.int32, (N_HEADS, tdec, pad), 1).reshape(rows, pad)
    kpos = lax.broadcasted_iota(jnp.int32, (rows, pad), 1)
    states[0] = update(states[0], jnp.where(kpos <= qpos, sn, NEG_BIG), [kn])
    m_all = states[0][0]
    for m_i, _, _ in states[1:]:
        m_all = jnp.maximum(m_all, m_i)
    l_all = jnp.zeros_like(m_all)
    o = jnp.zeros((rows, kl), F32)
    for m_i, l_i, acc_i in states:
        w = jnp.exp(m_i - m_all)
        l_all = l_all + w * l_i
        o = o + w * acc_i
    o = o * (1.0 / l_all)
    for j, y in enumerate(_heads_out(o, wuv_ref, tdec, full_m=True)):
        o_ref[:, j * LANES:(j + 1) * LANES] = y.astype(o_ref.dtype)


def _sample_attention(qlat, qpe, ckvn, kpen, cache_ckv, cache_kpe_t, page_table, layer, lw, *, dbatch, tdec, n_grp,
                      n_split):
    _, _, kl, rp, _ = lw["dims"]
    n_pages = page_table.shape[1]
    ps = cache_ckv.shape[2]
    hv = lw["w_uv_pad"].shape[2] * N_HEADS // 2
    hr = N_HEADS * rp
    rows = N_HEADS * tdec

    n_groups = n_pages // n_grp
    assert n_groups % DECODE_SLOTS == 0
    in_specs = [pl.BlockSpec((N_HEADS, tdec, kl), lambda b, pt: (0, b, 0)),
                pl.BlockSpec((tdec, hr), lambda b, pt: (b, 0)),
                pl.BlockSpec((tdec, kl), lambda b, pt: (b, 0)),
                pl.BlockSpec((tdec, rp), lambda b, pt: (b, 0)),
                _const_spec(lw["w_uv_pad"]),
                pl.BlockSpec(memory_space=pl.ANY), pl.BlockSpec(memory_space=pl.ANY)]
    grid_spec = pltpu.PrefetchScalarGridSpec(
        num_scalar_prefetch=1, grid=(dbatch,), in_specs=in_specs,
        out_specs=pl.BlockSpec((tdec, hv), lambda b, pt: (b, 0)),
        scratch_shapes=[pltpu.VMEM((DECODE_SLOTS, n_grp, ps, kl), F32), pltpu.VMEM((DECODE_SLOTS, n_grp, rp, ps), F32),
                        pltpu.SemaphoreType.DMA((DECODE_SLOTS,)), pltpu.SemaphoreType.DMA((DECODE_SLOTS,))])
    return pl.pallas_call(
        functools.partial(_decode_body, layer=layer, n_grp=n_grp, n_groups=n_groups, n_split=n_split, tdec=tdec,
                          rp=rp, kl=kl),
        grid_spec=grid_spec,
        out_shape=jax.ShapeDtypeStruct((dbatch * tdec, hv), F32),
        compiler_params=_params("arbitrary"),
        name="sample_attention",
    )(*_operands([page_table, qlat, qpe, ckvn, kpen, lw["w_uv_pad"], cache_ckv, cache_kpe_t]))


def _window_select(sums, shape):
    gw = shape[-1] // len(POOL_WINDOWS)
    grp = lax.broadcasted_iota(jnp.int32, shape, len(shape) - 1) // gw
    out = sums[-1]
    for gi in range(len(POOL_WINDOWS) - 2, -1, -1):
        out = jnp.where(grp == gi, sums[gi], out)
    return out


def _spatial_matmul(vn, ws_ref, bs_ref):
    n_g, clen, _ = ws_ref.shape
    gc = vn.shape[1] // n_g
    grp = lax.broadcasted_iota(jnp.int32, (clen, vn.shape[1]), 1) // gc
    parts = []
    for c in range(vn.shape[0] // clen):
        vc = vn[c * clen:(c + 1) * clen]
        s = _dot(ws_ref[n_g - 1], vc)
        for g in range(n_g - 2, -1, -1):
            s = jnp.where(grp == g, _dot(ws_ref[g], vc), s)
        parts.append(s + bs_ref[...])
    return parts[0] if len(parts) == 1 else jnp.concatenate(parts, axis=0)


def _merge_tail(x, d, u, s, yb_pre, wg_ref, pbd_ref, psc_ref, wpo_ref, wgo_ref, wmo_ref,
                wo_ref, lng_ref, lnb_ref, alpha):
    dm = x.shape[1]
    xb = x.astype(BF)
    ya = _dot(d.astype(BF), pbd_ref[...]) * psc_ref[...]
    ya = _dot(ya.astype(BF), wpo_ref[...])
    m = jax.nn.sigmoid(_dot(xb, wg_ref[:, 0:dm])) * ya
    yb = _dot(yb_pre.astype(BF), wmo_ref[...])
    m = m + jax.nn.sigmoid(_dot(xb, wg_ref[:, dm:2 * dm])) * yb
    yc = _dot((u.astype(F32) * s).astype(BF), wgo_ref[...])
    m = m + jax.nn.sigmoid(_dot(xb, wg_ref[:, 2 * dm:3 * dm])) * yc
    y = alpha * x + _dot(m.astype(BF), wo_ref[...])
    return _layernorm(y, lng_ref[...], lnb_ref[...])


def _merge_prompt_body(x_ref, a_ref, aprev_ref, icnt_ref, u_ref, vn_ref, yb_ref, *rest, tiles_per_seq, n_parts,
                       alpha):
    ws_ref, bs_ref = rest[:2]
    w_refs, o_ref = rest[2:-1], rest[-1]
    i = pl.program_id(0)
    a = a_ref[...]
    tm, pw = a.shape
    hist = jnp.where(i % tiles_per_seq == 0, 0.0, aprev_ref[...])
    n_hist = hist.shape[0] // SUBLANES
    ext = jnp.concatenate([hist, a], axis=0).reshape(tm // SUBLANES + n_hist, SUBLANES, pw)
    zero = jnp.zeros((1, SUBLANES, pw), F32)
    s2 = ext + _shift_rows(ext, _prev_groups(ext, zero), 1)
    s4 = s2 + _shift_rows(s2, _prev_groups(s2, zero), 2)
    s8 = s4 + _shift_rows(s4, _prev_groups(s4, zero), 4)
    s16 = s8 + _prev_groups(s8, zero)
    sel = _window_select([s[n_hist:] for s in (s2, s4, s8, s16)], (tm // SUBLANES, SUBLANES, pw))
    d = sel.reshape(tm, pw) * icnt_ref[...] - a
    part = tm // n_parts
    for j in range(n_parts):
        rows = slice(j * part, (j + 1) * part)
        o_ref[rows, :] = _merge_tail(x_ref[rows, :], d[rows], u_ref[rows, :],
                                     _spatial_matmul(vn_ref[rows, :], ws_ref, bs_ref), yb_ref[rows, :],
                                     *w_refs, alpha)


def _merge_sample_body(x_ref, ext_ref, icnt_ref, u_ref, vn_ref, yb_ref, coef_ref, bs_ref, *rest, tdec, alpha):
    w_refs, o_ref = rest[:-1], rest[-1]
    nb, ext_len, pw = ext_ref.shape
    acc = None
    sums = []
    for j in range(max(POOL_WINDOWS)):
        cur = ext_ref[:, ext_len - tdec - j:ext_len - j, :]
        acc = cur if acc is None else acc + cur
        if j + 1 in POOL_WINDOWS:
            sums.append(acc)
    tok = ext_ref[:, ext_len - tdec:ext_len, :]
    sel = _window_select(sums, (nb, tdec, pw))
    d = sel.reshape(nb * tdec, pw) * icnt_ref[...] - tok.reshape(nb * tdec, pw)
    vn3 = vn_ref[...].reshape(nb, tdec, vn_ref.shape[1])
    s = coef_ref[0] * vn3 + bs_ref[...]
    for k in range(1, tdec):
        s = s + coef_ref[k] * pltpu.roll(vn3, k, 1)
    o_ref[...] = _merge_tail(x_ref[...], d, u_ref[...], s.reshape(nb * tdec, vn_ref.shape[1]), yb_ref[...],
                             *w_refs, alpha)


def _merge_weights(lw):
    return [lw["w_gates"], lw["pool_bd"], lw["pool_scale"], lw["w_pool_out"],
            lw["w_gmlp_out"], lw["w_mla_out"], lw["w_out"], lw["ln1_g"], lw["ln1_b"]]


def _merge_prompt(x2d, a_in, icnt, u, vn, yb_pre, lw, *, tm, seq, n_parts):
    rows, dm = x2d.shape
    pw, _, _, _, gw = lw["dims"]
    tiles_per_seq = seq // tm
    hist_rows = 2 * SUBLANES
    hist_per_tile = tm // hist_rows
    weights = [lw["ws"], lw["bs"]] + _merge_weights(lw)
    row_spec = lambda w: pl.BlockSpec((tm, w), lambda i: (i, 0))
    in_specs = [row_spec(dm), row_spec(pw),
                pl.BlockSpec((hist_rows, pw), lambda i: (jnp.maximum(i * hist_per_tile - 1, 0), 0)),
                pl.BlockSpec((tm, pw), lambda i: (i % tiles_per_seq, 0)),
                row_spec(gw), row_spec(gw), row_spec(yb_pre.shape[1])]
    in_specs += [_const_spec(w) for w in weights]
    return pl.pallas_call(
        functools.partial(_merge_prompt_body, tiles_per_seq=tiles_per_seq, n_parts=n_parts, alpha=lw["alpha"]),
        grid=(rows // tm,), in_specs=in_specs, out_specs=row_spec(dm),
        out_shape=jax.ShapeDtypeStruct((rows, dm), F32),
        compiler_params=_params("arbitrary"),
        name="merge_prompt",
    )(x2d, a_in, a_in, icnt, u, vn, yb_pre, *_operands(weights))


def _merge_sample(x2d, ext, icnt, u, vn, yb_pre, lw, *, tdec):
    rows, dm = x2d.shape
    weights = [lw["ws_coef"], lw["bs_dec"]] + _merge_weights(lw)
    args = [x2d, ext, icnt, u, vn, yb_pre] + weights
    return pl.pallas_call(
        functools.partial(_merge_sample_body, tdec=tdec, alpha=lw["alpha"]),
        grid=(1,), in_specs=[_const_spec(a) for a in args], out_specs=_const_spec((rows, dm)),
        out_shape=jax.ShapeDtypeStruct((rows, dm), F32),
        compiler_params=_params("arbitrary"),
        name="merge_sample",
    )(*_operands(args))


def _conv_gate(cur, back1, back2, w, b):
    half = cur.shape[-1] // 2
    conv = (b + w[0:1] * back2 + w[1:2] * back1 + w[2:3] * cur).reshape(cur.shape[0] * SUBLANES, 2 * half)
    return jax.nn.silu(conv[:, :half]) * conv[:, half:]


def _shift_rows_ext(ext, k):
    r = pltpu.roll(ext, k, 1)
    t = lax.broadcasted_iota(jnp.int32, r[1:].shape, 1)
    return jnp.where(t >= k, r[1:], r[:-1])


def _ff_cols(ref_or_val, c, d_ff):
    lo = c * FF_CHUNK
    return jnp.concatenate([ref_or_val[:, lo:lo + FF_CHUNK], ref_or_val[:, d_ff + lo:d_ff + lo + FF_CHUNK]],
                           axis=-1)


def _ff_store_tail(tail, c, d_ff, idx, val):
    lo = c * FF_CHUNK
    tail[idx + (slice(lo, lo + FF_CHUNK),)] = val[..., :FF_CHUNK]
    tail[idx + (slice(d_ff + lo, d_ff + lo + FF_CHUNK),)] = val[..., FF_CHUNK:]


def _ffn_sample_body(x_ref, prefix_ref, wup_ref, cw_ref, cb_ref, wdn_ref, lng_ref, lnb_ref, o_ref, tail_ref,
                     h_ref, *, alpha):
    d_ff = wdn_ref.shape[0]
    cw2 = 2 * FF_CHUNK
    tm = x_ref.shape[0]
    grp = tm // SUBLANES
    x = x_ref[...]
    xb = x.astype(BF)
    for c in range(d_ff // FF_CHUNK):
        a3 = _dot(xb, _ff_cols(wup_ref, c, d_ff)).reshape(grp, SUBLANES, cw2)
        prev = _ff_cols(prefix_ref, c, d_ff).reshape(grp, SUBLANES, cw2)
        h = _conv_gate(a3, _shift_rows(a3, prev, 1), _shift_rows(a3, prev, 2), _ff_cols(cw_ref, c, d_ff),
                       _ff_cols(cb_ref, c, d_ff))
        h_ref[:, c * FF_CHUNK:(c + 1) * FF_CHUNK] = h.astype(BF)
        _ff_store_tail(tail_ref, c, d_ff, (slice(None), slice(None)), a3[:, SUBLANES - (CONV_W - 1):, :])
    o_ref[...] = _layernorm(alpha * x + _dot(h_ref[...], wdn_ref[...]), lng_ref[...], lnb_ref[...])


def _ffn_prompt_body(x_ref, wup_ref, cw_ref, cb_ref, wdn_ref, lng_ref, lnb_ref, o_ref, tail_ref,
                     h_ref, carry_ref, *, tiles_per_seq, n_parts, alpha):
    d_ff = wdn_ref.shape[0]
    n_chunks = d_ff // FF_CHUNK
    cw2 = 2 * FF_CHUNK
    part = x_ref.shape[0] // n_parts

    @pl.when(pl.program_id(0) % tiles_per_seq == 0)
    def _():
        carry_ref[...] = jnp.zeros(carry_ref.shape, F32)

    for j in range(n_parts):
        rows = slice(j * part, (j + 1) * part)
        x = x_ref[rows, :]
        xb = x.astype(BF)

        def up(c):
            return _dot(xb, _ff_cols(wup_ref, c, d_ff))

        a_next = up(0)
        for c in range(n_chunks):
            a = a_next
            if c + 1 < n_chunks:
                a_next = up(c + 1)
            ext = jnp.concatenate([carry_ref[c], a], axis=0).reshape(part // SUBLANES + 1, SUBLANES, cw2)
            h = _conv_gate(ext[1:], _shift_rows_ext(ext, 1), _shift_rows_ext(ext, 2), _ff_cols(cw_ref, c, d_ff),
                           _ff_cols(cb_ref, c, d_ff))
            h_ref[j, :, c * FF_CHUNK:(c + 1) * FF_CHUNK] = h.astype(BF)
            last = a[part - SUBLANES:]
            carry_ref[c] = last
            if j == n_parts - 1:
                _ff_store_tail(tail_ref, c, d_ff, (0, slice(None)), last[SUBLANES - (CONV_W - 1):, :])
        o_ref[rows, :] = _layernorm(alpha * x + _dot(h_ref[j], wdn_ref[...]), lng_ref[...], lnb_ref[...])


def _ffn(x2d, prefix, lw, *, tm, seq, n_seq, n_parts=1):
    rows, dm = x2d.shape
    wup, cw, cb, wdn = lw["w_up"], lw["conv_w"], lw["conv_b"], lw["w_down"]
    d_ff = wdn.shape[0]
    n_chunks = d_ff // FF_CHUNK
    cw2 = 2 * FF_CHUNK
    has_prefix = prefix is not None
    tiles_per_seq = max(seq // tm, 1)
    keep = CONV_W - 1
    in_specs = [pl.BlockSpec((tm, dm), lambda i: (i, 0))]
    args = [x2d]
    if has_prefix:
        in_specs.append(_const_spec(prefix.shape))
        args.append(prefix)
        tail_shape = (rows // SUBLANES, keep, 2 * d_ff)
        tail_spec = _const_spec(tail_shape)
    else:
        tail_shape = (n_seq, keep, 2 * d_ff)
        tail_spec = pl.BlockSpec((1, keep, 2 * d_ff), lambda i: (i // tiles_per_seq, 0, 0))
    weights = [wup, cw, cb, wdn, lw["ln2_g"], lw["ln2_b"]]
    in_specs += [_const_spec(w) for w in weights]
    if has_prefix:
        body = functools.partial(_ffn_sample_body, alpha=lw["alpha"])
        scratch = [pltpu.VMEM((tm, d_ff), BF)]
    else:
        body = functools.partial(_ffn_prompt_body, tiles_per_seq=tiles_per_seq, n_parts=n_parts, alpha=lw["alpha"])
        scratch = [pltpu.VMEM((n_parts, tm // n_parts, d_ff), BF), pltpu.VMEM((n_chunks, SUBLANES, cw2), F32)]
    return pl.pallas_call(
        body, grid=(rows // tm,), in_specs=in_specs,
        out_specs=[pl.BlockSpec((tm, dm), lambda i: (i, 0)), tail_spec],
        out_shape=[jax.ShapeDtypeStruct((rows, dm), F32), jax.ShapeDtypeStruct(tail_shape, F32)],
        scratch_shapes=scratch,
        compiler_params=_params("arbitrary"),
        name="ffn_sample" if has_prefix else "ffn_prompt",
    )(*args, *_operands(weights))


def _rope_tables(pos, half):
    inv = ROPE_THETA ** (-jnp.arange(half, dtype=F32) / half)
    ang = pos.astype(F32)[:, None] * inv[None, :]
    cos, sin = jnp.cos(ang), jnp.sin(ang)
    return jnp.concatenate([cos, cos], axis=-1), jnp.concatenate([sin, sin], axis=-1)


def _rotate_half_cols(w):
    half = w.shape[-1] // 2
    return jnp.concatenate([-w[..., half:], w[..., :half]], axis=-1)


def _paired_halves(x, axis):
    even = jnp.zeros_like(x)
    return jnp.where((jnp.arange(x.shape[1]) % 2 == 0).reshape((1, -1) + (1,) * (x.ndim - 2)),
                     jnp.concatenate([x, even], axis=axis), jnp.concatenate([even, x], axis=axis))


def _stacked_weights(p, sizes, tdec, dbatch):
    pw, ql, kl, rp, gw, dm, nope, vh, d_ff = sizes
    depth = p["w_in"].shape[0]
    w_in = p["w_in"]
    o = 0
    cols = {}
    for name, width in (("a", pw), ("cq", ql), ("ckv", kl), ("kr", rp), ("u", gw), ("v", gw), ("g", 3 * dm)):
        cols[name] = w_in[:, :, o:o + width]
        o += width
    kr_block = jnp.concatenate([cols["kr"], _rotate_half_cols(cols["kr"]),
                                jnp.zeros((depth, dm, LANES - 2 * rp), F32)], axis=2)
    sw = {}
    sw["w_in_main"] = jnp.concatenate([cols["a"], cols["cq"], cols["ckv"], cols["u"], cols["v"], kr_block],
                                      axis=2).astype(BF)
    sw["w_gates"] = cols["g"].astype(BF)
    for k in ("q_norm_g", "kv_norm_g", "v_norm_g", "v_norm_b", "pool_scale", "ln1_g", "ln1_b", "ln2_g", "ln2_b",
              "conv_b"):
        sw[k] = p[k].reshape(depth, 1, -1)
    w_uq = p["w_uq"].reshape(depth, ql, N_HEADS, nope + rp)
    q_nope = w_uq[..., :nope].reshape(depth, ql, N_HEADS * nope)
    q_pe = w_uq[..., nope:]
    sw["w_q"] = jnp.concatenate([q_nope, q_pe.reshape(depth, ql, N_HEADS * rp),
                                 _rotate_half_cols(q_pe).reshape(depth, ql, N_HEADS * rp)], axis=2).astype(BF)
    sw["w_q_t"] = jnp.swapaxes(sw["w_q"], 1, 2)
    sw["w_kv_heads"] = jnp.concatenate([p["w_uk"].reshape(depth, kl, N_HEADS * nope),
                                        p["w_uv"].reshape(depth, kl, N_HEADS * vh)], axis=2).astype(BF)
    sw["w_uk_pad"] = _paired_halves(jnp.transpose(p["w_uk"], (0, 2, 3, 1)), axis=2).astype(BF)
    sw["w_uv_pad"] = _paired_halves(jnp.transpose(p["w_uv"], (0, 2, 1, 3)), axis=3).astype(BF)
    pool_w = p["pool_w"]
    n_pg, pgw = pool_w.shape[1], pool_w.shape[2]
    sw["pool_bd"] = jnp.einsum("gh,dgij->dgihj", jnp.eye(n_pg, dtype=F32),
                               pool_w).reshape(depth, n_pg * pgw, n_pg * pgw).astype(BF)
    for k in ("w_pool_out", "w_mla_out", "w_gmlp_out", "w_out", "w_up", "w_down"):
        sw[k] = p[k].astype(BF)
    sw["conv_w"] = p["conv_w"]
    w_s = p["w_spatial"]
    b_s = p["b_spatial"]
    n_gg = w_s.shape[1]
    gc = gw // n_gg
    sw["ws"] = jnp.tril(w_s).astype(BF)
    sw["bs"] = jnp.repeat(jnp.swapaxes(b_s, 1, 2), gc, axis=2)
    ws_dec = w_s[:, :, :tdec, :tdec]
    diags = [jnp.pad(jnp.diagonal(ws_dec, offset=-k, axis1=2, axis2=3), ((0, 0), (0, 0), (k, 0)))
             for k in range(tdec)]
    sw["ws_coef"] = jnp.repeat(jnp.swapaxes(jnp.stack(diags, axis=1), 2, 3), gc, axis=3)
    sw["bs_dec"] = jnp.repeat(jnp.swapaxes(b_s[:, :, :tdec], 1, 2), gc, axis=2)
    return sw


def _layer_weights(l, sw, sizes, depth):
    pw, ql, kl, rp, gw, dm, nope, vh, d_ff = sizes
    lw = {k: _LayerView(v, l) for k, v in sw.items()}
    lw.update({"dims": (pw, ql, kl, rp, gw), "nope": nope, "scale": float((nope + rp) ** -0.5),
               "alpha": float((2.0 * depth) ** 0.25)})
    return lw


def _inv_count(pos, pw):
    gwidth = pw // len(POOL_WINDOWS)
    win = jnp.repeat(jnp.asarray(POOL_WINDOWS, jnp.int32), gwidth)
    cnt = jnp.minimum(pos[:, None] + 1, win[None, :]).astype(F32)
    return 1.0 / cnt


def kernel(x_prompt, x_sample, cache_ckv, cache_kpe, state_pool, state_ffn, page_table, w_in, pool_w, pool_scale,
           w_pool_out, q_norm_g, w_uq, kv_norm_g, w_uk, w_uv, w_mla_out, v_norm_g, v_norm_b, w_spatial, b_spatial,
           w_gmlp_out, w_out, ln1_g, ln1_b, w_up, conv_w, conv_b, w_down, ln2_g, ln2_b):
    p = dict(w_in=w_in, pool_w=pool_w, pool_scale=pool_scale, w_pool_out=w_pool_out, q_norm_g=q_norm_g, w_uq=w_uq,
             kv_norm_g=kv_norm_g, w_uk=w_uk, w_uv=w_uv, w_mla_out=w_mla_out, v_norm_g=v_norm_g, v_norm_b=v_norm_b,
             w_spatial=w_spatial, b_spatial=b_spatial, w_gmlp_out=w_gmlp_out, w_out=w_out, ln1_g=ln1_g,
             ln1_b=ln1_b, w_up=w_up, conv_w=conv_w, conv_b=conv_b, w_down=w_down, ln2_g=ln2_g, ln2_b=ln2_b)
    batch, seq, dm = x_prompt.shape
    dbatch, tdec, _ = x_sample.shape
    depth = w_in.shape[0]
    ps = cache_ckv.shape[2]
    n_pages = page_table.shape[1]
    past = n_pages * ps
    kl, rp = cache_ckv.shape[3], cache_kpe.shape[3]
    pw = state_pool.shape[3]
    ql = q_norm_g.shape[1]
    gw = v_norm_g.shape[1]
    nope, vh = w_uk.shape[3], w_uv.shape[3]
    d_ff = w_down.shape[1]
    sizes = (pw, ql, kl, rp, gw, dm, nope, vh, d_ff)
    assert tdec == SUBLANES and pw == 2 * LANES and kl % LANES == 0 and N_HEADS * rp == 2 * LANES
    assert nope * 2 == LANES and vh * 2 == LANES and d_ff % FF_CHUNK == 0

    tm = min(512, seq)
    tq = min(512, seq)
    tm_big = min(2 * tm, seq)
    rows_s = dbatch * tdec
    n_grp = next(g for g in (16, 8, 4, 2, 1) if n_pages % (DECODE_SLOTS * g) == 0)
    n_split = min(2, n_grp)
    assert seq % tm == 0 and tq == tm
    cache_kpe_t = jnp.swapaxes(cache_kpe, 2, 3)

    pos_p = jnp.arange(seq, dtype=jnp.int32)
    pos_s = past + jnp.arange(tdec, dtype=jnp.int32)

    def tables(pos, reps):
        c, s = _rope_tables(pos, rp // 2)
        zeros = jnp.zeros((pos.shape[0], LANES - rp), F32)
        ck, sk = jnp.concatenate([c, zeros], 1), jnp.concatenate([s, zeros], 1)
        cq, sq = jnp.tile(c, (1, N_HEADS)), jnp.tile(s, (1, N_HEADS))
        ic = _inv_count(pos, pw)
        return [jnp.tile(t, (reps, 1)) for t in (ck, sk, cq, sq, ic)]

    cos_kp, sin_kp, cos_qp, sin_qp, icnt_p = tables(pos_p, 1)
    cos_qpt, sin_qpt = cos_qp.T, sin_qp.T
    cos_ks, sin_ks, cos_qs, sin_qs, icnt_s = tables(pos_s, dbatch)

    xp = x_prompt.reshape(batch * seq, dm)
    xs = x_sample.reshape(rows_s, dm)
    outs = [[] for _ in range(9)]
    sw = _stacked_weights(p, sizes, tdec, dbatch)
    for l in range(depth):
        lw = _layer_weights(l, sw, sizes, depth)
        a_in, _, cqt, ckvn, kpe, kh, vt, u, vn = _in_proj(xp, lw, cos_kp, sin_kp, tm=tm, vn_dtype=BF)
        yb_pre = _prompt_attention(cqt, kh, vt, cos_qpt, sin_qpt, lw, batch=batch, seq=seq, tq=tq)
        xp = _merge_prompt(xp, a_in, icnt_p, u, vn, yb_pre, lw, tm=tm_big, seq=seq, n_parts=tm_big // tm)
        xp, tail_p = _ffn(xp, None, lw, tm=tm_big, seq=seq, n_seq=batch, n_parts=tm_big // tm)
        outs[0].append(ckvn.reshape(batch * seq // ps, ps, kl))
        outs[1].append(kpe.reshape(batch * seq // ps, ps, rp))
        outs[4].append(a_in.reshape(batch, seq, pw)[:, seq - POOL_KEEP:])
        outs[6].append(tail_p)
        a_s, cqn_s, _, ckvn_s, kpe_s, _, _, u_s, vn_s = _in_proj(xs, lw, cos_ks, sin_ks, tm=rows_s, vn_dtype=F32)
        qlat_s, qpe_s = _q_proj(cqn_s, lw, cos_qs, sin_qs)
        yb_s = _sample_attention(qlat_s, qpe_s, ckvn_s, kpe_s, cache_ckv, cache_kpe_t, page_table, l, lw,
                                 dbatch=dbatch, tdec=tdec, n_grp=n_grp, n_split=n_split)
        pool_ext = jnp.concatenate([jnp.zeros((dbatch, 1, pw), F32), state_pool[l],
                                    a_s.reshape(dbatch, tdec, pw)], axis=1)
        xs = _merge_sample(xs, pool_ext, icnt_s, u_s, vn_s, yb_s, lw, tdec=tdec)
        prefix = jnp.pad(state_ffn[l], ((0, 0), (SUBLANES - (CONV_W - 1), 0), (0, 0))).reshape(rows_s, 2 * d_ff)
        xs, tail_s = _ffn(xs, prefix, lw, tm=rows_s, seq=tdec, n_seq=dbatch)
        outs[2].append(ckvn_s.reshape(dbatch, tdec, kl))
        outs[3].append(kpe_s.reshape(dbatch, tdec, rp))
        outs[5].append(pool_ext[:, -POOL_KEEP:])
        outs[7].append(tail_s)
        outs[8].append(vn_s.reshape(dbatch, tdec, gw))
    stacked = [jnp.stack(o) for o in outs]
    new_ckv_p, new_kpe_p, new_ckv_s, new_kpe_s, pool_p, pool_s, ffn_p, ffn_s, gv_s = stacked
    return (xp.reshape(batch, seq, dm), xs.reshape(dbatch, tdec, dm), new_ckv_p, new_kpe_p, new_ckv_s, new_kpe_s,
            pool_p, pool_s, ffn_p, ffn_s, gv_s)
```
